```python
import jax
import jax.numpy as jnp
from jax import lax
import numpy as np


D_MODEL = 1024
BATCH = 2
SEQ = 16384
DEPTH = 4

GRID_W = 64
CTX_LEN = 256
CHUNK = 64
EPS = 1e-6
MIX_WIDTH = D_MODEL
HG_DK = 128
HG_DV = 128
HG_HEADS = (MIX_WIDTH // 2) // HG_DV
RET_DK = 128
RET_DV = 128
RET_HEADS = (MIX_WIDTH - HG_HEADS * HG_DV) // RET_DV
ROPE_BASE = 10000.0
N_EXPERTS = 16
CAPACITY_FACTOR = 2
EXPERT_FF = 2 * D_MODEL
IN_SPLITS = (HG_HEADS * HG_DK,) * 3 + (HG_HEADS * HG_DV,) * 2 + (RET_HEADS * RET_DK,) * 2 + (RET_HEADS * RET_DV,) * 2
IN_WIDTH = sum(IN_SPLITS)

kernel_name = 'hybrid_hgrn2_retention_ecmoe_prefix_dit'


def rms_norm(x, g=None):
    xf = x.astype(jnp.float32)
    y = xf * lax.rsqrt(jnp.mean(xf * xf, axis=-1, keepdims=True) + EPS)
    if g is not None:
        y = y * g.astype(jnp.float32)
    return y.astype(x.dtype)


def adaln(cond, w, b):
    m = jax.nn.silu(cond) @ w + b
    return jnp.split(m[:, None, :], 6, axis=-1)


def modulate(h, shift, scale):
    return h * (1 + scale) + shift


def to_bwd(z, lc):
    return jnp.concatenate([jnp.flip(z[:, :lc], axis=1), jnp.flip(z[:, lc:], axis=1)], axis=1)


def axial_rope_angles(rows, lc):
    n_freq = RET_DK // 4
    inv = ROPE_BASE ** (-jnp.arange(n_freq, dtype=jnp.float32) / n_freq)
    row = jnp.repeat(jnp.arange(rows, dtype=jnp.float32), GRID_W)
    col = jnp.tile(jnp.arange(GRID_W, dtype=jnp.float32), rows)
    lat = jnp.concatenate([row[:, None] * inv, col[:, None] * inv], axis=-1)
    return jnp.concatenate([jnp.zeros((lc, 2 * n_freq), jnp.float32), lat], axis=0)


def apply_rope(z, ang):
    cos = jnp.cos(ang)[None, :, None, :]
    sin = jnp.sin(ang)[None, :, None, :]
    z1, z2 = z[..., 0::2], z[..., 1::2]
    return jnp.stack([z1 * cos - z2 * sin, z1 * sin + z2 * cos], axis=-1).reshape(z.shape)


def to_chunks(z):
    b, l, h, d = z.shape
    return z.reshape(b, l // CHUNK, CHUNK, h, d).transpose(1, 0, 3, 2, 4)


def from_chunks(o):
    n, b, h, c, d = o.shape
    return o.transpose(1, 0, 3, 2, 4).reshape(b, n * c, h, d)


def hgrn2_scan(q, k, v, log_f):
    b, _, h, dk = q.shape
    dv = v.shape[-1]
    causal = jnp.tril(jnp.ones((CHUNK, CHUNK), dtype=bool))[:, :, None]

    def step(state, inp):
        qc, kc, vc, lfc = inp
        cum = jnp.cumsum(lfc, axis=2)
        diff = cum[:, :, :, None, :] - cum[:, :, None, :, :]
        dec = jnp.exp(jnp.where(causal, diff, -jnp.inf))
        scores = jnp.einsum('bhtk,bhsk,bhtsk->bhts', qc, kc, dec)
        out = (jnp.einsum('bhts,bhsv->bhtv', scores, vc)
               + jnp.einsum('bhtk,bhkv->bhtv', qc * jnp.exp(cum), state))
        last = cum[:, :, -1:, :]
        state = (jnp.exp(last[:, :, 0, :])[..., None] * state
                 + jnp.einsum('bhsk,bhsv->bhkv', kc * jnp.exp(last - cum), vc))
        return state, out

    s0 = jnp.zeros((b, h, dk, dv), jnp.float32)
    _, o = lax.scan(step, s0, (to_chunks(q), to_chunks(k), to_chunks(v), to_chunks(log_f)))
    return from_chunks(o)


def retention_scan(q, k, v, log_gamma):
    b, _, h, dk = q.shape
    dv = v.shape[-1]
    idx = jnp.arange(CHUNK, dtype=jnp.float32)
    rel = idx[:, None] - idx[None, :]
    dmat = jnp.where(rel >= 0, jnp.exp(jnp.maximum(rel, 0.0) * log_gamma[:, None, None]), 0.0)
    into = jnp.exp((idx + 1.0) * log_gamma[:, None])[None, :, :, None]
    tail = jnp.exp((CHUNK - 1.0 - idx) * log_gamma[:, None])[None, :, :, None]
    carry = jnp.exp(CHUNK * log_gamma)[None, :, None, None]

    def step(state, inp):
        qc, kc, vc = inp
        scores = jnp.einsum('bhtk,bhsk->bhts', qc, kc) * dmat[None]
        out = (jnp.einsum('bhts,bhsv->bhtv', scores, vc)
               + jnp.einsum('bhtk,bhkv->bhtv', qc, state) * into)
        state = carry * state + jnp.einsum('bhsk,bhsv->bhkv', kc * tail, vc)
        return state, out

    s0 = jnp.zeros((b, h, dk, dv), jnp.float32)
    _, o = lax.scan(step, s0, (to_chunks(q), to_chunks(k), to_chunks(v)))
    return from_chunks(o)


def token_mixers(h, lc, w_in, lb, onorm, log_gamma, ang):
    b, l, _ = h.shape
    p = (h @ w_in).astype(jnp.float32)
    bounds = [int(i) for i in np.cumsum(IN_SPLITS)[:-1]]
    hq, hff, hfb, hi, hgt, rq, rk, rv, rgt = jnp.split(p, bounds, axis=-1)

    def split_heads(z, d):
        return z.reshape(b, l, z.shape[-1] // d, d)

    lb = lb.reshape(HG_HEADS, HG_DK)
    log_lb, log_1mlb = jnp.log(lb), jnp.log1p(-lb)
    q = split_heads(hq, HG_DK)
    i = split_heads(hi, HG_DV)
    lf_f = jnp.logaddexp(log_lb, log_1mlb + jax.nn.log_sigmoid(split_heads(hff, HG_DK)))
    lf_b = to_bwd(jnp.logaddexp(log_lb, log_1mlb + jax.nn.log_sigmoid(split_heads(hfb, HG_DK))), lc)
    o_hg = (hgrn2_scan(q, -jnp.expm1(lf_f), i, lf_f)
            + to_bwd(hgrn2_scan(to_bwd(q, lc), -jnp.expm1(lf_b), to_bwd(i, lc), lf_b), lc))
    o_hg = rms_norm(o_hg, onorm) * jax.nn.silu(split_heads(hgt, HG_DV))

    rq = apply_rope(split_heads(rq, RET_DK), ang)
    rk = apply_rope(split_heads(rk, RET_DK), ang) * RET_DK ** -0.5
    rv = split_heads(rv, RET_DV)
    o_ret = (retention_scan(rq, rk, rv, log_gamma[0])
             + to_bwd(retention_scan(to_bwd(rq, lc), to_bwd(rk, lc), to_bwd(rv, lc), log_gamma[1]), lc))
    o_ret = rms_norm(o_ret) * jax.nn.silu(split_heads(rgt, RET_DV))

    out = jnp.concatenate([o_hg.reshape(b, l, -1), o_ret.reshape(b, l, -1)], axis=-1)
    return out.astype(h.dtype)


def expert_choice_ffn(h, w_router, w_gate, w_up, w_down):
    b, n, d = h.shape
    cap = CAPACITY_FACTOR * n // N_EXPERTS
    aff = jax.nn.softmax((h @ w_router).astype(jnp.float32), axis=-1)
    gates, idx = lax.top_k(jnp.swapaxes(aff, 1, 2), cap)
    flat = idx.reshape(b, -1)
    bidx = jnp.arange(b)[:, None]
    xs = h[bidx, flat].reshape(b, N_EXPERTS, cap, d)
    hid = (jax.nn.silu(jnp.einsum('becd,edf->becf', xs, w_gate))
           * jnp.einsum('becd,edf->becf', xs, w_up))
    ys = jnp.einsum('becf,efd->becd', hid, w_down) * gates[..., None].astype(h.dtype)
    return jnp.zeros_like(h).at[bidx, flat].add(ys.reshape(b, -1, d))


def setup_inputs(seed: int = 0) -> dict:
    key = jax.random.key(seed)
    ks = jax.random.split(key, 20)
    nrm = jax.random.normal
    s = D_MODEL ** -0.5
    base_decay = jnp.log(-jnp.log1p(-(2.0 ** (-5.0 - jnp.arange(RET_HEADS, dtype=jnp.float32)))))
    return {
        'x': nrm(ks[0], (BATCH, SEQ, D_MODEL), jnp.float32),
        'c': nrm(ks[1], (BATCH, D_MODEL), jnp.float32),
        'ctx': nrm(ks[2], (BATCH, CTX_LEN, D_MODEL), jnp.float32),
        'c_ctx': nrm(ks[3], (D_MODEL,), jnp.float32),
        'w_ada': nrm(ks[4], (DEPTH, D_MODEL, 6 * D_MODEL), jnp.float32) * (0.5 * s),
        'b_ada': nrm(ks[5], (DEPTH, 6 * D_MODEL), jnp.float32) * 0.02,
        'g_mix': 1.0 + 0.02 * nrm(ks[6], (DEPTH, D_MODEL), jnp.float32),
        'g_ffn': 1.0 + 0.02 * nrm(ks[7], (DEPTH, D_MODEL), jnp.float32),
        'w_in': nrm(ks[8], (DEPTH, D_MODEL, IN_WIDTH), jnp.float32) * s,
        'w_out': nrm(ks[9], (DEPTH, MIX_WIDTH, D_MODEL), jnp.float32) * MIX_WIDTH ** -0.5,
        'hg_lb_logits': nrm(ks[10], (DEPTH, HG_HEADS * HG_DK), jnp.float32) * 0.5,
        'hg_onorm': 1.0 + 0.02 * nrm(ks[11], (DEPTH, HG_DV), jnp.float32),
        'ret_decay': base_decay[None, None, :] + 0.01 * nrm(ks[12], (DEPTH, 2, RET_HEADS), jnp.float32),
        'w_router': nrm(ks[13], (DEPTH, D_MODEL, N_EXPERTS), jnp.float32) * s,
        'w_e_gate': nrm(ks[14], (DEPTH, N_EXPERTS, D_MODEL, EXPERT_FF), jnp.float32) * s,
        'w_e_up': nrm(ks[15], (DEPTH, N_EXPERTS, D_MODEL, EXPERT_FF), jnp.float32) * s,
        'w_e_down': nrm(ks[16], (DEPTH, N_EXPERTS, EXPERT_FF, D_MODEL), jnp.float32) * EXPERT_FF ** -0.5,
        'g_final': 1.0 + 0.02 * nrm(ks[17], (D_MODEL,), jnp.float32),
    }


def reference(x, c, ctx, c_ctx, w_ada, b_ada, g_mix, g_ffn, w_in, w_out, hg_lb_logits, hg_onorm,
              ret_decay, w_router, w_e_gate, w_e_up, w_e_down, g_final):
    n = x.shape[1]
    lc = ctx.shape[1]
    rows = n // GRID_W
    ang = axial_rope_angles(rows, lc)
    gamma_cum = jnp.cumsum(jax.nn.softmax(hg_lb_logits.astype(jnp.float32), axis=0), axis=0)
    lbs = gamma_cum - gamma_cum[0:1]
    log_gammas = -jnp.exp(ret_decay.astype(jnp.float32))
    xl, xc = x, ctx
    for l in range(DEPTH):
        last = l == DEPTH - 1
        sh1, sc1, gt1, sh2, sc2, gt2 = adaln(c, w_ada[l], b_ada[l])
        csh1, csc1, cgt1, csh2, csc2, cgt2 = adaln(c_ctx[None], w_ada[l], b_ada[l])
        h = jnp.concatenate([modulate(rms_norm(xc, g_mix[l]), csh1, csc1),
                             modulate(rms_norm(xl, g_mix[l]), sh1, sc1)], axis=1)
        mix = token_mixers(h, lc, w_in[l], lbs[l], hg_onorm[l], log_gammas[l], ang)
        xl = xl + gt1 * (mix[:, lc:] @ w_out[l])
        xl = xl + gt2 * expert_choice_ffn(modulate(rms_norm(xl, g_ffn[l]), sh2, sc2),
                                          w_router[l], w_e_gate[l], w_e_up[l], w_e_down[l])
        if not last:
            xc = xc + cgt1 * (mix[:, :lc] @ w_out[l])
            xc = xc + cgt2 * expert_choice_ffn(modulate(rms_norm(xc, g_ffn[l]), csh2, csc2),
                                               w_router[l], w_e_gate[l], w_e_up[l], w_e_down[l])
    return rms_norm(xl, g_final)
```

```python
import functools

import numpy as np
import jax
import jax.numpy as jnp
from jax import lax
from jax.experimental import pallas as pl
from jax.experimental.pallas import tpu as pltpu

F32 = jnp.float32
BF16 = jnp.bfloat16

EPS = 1e-6
GRID_W = 64
ROPE_BASE = 10000.0
N_EXPERTS = 16
CAPACITY_FACTOR = 2
HEAD_DIM = 128
LANES = 128
SCAN_CHUNK = 128
ROW_TILE = 256
VMEM_LIMIT = 56 * 1024 * 1024

N_LEVELS = int(np.log2(SCAN_CHUNK))
W_ROWS = (N_LEVELS + 2) * SCAN_CHUNK + 8


def _cparams(sem):
    return pltpu.CompilerParams(dimension_semantics=sem, vmem_limit_bytes=VMEM_LIMIT)


def _nt_dot(a, b):
    return lax.dot_general(a, b, (((1,), (1,)), ((), ())), preferred_element_type=F32)


def _tn_dot(a, b):
    return lax.dot_general(a, b, (((0,), (0,)), ((), ())), preferred_element_type=F32)


def _sigmoid(x):
    return 1.0 / (1.0 + jnp.exp(-x))


def _silu(x):
    return x * _sigmoid(x)


def _adaln_kernel(cond_ref, w_ref, b_ref, o_ref):
    s = _silu(cond_ref[...])
    o_ref[0] = jnp.dot(s.astype(BF16), w_ref[0].astype(BF16), preferred_element_type=F32) + b_ref[0]


def _adaln(cond8, w_ada, b_ada):
    depth, d, n6 = w_ada.shape
    tn = n6 // 4
    return pl.pallas_call(
        _adaln_kernel,
        grid=(depth, n6 // tn),
        in_specs=[pl.BlockSpec((8, d), lambda l, j: (0, 0)),
                  pl.BlockSpec((1, d, tn), lambda l, j: (l, 0, j)),
                  pl.BlockSpec((1, 1, tn), lambda l, j: (l, 0, j))],
        out_specs=pl.BlockSpec((1, 8, tn), lambda l, j: (l, 0, j)),
        out_shape=jax.ShapeDtypeStruct((depth, 8, n6), F32),
        compiler_params=_cparams(("arbitrary", "arbitrary")),
        name="adaln",
    )(cond8, w_ada, b_ada.reshape(depth, 1, n6))


def _rms(x):
    return x * lax.rsqrt(jnp.mean(x * x, axis=-1, keepdims=True) + EPS)


def _inproj_kernel(has_moe, *refs):
    if has_moe:
        x_ref, moe_ref, gt2_ref, mod_ref, g_ref, w_ref, xo_ref, p_ref = refs
        x = x_ref[0] + gt2_ref[0][5:6] * moe_ref[0]
        xo_ref[0] = x
    else:
        x_ref, mod_ref, g_ref, w_ref, p_ref = refs
        x = x_ref[0]
    mod = mod_ref[0]
    h = (_rms(x) * g_ref[...]) * (1.0 + mod[1:2]) + mod[0:1]
    p_ref[0] = jnp.dot(h.astype(BF16), w_ref[...], preferred_element_type=F32)


def _seg_spec(d, ctx_tiles):
    return pl.BlockSpec((1, 6, d), lambda b, t: (2 * b + jnp.where(t < ctx_tiles, 0, 1), 0, 0))


def _inproj(x, moe, modtab_prev, modtab, g, w_bf16, ctx_len):
    b, l, d = x.shape
    n = w_bf16.shape[1]
    tm = ROW_TILE
    ctx_tiles = ctx_len // tm
    row = pl.BlockSpec((1, tm, d), lambda b, t: (b, t, 0))
    has_moe = moe is not None
    in_specs = [row]
    args = [x]
    if has_moe:
        in_specs += [row, _seg_spec(d, ctx_tiles)]
        args += [moe, modtab_prev]
    in_specs += [_seg_spec(d, ctx_tiles),
                 pl.BlockSpec((1, d), lambda b, t: (0, 0)),
                 pl.BlockSpec((d, n), lambda b, t: (0, 0))]
    args += [modtab, g.reshape(1, d), w_bf16]
    p_spec = pl.BlockSpec((1, tm, n), lambda b, t: (b, t, 0))
    p_shape = jax.ShapeDtypeStruct((b, l, n), F32)
    if has_moe:
        out_specs = [row, p_spec]
        out_shape = [jax.ShapeDtypeStruct((b, l, d), F32), p_shape]
    else:
        out_specs = p_spec
        out_shape = p_shape
    res = pl.pallas_call(
        functools.partial(_inproj_kernel, has_moe),
        grid=(b, l // tm),
        in_specs=in_specs, out_specs=out_specs, out_shape=out_shape,
        compiler_params=_cparams(("arbitrary", "arbitrary")),
        name="inproj",
    )(*args)
    if has_moe:
        return res[0], res[1]
    return x, res


def _bwd_chunk(j, ctx_chunks, n_chunks):
    return jnp.where(j < ctx_chunks, ctx_chunks - 1 - j, n_chunks - 1 + ctx_chunks - j)


def _decay_tables():
    c = SCAN_CHUNK
    w = np.zeros((W_ROWS, c), np.float32)
    masks = np.zeros((N_LEVELS + 1, c, c), np.float32)
    for lev in range(N_LEVELS):
        m = c >> (lev + 1)
        for t in range(c):
            g0 = (t // (2 * m)) * 2 * m
            bnd = g0 + m - 1
            if t > bnd:
                w[lev * c + t, bnd + 1:t + 1] = 1.0
                masks[lev, t, g0:bnd + 1] = 1.0
            else:
                w[lev * c + t, t + 1:bnd + 1] = 1.0
    masks[N_LEVELS] = np.eye(c, dtype=np.float32)
    base = N_LEVELS * c
    for t in range(c):
        w[base + t, :t + 1] = 1.0
        w[base + c + t, t + 1:] = 1.0
    w[base + 2 * c:, :] = 1.0
    w_b = w.copy()
    w_b[:base + 2 * c] = w[:base + 2 * c].reshape(-1, c, c)[:, ::-1, ::-1].reshape(-1, c)
    masks_b = masks[:, ::-1, ::-1]
    return (jnp.asarray(np.stack([w, w_b]), BF16), jnp.asarray(np.stack([masks, masks_b]), F32))


def _hgrn_kernel(n_heads, qf_ref, ff_ref, if_ref, qb_ref, fb_ref, ib_ref, lb_ref, w_ref, mask_ref,
                 of_ref, ob_ref, st_ref):
    c = SCAN_CHUNK

    @pl.when(pl.program_id(1) == 0)
    def _():
        st_ref[...] = jnp.zeros_like(st_ref)

    row = lax.broadcasted_iota(jnp.int32, (c, HEAD_DIM), 0)
    dirs = ((qf_ref, ff_ref, if_ref, of_ref), (qb_ref, fb_ref, ib_ref, ob_ref))
    for d, (q_ref, f_ref, i_ref, o_ref) in enumerate(dirs):
        for h in range(n_heads):
            sl = slice(h * HEAD_DIM, (h + 1) * HEAD_DIM)
            q = q_ref[0, :, sl]
            xf = f_ref[0, :, sl]
            v = i_ref[0, :, sl].astype(BF16)
            b_ = lb_ref[1:2, sl] + (jnp.minimum(xf, 0.0) - jnp.log1p(jnp.exp(-jnp.abs(xf))))
            a_ = lb_ref[0:1, sl]
            lf = jnp.maximum(a_, b_) + jnp.log1p(jnp.exp(-jnp.abs(a_ - b_)))
            k = 1.0 - jnp.exp(lf)
            hi = lf.astype(BF16)
            r1 = lf - hi.astype(F32)
            mid = r1.astype(BF16)
            lo = (r1 - mid.astype(F32)).astype(BF16)
            ex3 = jnp.dot(w_ref[d], jnp.concatenate([hi, mid, lo], axis=1), preferred_element_type=F32)
            e = jnp.exp(ex3[:, :HEAD_DIM] + ex3[:, HEAD_DIM:2 * HEAD_DIM] + ex3[:, 2 * HEAD_DIM:])
            qb16 = q.astype(BF16)
            kb16 = k.astype(BF16)
            scores = _nt_dot(qb16, kb16) * mask_ref[d, N_LEVELS]
            for lev in range(N_LEVELS):
                shift = N_LEVELS - 1 - lev
                is_q = ((row >> shift) & 1) == (1 - d)
                z = (jnp.where(is_q, q, k) * e[lev * c:(lev + 1) * c]).astype(BF16)
                scores = scores + _nt_dot(z, z) * mask_ref[d, lev]
            base = N_LEVELS * c
            st = st_ref[d, h]
            out = jnp.dot(scores.astype(BF16), v, preferred_element_type=F32)
            out = out + _nt_dot((q * e[base:base + c]).astype(BF16), st.astype(BF16))
            o_ref[0, :, sl] = out
            kdec = (k * e[base + c:base + 2 * c]).astype(BF16)
            st_ref[d, h] = st * e[base + 2 * c:base + 2 * c + 1] + _tn_dot(v, kdec)


def _hgrn_scan(p, lb_tab, w_tab, mask_tab, ctx_len, width):
    b, l, _ = p.shape
    c = SCAN_CHUNK
    n_chunks = l // c
    ctx_chunks = ctx_len // c
    n_heads = width // HEAD_DIM

    def fwd(g):
        return pl.BlockSpec((1, c, width), lambda b, j: (b, j, g))

    def bwd(g):
        return pl.BlockSpec((1, c, width), lambda b, j: (b, _bwd_chunk(j, ctx_chunks, n_chunks), g))

    out_shape = jax.ShapeDtypeStruct((b, l, width), F32)
    return pl.pallas_call(
        functools.partial(_hgrn_kernel, n_heads),
        grid=(b, n_chunks),
        in_specs=[fwd(0), fwd(1), fwd(3), bwd(0), bwd(2), bwd(3),
                  pl.BlockSpec((2, width), lambda b, j: (0, 0)),
                  pl.BlockSpec(w_tab.shape, lambda b, j: (0, 0, 0)),
                  pl.BlockSpec(mask_tab.shape, lambda b, j: (0, 0, 0, 0))],
        out_specs=[fwd(0), bwd(0)],
        out_shape=[out_shape, out_shape],
        scratch_shapes=[pltpu.VMEM((2, n_heads, HEAD_DIM, HEAD_DIM), F32)],
        compiler_params=_cparams(("arbitrary", "arbitrary")),
        name="hgrn_scan",
    )(p, p, p, p, p, p, lb_tab, w_tab, mask_tab)


def _rope(z, cos_rep, sin_signed, even_lane):
    partner = jnp.where(even_lane, pltpu.roll(z, LANES - 1, 1), pltpu.roll(z, 1, 1))
    return z * cos_rep + partner * sin_signed


def _ret_kernel(n_heads, lg_ref, qf_ref, kf_ref, vf_ref, cf_ref, sf_ref,
                qb_ref, kb_ref, vb_ref, cb_ref, sb_ref, of_ref, ob_ref, st_ref):
    c = SCAN_CHUNK
    k_scale = HEAD_DIM ** -0.5

    @pl.when(pl.program_id(1) == 0)
    def _():
        st_ref[...] = jnp.zeros_like(st_ref)

    ti = lax.broadcasted_iota(jnp.int32, (c, c), 0)
    si = lax.broadcasted_iota(jnp.int32, (c, c), 1)
    rowf = lax.broadcasted_iota(jnp.int32, (c, HEAD_DIM), 0).astype(F32)
    even_lane = (lax.broadcasted_iota(jnp.int32, (c, HEAD_DIM), 1) & 1) == 0
    dirs = ((qf_ref, kf_ref, vf_ref, cf_ref, sf_ref, of_ref), (qb_ref, kb_ref, vb_ref, cb_ref, sb_ref, ob_ref))
    for d, (q_ref, k_ref, v_ref, c_ref, s_ref, o_ref) in enumerate(dirs):
        cos_rep = c_ref[...]
        sin_signed = s_ref[...]
        rel = (ti - si) if d == 0 else (si - ti)
        relf = jnp.maximum(rel, 0).astype(F32)
        for h in range(n_heads):
            sl = slice(h * HEAD_DIM, (h + 1) * HEAD_DIM)
            lg = lg_ref[d, h]
            dmat = jnp.where(rel >= 0, jnp.exp(relf * lg), 0.0)
            if d == 0:
                into = jnp.exp((rowf + 1.0) * lg)
                tail = jnp.exp((c - 1.0 - rowf) * lg)
            else:
                into = jnp.exp((c - rowf) * lg)
                tail = jnp.exp(rowf * lg)
            q = _rope(q_ref[0, :, sl], cos_rep, sin_signed, even_lane)
            k = _rope(k_ref[0, :, sl], cos_rep, sin_signed, even_lane) * k_scale
            v = v_ref[0, :, sl].astype(BF16)
            qb16 = q.astype(BF16)
            scores = _nt_dot(qb16, k.astype(BF16)) * dmat
            st = st_ref[d, h]
            out = jnp.dot(scores.astype(BF16), v, preferred_element_type=F32)
            out = out + _nt_dot(qb16, st.astype(BF16)) * into
            o_ref[0, :, sl] = out
            st_ref[d, h] = st * jnp.exp(c * lg) + _tn_dot(v, (k * tail).astype(BF16))


def _ret_scan(p, lg, cos_rep, sin_signed, ctx_len, width, first_group):
    b, l, _ = p.shape
    c = SCAN_CHUNK
    n_chunks = l // c
    ctx_chunks = ctx_len // c
    n_heads = width // HEAD_DIM

    def fwd(g):
        return pl.BlockSpec((1, c, width), lambda b, j, lg: (b, j, g))

    def bwd(g):
        return pl.BlockSpec((1, c, width), lambda b, j, lg: (b, _bwd_chunk(j, ctx_chunks, n_chunks), g))

    tab_f = pl.BlockSpec((c, HEAD_DIM), lambda b, j, lg: (j, 0))
    tab_b = pl.BlockSpec((c, HEAD_DIM), lambda b, j, lg: (_bwd_chunk(j, ctx_chunks, n_chunks), 0))
    g0 = first_group
    out_shape = jax.ShapeDtypeStruct((b, l, width), F32)
    return pl.pallas_call(
        functools.partial(_ret_kernel, n_heads),
        grid_spec=pltpu.PrefetchScalarGridSpec(
            num_scalar_prefetch=1,
            grid=(b, n_chunks),
            in_specs=[fwd(g0), fwd(g0 + 1), fwd(g0 + 2), tab_f, tab_f,
                      bwd(g0), bwd(g0 + 1), bwd(g0 + 2), tab_b, tab_b],
            out_specs=[fwd(0), bwd(0)],
            scratch_shapes=[pltpu.VMEM((2, n_heads, HEAD_DIM, HEAD_DIM), F32)]),
        out_shape=[out_shape, out_shape],
        compiler_params=_cparams(("arbitrary", "arbitrary")),
        name="ret_scan",
    )(lg, p, p, p, cos_rep, sin_signed, p, p, p, cos_rep, sin_signed)


def _outproj_kernel(n_hg, n_ret, x_ref, hf_ref, hb_ref, rf_ref, rb_ref, hg_ref, rg_ref, mod_ref,
                    on_ref, wo_ref, g2_ref, wrh_ref, wrl_ref, xo_ref, h2_ref, aff_ref):
    mod = mod_ref[0]
    o_hg = hf_ref[0] + hb_ref[0]
    o_rt = rf_ref[0] + rb_ref[0]
    parts = []
    for h in range(n_hg):
        sl = slice(h * HEAD_DIM, (h + 1) * HEAD_DIM)
        parts.append((_rms(o_hg[:, sl]) * on_ref[...]) * _silu(hg_ref[0, :, sl]))
    for h in range(n_ret):
        sl = slice(h * HEAD_DIM, (h + 1) * HEAD_DIM)
        parts.append(_rms(o_rt[:, sl]) * _silu(rg_ref[0, :, sl]))
    mix = jnp.concatenate(parts, axis=1).astype(BF16)
    x = x_ref[0] + mod[2:3] * jnp.dot(mix, wo_ref[...], preferred_element_type=F32)
    xo_ref[0] = x
    h2 = (_rms(x) * g2_ref[...]) * (1.0 + mod[4:5]) + mod[3:4]
    h2_ref[0] = h2.astype(BF16)
    hh = h2.astype(BF16)
    hl = (h2 - hh.astype(F32)).astype(BF16)
    logits = _nt_dot(wrh_ref[...], hh) + _nt_dot(wrh_ref[...], hl) + _nt_dot(wrl_ref[...], hh)
    mx = jnp.max(logits, axis=0, keepdims=True)
    ex = jnp.exp(logits - mx)
    aff_ref[0] = ex / jnp.sum(ex, axis=0, keepdims=True)


def _outproj(x, hf, hb, rf, rb, p, modtab, onorm, wo_bf16, g2, w_router, ctx_len, hg_width):
    b, l, d = x.shape
    tm = ROW_TILE
    ctx_tiles = ctx_len // tm
    n_hg = hg_width // HEAD_DIM
    ret_width = d - hg_width
    n_ret = ret_width // HEAD_DIM
    n_e = w_router.shape[1]
    wrt = w_router.T
    wrh = wrt.astype(BF16)
    wrl = (wrt - wrh.astype(F32)).astype(BF16)
    row = pl.BlockSpec((1, tm, d), lambda b, t: (b, t, 0))
    hrow = pl.BlockSpec((1, tm, hg_width), lambda b, t: (b, t, 0))
    rrow = pl.BlockSpec((1, tm, ret_width), lambda b, t: (b, t, 0))
    full = lambda shape: pl.BlockSpec(shape, lambda b, t: tuple(0 for _ in shape))
    return pl.pallas_call(
        functools.partial(_outproj_kernel, n_hg, n_ret),
        grid=(b, l // tm),
        in_specs=[row, hrow, hrow, rrow, rrow,
                  pl.BlockSpec((1, tm, hg_width), lambda b, t: (b, t, 4)),
                  pl.BlockSpec((1, tm, ret_width), lambda b, t: (b, t, 8)),
                  _seg_spec(d, ctx_tiles),
                  full((1, HEAD_DIM)), full((d, d)), full((1, d)), full((n_e, d)), full((n_e, d))],
        out_specs=[row, row, pl.BlockSpec((1, n_e, tm), lambda b, t: (b, 0, t))],
        out_shape=[jax.ShapeDtypeStruct((b, l, d), F32),
                   jax.ShapeDtypeStruct((b, l, d), BF16),
                   jax.ShapeDtypeStruct((b, n_e, l), F32)],
        compiler_params=_cparams(("arbitrary", "arbitrary")),
        name="outproj",
    )(x, hf, hb, rf, rb, p, p, modtab, onorm.reshape(1, HEAD_DIM), wo_bf16, g2.reshape(1, d), wrh, wrl)


def _ffn_kernel(x_ref, gate_ref, wg_ref, wu_ref, wd_ref, o_ref):
    x = x_ref[0]
    a = jnp.dot(x, wg_ref[0], preferred_element_type=F32)
    u = jnp.dot(x, wu_ref[0], preferred_element_type=F32)
    hid = (_silu(a) * u).astype(BF16)
    o_ref[0] = jnp.dot(hid, wd_ref[0], preferred_element_type=F32) * gate_ref[0]


def _row_tile(rows, cap=512):
    best = 16
    for t in range(16, cap + 1, 16):
        if rows % t == 0:
            best = t
    return best


def _expert_ffn(xs, gates, wg, wu, wd):
    n_e, rows, d = xs.shape
    ff = wg.shape[2]
    tm = _row_tile(rows)
    return pl.pallas_call(
        _ffn_kernel,
        grid=(n_e, rows // tm),
        in_specs=[pl.BlockSpec((1, tm, d), lambda e, t: (e, t, 0)),
                  pl.BlockSpec((1, tm, 1), lambda e, t: (e, t, 0)),
                  pl.BlockSpec((1, d, ff), lambda e, t: (e, 0, 0)),
                  pl.BlockSpec((1, d, ff), lambda e, t: (e, 0, 0)),
                  pl.BlockSpec((1, ff, d), lambda e, t: (e, 0, 0))],
        out_specs=pl.BlockSpec((1, tm, d), lambda e, t: (e, t, 0)),
        out_shape=jax.ShapeDtypeStruct((n_e, rows, d), F32),
        compiler_params=_cparams(("arbitrary", "arbitrary")),
        name="expert_ffn",
    )(xs, gates, wg, wu, wd)


def _final_kernel(x_ref, moe_ref, gt2_ref, g_ref, o_ref):
    x = x_ref[0] + gt2_ref[0][5:6] * moe_ref[0]
    o_ref[0] = _rms(x) * g_ref[...]


def _final(x, moe, modtab, g, ctx_len):
    b, l, d = x.shape
    tm = ROW_TILE
    ctx_tiles = ctx_len // tm
    n_t = (l - ctx_len) // tm
    row = pl.BlockSpec((1, tm, d), lambda b, t: (b, t + ctx_tiles, 0))
    return pl.pallas_call(
        _final_kernel,
        grid=(b, n_t),
        in_specs=[row, row,
                  pl.BlockSpec((1, 6, d), lambda b, t: (2 * b + 1, 0, 0)),
                  pl.BlockSpec((1, d), lambda b, t: (0, 0))],
        out_specs=pl.BlockSpec((1, tm, d), lambda b, t: (b, t, 0)),
        out_shape=jax.ShapeDtypeStruct((b, l - ctx_len, d), F32),
        compiler_params=_cparams(("arbitrary", "arbitrary")),
        name="final_norm",
    )(x, moe, modtab, g.reshape(1, d))


def _rope_tables(rows, ctx_len):
    n_freq = HEAD_DIM // 4
    inv = ROPE_BASE ** (-jnp.arange(n_freq, dtype=F32) / n_freq)
    r = jnp.repeat(jnp.arange(rows, dtype=F32), GRID_W)
    cc = jnp.tile(jnp.arange(GRID_W, dtype=F32), rows)
    lat = jnp.concatenate([r[:, None] * inv, cc[:, None] * inv], axis=-1)
    ang = jnp.concatenate([jnp.zeros((ctx_len, 2 * n_freq), F32), lat], axis=0)
    cos = jnp.cos(ang)
    sin = jnp.sin(ang)
    cos_rep = jnp.repeat(cos, 2, axis=-1)
    sin_signed = jnp.stack([-sin, sin], axis=-1).reshape(ang.shape[0], HEAD_DIM)
    return cos_rep, sin_signed


def _route(aff_t, cap):
    return lax.top_k(aff_t, cap)


def kernel(x, c, ctx, c_ctx, w_ada, b_ada, g_mix, g_ffn, w_in, w_out, hg_lb_logits, hg_onorm,
           ret_decay, w_router, w_e_gate, w_e_up, w_e_down, g_final):
    bsz, n, d = x.shape
    lc = ctx.shape[1]
    depth = w_ada.shape[0]
    l = lc + n
    hg_width = hg_lb_logits.shape[1]
    ret_width = d - hg_width
    n_e = w_router.shape[2]
    assert lc % ROW_TILE == 0 and n % ROW_TILE == 0 and lc % SCAN_CHUNK == 0 and n % SCAN_CHUNK == 0
    assert hg_width % HEAD_DIM == 0 and ret_width % HEAD_DIM == 0 and n % GRID_W == 0
    assert w_in.shape[2] == 5 * hg_width + 4 * ret_width and hg_width == ret_width

    cos_rep, sin_signed = _rope_tables(n // GRID_W, lc)
    w_tab, mask_tab = _decay_tables()

    gamma_cum = jnp.cumsum(jax.nn.softmax(hg_lb_logits.astype(F32), axis=0), axis=0)
    lbs = gamma_cum - gamma_cum[0:1]
    lb_tabs = jnp.stack([jnp.log(lbs), jnp.log1p(-lbs)], axis=1)
    log_gammas = -jnp.exp(ret_decay.astype(F32))

    cond8 = jnp.zeros((8, d), F32).at[:bsz].set(c).at[bsz].set(c_ctx)
    mods = _adaln(cond8, w_ada, b_ada).reshape(depth, 8, 6, d)
    modtabs = jnp.stack([jnp.broadcast_to(mods[:, bsz:bsz + 1], (depth, bsz, 6, d)), mods[:, :bsz]],
                        axis=2).reshape(depth, 2 * bsz, 6, d)

    cap_lat = CAPACITY_FACTOR * n // n_e
    cap_ctx = CAPACITY_FACTOR * lc // n_e
    boff = (jnp.arange(bsz, dtype=jnp.int32) * l)[:, None, None]

    xs = jnp.concatenate([ctx, x], axis=1)
    moe = None
    for layer in range(depth):
        last = layer == depth - 1
        xs, p = _inproj(xs, moe, modtabs[layer - 1] if layer else None, modtabs[layer], g_mix[layer],
                        w_in[layer].astype(BF16), lc)
        hf, hb = _hgrn_scan(p, lb_tabs[layer], w_tab, mask_tab, lc, hg_width)
        rf, rb = _ret_scan(p, log_gammas[layer], cos_rep, sin_signed, lc, ret_width, 5)
        xs, h2, aff_t = _outproj(xs, hf, hb, rf, rb, p, modtabs[layer], hg_onorm[layer],
                                 w_out[layer].astype(BF16), g_ffn[layer], w_router[layer], lc, hg_width)
        g_lat, i_lat = _route(aff_t[:, :, lc:], cap_lat)
        rows = [(i_lat + lc + boff), ]
        gts = [g_lat]
        if not last:
            g_ctx, i_ctx = _route(aff_t[:, :, :lc], cap_ctx)
            rows.append(i_ctx + boff)
            gts.append(g_ctx)
        flat = jnp.concatenate([r.transpose(1, 0, 2).reshape(n_e, -1) for r in rows], axis=1)
        gate = jnp.concatenate([g.transpose(1, 0, 2).reshape(n_e, -1) for g in gts], axis=1)
        n_rows = flat.shape[1]
        pad = (-n_rows) % 16
        if pad:
            flat = jnp.pad(flat, ((0, 0), (0, pad)))
            gate = jnp.pad(gate, ((0, 0), (0, pad)))
        h2f = h2.reshape(bsz * l, d)
        gathered = h2f[flat.reshape(-1)].reshape(n_e, n_rows + pad, d)
        ys = _expert_ffn(gathered, gate[:, :, None], w_e_gate[layer].astype(BF16),
                         w_e_up[layer].astype(BF16), w_e_down[layer].astype(BF16))
        moe = jnp.zeros((bsz * l, d), F32).at[flat.reshape(-1)].add(ys.reshape(-1, d)).reshape(bsz, l, d)
    return _final(xs, moe, modtabs[depth - 1], g_final, lc)
```

```python
import functools

import numpy as np
import jax
import jax.numpy as jnp
from jax import lax
from jax.experimental import pallas as pl
from jax.experimental.pallas import tpu as pltpu

F32 = jnp.float32
BF16 = jnp.bfloat16

EPS = 1e-6
GRID_W = 64
ROPE_BASE = 10000.0
N_EXPERTS = 16
CAPACITY_FACTOR = 2
HEAD_DIM = 128
LANES = 128
SCAN_CHUNK = 128
ROW_TILE = 256
VMEM_LIMIT = 56 * 1024 * 1024

N_LEVELS = int(np.log2(SCAN_CHUNK))
SMALL_HALVES = (4, 2)
W_ROWS = (1 + len(SMALL_HALVES)) * SCAN_CHUNK
LOG2E = float(np.log2(np.e))


def _cparams(sem):
    return pltpu.CompilerParams(dimension_semantics=sem, vmem_limit_bytes=VMEM_LIMIT)


def _nt_dot(a, b):
    return lax.dot_general(a, b, (((1,), (1,)), ((), ())), preferred_element_type=F32)


def _tn_dot(a, b):
    return lax.dot_general(a, b, (((0,), (0,)), ((), ())), preferred_element_type=F32)


def _sigmoid(x):
    return 1.0 / (1.0 + jnp.exp(-x))


def _silu(x):
    return x * _sigmoid(x)


def _adaln_kernel(cond_ref, w_ref, b_ref, o_ref):
    s = _silu(cond_ref[...])
    o_ref[0] = jnp.dot(s.astype(BF16), w_ref[0].astype(BF16), preferred_element_type=F32) + b_ref[0]


def _adaln(cond8, w_ada, b_ada):
    depth, d, n6 = w_ada.shape
    tn = n6 // 4
    return pl.pallas_call(
        _adaln_kernel,
        grid=(depth, n6 // tn),
        in_specs=[pl.BlockSpec((8, d), lambda l, j: (0, 0)),
                  pl.BlockSpec((1, d, tn), lambda l, j: (l, 0, j)),
                  pl.BlockSpec((1, 1, tn), lambda l, j: (l, 0, j))],
        out_specs=pl.BlockSpec((1, 8, tn), lambda l, j: (l, 0, j)),
        out_shape=jax.ShapeDtypeStruct((depth, 8, n6), F32),
        compiler_params=_cparams(("arbitrary", "arbitrary")),
        name="adaln",
    )(cond8, w_ada, b_ada.reshape(depth, 1, n6))


def _rms(x):
    return x * lax.rsqrt(jnp.mean(x * x, axis=-1, keepdims=True) + EPS)


def _inproj_kernel(has_moe, *refs):
    if has_moe:
        x_ref, moe_ref, gt2_ref, mod_ref, g_ref, w_ref, xo_ref, p_ref = refs
        x = x_ref[0] + gt2_ref[0][5:6] * moe_ref[0]
        xo_ref[0] = x
    else:
        x_ref, mod_ref, g_ref, w_ref, p_ref = refs
        x = x_ref[0]
    mod = mod_ref[0]
    h = (_rms(x) * g_ref[...]) * (1.0 + mod[1:2]) + mod[0:1]
    p_ref[0] = jnp.dot(h.astype(BF16), w_ref[0], preferred_element_type=F32)


def _seg_spec(d, ctx_tiles):
    return pl.BlockSpec((1, 6, d), lambda b, t: (2 * b + jnp.where(t < ctx_tiles, 0, 1), 0, 0))


def _inproj(x, moe, modtab_prev, modtab, g, w_bf16, layer, ctx_len):
    b, l, d = x.shape
    n = w_bf16.shape[2]
    tm = ROW_TILE
    ctx_tiles = ctx_len // tm
    row = pl.BlockSpec((1, tm, d), lambda b, t: (b, t, 0))
    has_moe = moe is not None
    in_specs = [row]
    args = [x]
    if has_moe:
        in_specs += [row, _seg_spec(d, ctx_tiles)]
        args += [moe, modtab_prev]
    in_specs += [_seg_spec(d, ctx_tiles),
                 pl.BlockSpec((1, d), lambda b, t: (0, 0)),
                 pl.BlockSpec((1, d, n), lambda b, t: (layer, 0, 0))]
    args += [modtab, g.reshape(1, d), w_bf16]
    p_spec = pl.BlockSpec((1, tm, n), lambda b, t: (b, t, 0))
    p_shape = jax.ShapeDtypeStruct((b, l, n), F32)
    if has_moe:
        out_specs = [row, p_spec]
        out_shape = [jax.ShapeDtypeStruct((b, l, d), F32), p_shape]
    else:
        out_specs = p_spec
        out_shape = p_shape
    res = pl.pallas_call(
        functools.partial(_inproj_kernel, has_moe),
        grid=(b, l // tm),
        in_specs=in_specs, out_specs=out_specs, out_shape=out_shape,
        compiler_params=_cparams(("arbitrary", "arbitrary")),
        name="inproj",
    )(*args)
    if has_moe:
        return res[0], res[1]
    return x, res


def _bwd_chunk(j, ctx_chunks, n_chunks):
    return jnp.where(j < ctx_chunks, ctx_chunks - 1 - j, n_chunks - 1 + ctx_chunks - j)


def _decay_tables():
    c = SCAN_CHUNK
    w = np.zeros((W_ROWS, c), np.float32)
    masks = np.zeros((N_LEVELS + 1, c, c), np.float32)
    for lev in range(N_LEVELS):
        m = c >> (lev + 1)
        for t in range(c):
            g0 = (t // (2 * m)) * 2 * m
            bnd = g0 + m - 1
            if t > bnd:
                masks[lev, t, g0:bnd + 1] = 1.0
            if m in SMALL_HALVES:
                r0 = (1 + SMALL_HALVES.index(m)) * c
                if t > bnd:
                    w[r0 + t, bnd + 1:t + 1] = 1.0
                else:
                    w[r0 + t, t + 1:bnd + 1] = 1.0
    masks[N_LEVELS] = np.eye(c, dtype=np.float32)
    for t in range(c):
        w[t, :t + 1] = 1.0
    w_b = w.reshape(-1, c, c)[:, ::-1, ::-1].reshape(-1, c)
    masks_b = masks[:, ::-1, ::-1]
    return (jnp.asarray(np.stack([w, w_b]), BF16), jnp.asarray(np.stack([masks, masks_b]), F32))


def _hgrn_kernel(n_heads, qf_ref, ff_ref, if_ref, qb_ref, fb_ref, ib_ref, lb_ref, w_ref, mask_ref,
                 of_ref, ob_ref, st_ref, a_ref):
    c = SCAN_CHUNK

    @pl.when(pl.program_id(1) == 0)
    def _():
        st_ref[...] = jnp.zeros_like(st_ref)

    dirs = ((qf_ref, ff_ref, if_ref, of_ref), (qb_ref, fb_ref, ib_ref, ob_ref))
    for d, (q_ref, f_ref, i_ref, o_ref) in enumerate(dirs):
        for h in range(n_heads):
            sl = slice(h * HEAD_DIM, (h + 1) * HEAD_DIM)
            q = q_ref[0, :, sl]
            xf = f_ref[0, :, sl]
            v = i_ref[0, :, sl].astype(BF16)
            b_ = lb_ref[1:2, sl] + (jnp.minimum(xf, 0.0) - jnp.log(1.0 + jnp.exp(-jnp.abs(xf))))
            a_ = lb_ref[0:1, sl]
            lf2 = (jnp.maximum(a_, b_) + jnp.log(1.0 + jnp.exp(-jnp.abs(a_ - b_)))) * LOG2E
            f = jnp.exp2(lf2)
            k = 1.0 - f
            hi = lf2.astype(BF16)
            lo = (lf2 - hi.astype(F32)).astype(BF16)
            ex2 = jnp.dot(w_ref[d], jnp.concatenate([hi, lo], axis=1), preferred_element_type=F32)
            ex = ex2[:, :HEAD_DIM] + ex2[:, HEAD_DIM:]
            cum = ex[:c]
            a_ref[d, h] = cum
            qb16 = q.astype(BF16)
            kb16 = k.astype(BF16)
            scores = _nt_dot(qb16, kb16) * mask_ref[d, N_LEVELS]
            for lev in range(N_LEVELS):
                m = c >> (lev + 1)
                if m == 1:
                    lhs, rhs = (q * f).astype(BF16), kb16
                else:
                    if m in SMALL_HALVES:
                        r0 = (1 + SMALL_HALVES.index(m)) * c
                        e = jnp.exp2(ex[r0:r0 + c])
                    else:
                        parts = []
                        for g0 in range(0, c, 2 * m):
                            mid = g0 + m - 1 + d
                            parts.append(cum[g0:g0 + 2 * m] - a_ref[d, h, mid:mid + 1, :])
                        e = jnp.exp2(-jnp.abs(jnp.concatenate(parts, axis=0) if len(parts) > 1 else parts[0]))
                    lhs, rhs = (q * e).astype(BF16), (k * e).astype(BF16)
                scores = scores + _nt_dot(lhs, rhs) * mask_ref[d, lev]
            edge = (c - 1) * (1 - d)
            tot = a_ref[d, h, edge:edge + 1, :]
            st = st_ref[d, h]
            out = jnp.dot(scores.astype(BF16), v, preferred_element_type=F32)
            out = out + _nt_dot((q * jnp.exp2(cum)).astype(BF16), st.astype(BF16))
            o_ref[0, :, sl] = out
            kdec = (k * jnp.exp2(tot - cum)).astype(BF16)
            st_ref[d, h] = st * jnp.exp2(tot) + _tn_dot(v, kdec)


def _hgrn_scan(p, lb_tab, w_tab, mask_tab, ctx_len, width):
    b, l, _ = p.shape
    c = SCAN_CHUNK
    n_chunks = l // c
    ctx_chunks = ctx_len // c
    n_heads = width // HEAD_DIM

    def fwd(g):
        return pl.BlockSpec((1, c, width), lambda b, j: (b, j, g))

    def bwd(g):
        return pl.BlockSpec((1, c, width), lambda b, j: (b, _bwd_chunk(j, ctx_chunks, n_chunks), g))

    out_shape = jax.ShapeDtypeStruct((b, l, width), F32)
    return pl.pallas_call(
        functools.partial(_hgrn_kernel, n_heads),
        grid=(b, n_chunks),
        in_specs=[fwd(0), fwd(1), fwd(3), bwd(0), bwd(2), bwd(3),
                  pl.BlockSpec((2, width), lambda b, j: (0, 0)),
                  pl.BlockSpec(w_tab.shape, lambda b, j: (0, 0, 0)),
                  pl.BlockSpec(mask_tab.shape, lambda b, j: (0, 0, 0, 0))],
        out_specs=[fwd(0), bwd(0)],
        out_shape=[out_shape, out_shape],
        scratch_shapes=[pltpu.VMEM((2, n_heads, HEAD_DIM, HEAD_DIM), F32),
                        pltpu.VMEM((2, n_heads, c, HEAD_DIM), F32)],
        compiler_params=_cparams(("arbitrary", "arbitrary")),
        name="hgrn_scan",
    )(p, p, p, p, p, p, lb_tab, w_tab, mask_tab)


def _rope(z, cos_rep, sin_signed, even_lane):
    partner = jnp.where(even_lane, pltpu.roll(z, LANES - 1, 1), pltpu.roll(z, 1, 1))
    return z * cos_rep + partner * sin_signed


def _ret_kernel(n_heads, lg_ref, qf_ref, kf_ref, vf_ref, cf_ref, sf_ref,
                qb_ref, kb_ref, vb_ref, cb_ref, sb_ref, of_ref, ob_ref, st_ref, dm_ref, in_ref, tl_ref):
    c = SCAN_CHUNK
    k_scale = HEAD_DIM ** -0.5

    @pl.when(pl.program_id(1) == 0)
    def _():
        st_ref[...] = jnp.zeros_like(st_ref)
        ti = lax.broadcasted_iota(jnp.int32, (c, c), 0)
        si = lax.broadcasted_iota(jnp.int32, (c, c), 1)
        rowf = lax.broadcasted_iota(jnp.int32, (c, HEAD_DIM), 0).astype(F32)
        for d in range(2):
            rel = (ti - si) if d == 0 else (si - ti)
            relf = jnp.maximum(rel, 0).astype(F32)
            for h in range(n_heads):
                lg = lg_ref[d, h]
                dm_ref[d, h] = jnp.where(rel >= 0, jnp.exp(relf * lg), 0.0)
                if d == 0:
                    in_ref[d, h] = jnp.exp((rowf + 1.0) * lg)
                    tl_ref[d, h] = jnp.exp((c - 1.0 - rowf) * lg)
                else:
                    in_ref[d, h] = jnp.exp((c - rowf) * lg)
                    tl_ref[d, h] = jnp.exp(rowf * lg)

    even_lane = (lax.broadcasted_iota(jnp.int32, (c, HEAD_DIM), 1) & 1) == 0
    dirs = ((qf_ref, kf_ref, vf_ref, cf_ref, sf_ref, of_ref), (qb_ref, kb_ref, vb_ref, cb_ref, sb_ref, ob_ref))
    for d, (q_ref, k_ref, v_ref, c_ref, s_ref, o_ref) in enumerate(dirs):
        cos_rep = c_ref[...]
        sin_signed = s_ref[...]
        for h in range(n_heads):
            sl = slice(h * HEAD_DIM, (h + 1) * HEAD_DIM)
            q = _rope(q_ref[0, :, sl], cos_rep, sin_signed, even_lane)
            k = _rope(k_ref[0, :, sl], cos_rep, sin_signed, even_lane) * k_scale
            v = v_ref[0, :, sl].astype(BF16)
            qb16 = q.astype(BF16)
            scores = _nt_dot(qb16, k.astype(BF16)) * dm_ref[d, h]
            st = st_ref[d, h]
            out = jnp.dot(scores.astype(BF16), v, preferred_element_type=F32)
            out = out + _nt_dot(qb16, st.astype(BF16)) * in_ref[d, h]
            o_ref[0, :, sl] = out
            edge = (c - 1) * (1 - d)
            st_ref[d, h] = (st * in_ref[d, h, edge:edge + 1, :]
                            + _tn_dot(v, (k * tl_ref[d, h]).astype(BF16)))


def _ret_scan(p, lg, cos_rep, sin_signed, ctx_len, width, first_group):
    b, l, _ = p.shape
    c = SCAN_CHUNK
    n_chunks = l // c
    ctx_chunks = ctx_len // c
    n_heads = width // HEAD_DIM

    def fwd(g):
        return pl.BlockSpec((1, c, width), lambda b, j, lg: (b, j, g))

    def bwd(g):
        return pl.BlockSpec((1, c, width), lambda b, j, lg: (b, _bwd_chunk(j, ctx_chunks, n_chunks), g))

    tab_f = pl.BlockSpec((c, HEAD_DIM), lambda b, j, lg: (j, 0))
    tab_b = pl.BlockSpec((c, HEAD_DIM), lambda b, j, lg: (_bwd_chunk(j, ctx_chunks, n_chunks), 0))
    g0 = first_group
    out_shape = jax.ShapeDtypeStruct((b, l, width), F32)
    return pl.pallas_call(
        functools.partial(_ret_kernel, n_heads),
        grid_spec=pltpu.PrefetchScalarGridSpec(
            num_scalar_prefetch=1,
            grid=(b, n_chunks),
            in_specs=[fwd(g0), fwd(g0 + 1), fwd(g0 + 2), tab_f, tab_f,
                      bwd(g0), bwd(g0 + 1), bwd(g0 + 2), tab_b, tab_b],
            out_specs=[fwd(0), bwd(0)],
            scratch_shapes=[pltpu.VMEM((2, n_heads, HEAD_DIM, HEAD_DIM), F32),
                            pltpu.VMEM((2, n_heads, c, c), F32),
                            pltpu.VMEM((2, n_heads, c, HEAD_DIM), F32),
                            pltpu.VMEM((2, n_heads, c, HEAD_DIM), F32)]),
        out_shape=[out_shape, out_shape],
        compiler_params=_cparams(("arbitrary", "arbitrary")),
        name="ret_scan",
    )(lg, p, p, p, cos_rep, sin_signed, p, p, p, cos_rep, sin_signed)


def _outproj_kernel(n_hg, n_ret, x_ref, hf_ref, hb_ref, rf_ref, rb_ref, hg_ref, rg_ref, mod_ref,
                    on_ref, wo_ref, g2_ref, wrh_ref, wrl_ref, xo_ref, h2_ref, aff_ref):
    mod = mod_ref[0]
    o_hg = hf_ref[0] + hb_ref[0]
    o_rt = rf_ref[0] + rb_ref[0]
    parts = []
    for h in range(n_hg):
        sl = slice(h * HEAD_DIM, (h + 1) * HEAD_DIM)
        parts.append((_rms(o_hg[:, sl]) * on_ref[...]) * _silu(hg_ref[0, :, sl]))
    for h in range(n_ret):
        sl = slice(h * HEAD_DIM, (h + 1) * HEAD_DIM)
        parts.append(_rms(o_rt[:, sl]) * _silu(rg_ref[0, :, sl]))
    mix = jnp.concatenate(parts, axis=1).astype(BF16)
    x = x_ref[0] + mod[2:3] * jnp.dot(mix, wo_ref[0], preferred_element_type=F32)
    xo_ref[0] = x
    h2 = (_rms(x) * g2_ref[...]) * (1.0 + mod[4:5]) + mod[3:4]
    h2_ref[0] = h2.astype(BF16)
    hh = h2.astype(BF16)
    hl = (h2 - hh.astype(F32)).astype(BF16)
    logits = _nt_dot(wrh_ref[...], hh) + _nt_dot(wrh_ref[...], hl) + _nt_dot(wrl_ref[...], hh)
    mx = jnp.max(logits, axis=0, keepdims=True)
    ex = jnp.exp(logits - mx)
    aff_ref[0] = ex / jnp.sum(ex, axis=0, keepdims=True)


def _outproj(x, hf, hb, rf, rb, p, modtab, onorm, wo_bf16, layer, g2, w_router, ctx_len, hg_width):
    b, l, d = x.shape
    tm = ROW_TILE
    ctx_tiles = ctx_len // tm
    n_hg = hg_width // HEAD_DIM
    ret_width = d - hg_width
    n_ret = ret_width // HEAD_DIM
    n_e = w_router.shape[1]
    wrt = w_router.T
    wrh = wrt.astype(BF16)
    wrl = (wrt - wrh.astype(F32)).astype(BF16)
    row = pl.BlockSpec((1, tm, d), lambda b, t: (b, t, 0))
    hrow = pl.BlockSpec((1, tm, hg_width), lambda b, t: (b, t, 0))
    rrow = pl.BlockSpec((1, tm, ret_width), lambda b, t: (b, t, 0))
    full = lambda shape: pl.BlockSpec(shape, lambda b, t: tuple(0 for _ in shape))
    return pl.pallas_call(
        functools.partial(_outproj_kernel, n_hg, n_ret),
        grid=(b, l // tm),
        in_specs=[row, hrow, hrow, rrow, rrow,
                  pl.BlockSpec((1, tm, hg_width), lambda b, t: (b, t, 4)),
                  pl.BlockSpec((1, tm, ret_width), lambda b, t: (b, t, 8)),
                  _seg_spec(d, ctx_tiles),
                  full((1, HEAD_DIM)), pl.BlockSpec((1, d, d), lambda b, t: (layer, 0, 0)),
                  full((1, d)), full((n_e, d)), full((n_e, d))],
        out_specs=[row, row, pl.BlockSpec((1, n_e, tm), lambda b, t: (b, 0, t))],
        out_shape=[jax.ShapeDtypeStruct((b, l, d), F32),
                   jax.ShapeDtypeStruct((b, l, d), BF16),
                   jax.ShapeDtypeStruct((b, n_e, l), F32)],
        compiler_params=_cparams(("arbitrary", "arbitrary")),
        name="outproj",
    )(x, hf, hb, rf, rb, p, p, modtab, onorm.reshape(1, HEAD_DIM), wo_bf16, g2.reshape(1, d), wrh, wrl)


def _ffn_kernel(x_ref, gate_ref, wg_ref, wu_ref, wd_ref, o_ref):
    x = x_ref[0]
    a = jnp.dot(x, wg_ref[0, 0], preferred_element_type=F32)
    u = jnp.dot(x, wu_ref[0, 0], preferred_element_type=F32)
    hid = (_silu(a) * u).astype(BF16)
    o_ref[0] = jnp.dot(hid, wd_ref[0, 0], preferred_element_type=F32) * gate_ref[0]


def _row_tile(rows, cap=512):
    best = 16
    for t in range(16, cap + 1, 16):
        if rows % t == 0:
            best = t
    return best


def _expert_ffn(xs, gates, wg, wu, wd, layer):
    n_e, rows, d = xs.shape
    ff = wg.shape[3]
    tm = _row_tile(rows)
    return pl.pallas_call(
        _ffn_kernel,
        grid=(n_e, rows // tm),
        in_specs=[pl.BlockSpec((1, tm, d), lambda e, t: (e, t, 0)),
                  pl.BlockSpec((1, tm, 1), lambda e, t: (e, t, 0)),
                  pl.BlockSpec((1, 1, d, ff), lambda e, t: (layer, e, 0, 0)),
                  pl.BlockSpec((1, 1, d, ff), lambda e, t: (layer, e, 0, 0)),
                  pl.BlockSpec((1, 1, ff, d), lambda e, t: (layer, e, 0, 0))],
        out_specs=pl.BlockSpec((1, tm, d), lambda e, t: (e, t, 0)),
        out_shape=jax.ShapeDtypeStruct((n_e, rows, d), F32),
        compiler_params=_cparams(("arbitrary", "arbitrary")),
        name="expert_ffn",
    )(xs, gates, wg, wu, wd)


def _final_kernel(x_ref, moe_ref, gt2_ref, g_ref, o_ref):
    x = x_ref[0] + gt2_ref[0][5:6] * moe_ref[0]
    o_ref[0] = _rms(x) * g_ref[...]


def _final(x, moe, modtab, g, ctx_len):
    b, l, d = x.shape
    tm = ROW_TILE
    ctx_tiles = ctx_len // tm
    n_t = (l - ctx_len) // tm
    row = pl.BlockSpec((1, tm, d), lambda b, t: (b, t + ctx_tiles, 0))
    return pl.pallas_call(
        _final_kernel,
        grid=(b, n_t),
        in_specs=[row, row,
                  pl.BlockSpec((1, 6, d), lambda b, t: (2 * b + 1, 0, 0)),
                  pl.BlockSpec((1, d), lambda b, t: (0, 0))],
        out_specs=pl.BlockSpec((1, tm, d), lambda b, t: (b, t, 0)),
        out_shape=jax.ShapeDtypeStruct((b, l - ctx_len, d), F32),
        compiler_params=_cparams(("arbitrary", "arbitrary")),
        name="final_norm",
    )(x, moe, modtab, g.reshape(1, d))


def _rope_tables(rows, ctx_len):
    n_freq = HEAD_DIM // 4
    inv = ROPE_BASE ** (-jnp.arange(n_freq, dtype=F32) / n_freq)
    r = jnp.repeat(jnp.arange(rows, dtype=F32), GRID_W)
    cc = jnp.tile(jnp.arange(GRID_W, dtype=F32), rows)
    lat = jnp.concatenate([r[:, None] * inv, cc[:, None] * inv], axis=-1)
    ang = jnp.concatenate([jnp.zeros((ctx_len, 2 * n_freq), F32), lat], axis=0)
    cos = jnp.cos(ang)
    sin = jnp.sin(ang)
    cos_rep = jnp.repeat(cos, 2, axis=-1)
    sin_signed = jnp.stack([-sin, sin], axis=-1).reshape(ang.shape[0], HEAD_DIM)
    return cos_rep, sin_signed


def _route(aff_t, cap):
    return lax.top_k(aff_t, cap)


def kernel(x, c, ctx, c_ctx, w_ada, b_ada, g_mix, g_ffn, w_in, w_out, hg_lb_logits, hg_onorm,
           ret_decay, w_router, w_e_gate, w_e_up, w_e_down, g_final):
    bsz, n, d = x.shape
    lc = ctx.shape[1]
    depth = w_ada.shape[0]
    l = lc + n
    hg_width = hg_lb_logits.shape[1]
    ret_width = d - hg_width
    n_e = w_router.shape[2]
    assert lc % ROW_TILE == 0 and n % ROW_TILE == 0 and lc % SCAN_CHUNK == 0 and n % SCAN_CHUNK == 0
    assert hg_width % HEAD_DIM == 0 and ret_width % HEAD_DIM == 0 and n % GRID_W == 0
    assert w_in.shape[2] == 5 * hg_width + 4 * ret_width and hg_width == ret_width

    cos_rep, sin_signed = _rope_tables(n // GRID_W, lc)
    w_tab, mask_tab = _decay_tables()

    gamma_cum = jnp.cumsum(jax.nn.softmax(hg_lb_logits.astype(F32), axis=0), axis=0)
    lbs = gamma_cum - gamma_cum[0:1]
    lb_tabs = jnp.stack([jnp.log(lbs), jnp.log1p(-lbs)], axis=1)
    log_gammas = -jnp.exp(ret_decay.astype(F32))

    cond8 = jnp.zeros((8, d), F32).at[:bsz].set(c).at[bsz].set(c_ctx)
    mods = _adaln(cond8, w_ada, b_ada).reshape(depth, 8, 6, d)
    modtabs = jnp.stack([jnp.broadcast_to(mods[:, bsz:bsz + 1], (depth, bsz, 6, d)), mods[:, :bsz]],
                        axis=2).reshape(depth, 2 * bsz, 6, d)

    cap_lat = CAPACITY_FACTOR * n // n_e
    cap_ctx = CAPACITY_FACTOR * lc // n_e
    boff = (jnp.arange(bsz, dtype=jnp.int32) * l)[:, None, None]

    w_in16, w_out16 = w_in.astype(BF16), w_out.astype(BF16)
    wg16, wu16, wd16 = w_e_gate.astype(BF16), w_e_up.astype(BF16), w_e_down.astype(BF16)
    xs = jnp.concatenate([ctx, x], axis=1)
    moe = None
    for layer in range(depth):
        last = layer == depth - 1
        xs, p = _inproj(xs, moe, modtabs[layer - 1] if layer else None, modtabs[layer], g_mix[layer],
                        w_in16, layer, lc)
        hf, hb = _hgrn_scan(p, lb_tabs[layer], w_tab, mask_tab, lc, hg_width)
        rf, rb = _ret_scan(p, log_gammas[layer], cos_rep, sin_signed, lc, ret_width, 5)
        xs, h2, aff_t = _outproj(xs, hf, hb, rf, rb, p, modtabs[layer], hg_onorm[layer],
                                 w_out16, layer, g_ffn[layer], w_router[layer], lc, hg_width)
        g_lat, i_lat = _route(aff_t[:, :, lc:], cap_lat)
        rows = [(i_lat + lc + boff), ]
        gts = [g_lat]
        if not last:
            g_ctx, i_ctx = _route(aff_t[:, :, :lc], cap_ctx)
            rows.append(i_ctx + boff)
            gts.append(g_ctx)
        flat = jnp.concatenate([r.transpose(1, 0, 2).reshape(n_e, -1) for r in rows], axis=1)
        gate = jnp.concatenate([g.transpose(1, 0, 2).reshape(n_e, -1) for g in gts], axis=1)
        n_rows = flat.shape[1]
        pad = (-n_rows) % 16
        if pad:
            flat = jnp.pad(flat, ((0, 0), (0, pad)))
            gate = jnp.pad(gate, ((0, 0), (0, pad)))
        h2f = h2.reshape(bsz * l, d)
        gathered = h2f[flat.reshape(-1)].reshape(n_e, n_rows + pad, d)
        ys = _expert_ffn(gathered, gate[:, :, None], wg16, wu16, wd16, layer)
        moe = jnp.zeros((bsz * l, d), F32).at[flat.reshape(-1)].add(ys.reshape(-1, d)).reshape(bsz, l, d)
    return _final(xs, moe, modtabs[depth - 1], g_final, lc)
```

```python
import functools

import numpy as np
import jax
import jax.numpy as jnp
from jax import lax
from jax.experimental import pallas as pl
from jax.experimental.pallas import tpu as pltpu

F32 = jnp.float32
BF16 = jnp.bfloat16

EPS = 1e-6
GRID_W = 64
ROPE_BASE = 10000.0
N_EXPERTS = 16
CAPACITY_FACTOR = 2
HEAD_DIM = 128
LANES = 128
SCAN_CHUNK = 128
ROW_TILE = 256
VMEM_LIMIT = 56 * 1024 * 1024

N_LEVELS = int(np.log2(SCAN_CHUNK))
SMALL_HALVES = (4, 2)
W_ROWS = (1 + len(SMALL_HALVES)) * SCAN_CHUNK
LOG2E = float(np.log2(np.e))


def _cparams(sem):
    return pltpu.CompilerParams(dimension_semantics=sem, vmem_limit_bytes=VMEM_LIMIT)


def _nt_dot(a, b):
    return lax.dot_general(a, b, (((1,), (1,)), ((), ())), preferred_element_type=F32)


def _tn_dot(a, b):
    return lax.dot_general(a, b, (((0,), (0,)), ((), ())), preferred_element_type=F32)


def _sigmoid(x):
    return 1.0 / (1.0 + jnp.exp(-x))


def _silu(x):
    return x * _sigmoid(x)


def _adaln_kernel(cond_ref, w_ref, b_ref, o_ref):
    s = _silu(cond_ref[...])
    o_ref[0] = jnp.dot(s.astype(BF16), w_ref[0].astype(BF16), preferred_element_type=F32) + b_ref[0]


def _adaln(cond8, w_ada, b_ada):
    depth, d, n6 = w_ada.shape
    tn = n6 // 4
    return pl.pallas_call(
        _adaln_kernel,
        grid=(depth, n6 // tn),
        in_specs=[pl.BlockSpec((8, d), lambda l, j: (0, 0)),
                  pl.BlockSpec((1, d, tn), lambda l, j: (l, 0, j)),
                  pl.BlockSpec((1, 1, tn), lambda l, j: (l, 0, j))],
        out_specs=pl.BlockSpec((1, 8, tn), lambda l, j: (l, 0, j)),
        out_shape=jax.ShapeDtypeStruct((depth, 8, n6), F32),
        compiler_params=_cparams(("arbitrary", "arbitrary")),
        name="adaln",
    )(cond8, w_ada, b_ada.reshape(depth, 1, n6))


def _rms(x):
    return x * lax.rsqrt(jnp.mean(x * x, axis=-1, keepdims=True) + EPS)


COMBINE_ROWS = 64
YS_ALIGN = 16


def _combine_tile(b, t, start_ref, npass_ref, arow_ref, ys_ref, expand_ref, lanemod_ref, buf_ref, sem):
    n_e = arow_ref.shape[2]
    n_blk = arow_ref.shape[1]
    tm = n_blk * LANES
    wn = COMBINE_ROWS
    n_rows = ys_ref.shape[1]
    d = ys_ref.shape[2]
    arow = jnp.concatenate([arow_ref[0, i] for i in range(n_blk)], axis=1)
    erow = lax.broadcasted_iota(jnp.int32, (n_e, tm), 0)
    starts = [(start_ref[b, t * n_blk * n_e + e] // YS_ALIGN) * YS_ALIGN for e in range(n_e)]
    first = jnp.zeros((n_e, tm), F32)
    for e in range(n_e):
        first = jnp.where(erow == e, starts[e].astype(F32), first)
    rel0 = arow - first

    def copy(e, p):
        src = jnp.minimum(starts[e] + p * wn, n_rows - wn)
        src = pl.multiple_of(src, YS_ALIGN)
        return src, pltpu.make_async_copy(ys_ref.at[e, pl.ds(src, wn), :],
                                          buf_ref.at[pl.ds(e * wn, wn), :], sem.at[e])

    def one_pass(p, acc):
        copies = [copy(e, p) for e in range(n_e)]
        for _, cp in copies:
            cp.start()
        win = jnp.zeros((n_e, tm), F32)
        for e in range(n_e):
            win = jnp.where(erow == e, copies[e][0].astype(F32), win)
        lo = jnp.asarray(p * wn, F32)
        mine = (arow >= 0.0) & (rel0 >= lo) & (rel0 < lo + wn)
        rel = jnp.where(mine, arow - win, 255.0).astype(BF16)
        spread = _tn_dot(rel, expand_ref[...])
        onehot = jnp.where(spread == lanemod_ref[...], 1.0, 0.0).astype(BF16)
        for _, cp in copies:
            cp.wait()
        return acc + jnp.dot(onehot, buf_ref[...], preferred_element_type=F32)

    return lax.fori_loop(0, npass_ref[b, t], one_pass, jnp.zeros((tm, d), F32))


def _inproj_kernel(has_moe, *refs):
    if has_moe:
        (start_ref, npass_ref, x_ref, arow_ref, ys_ref, expand_ref, lanemod_ref, gt2_ref, mod_ref, g_ref, w_ref,
         xo_ref, p_ref, buf_ref, sem) = refs
        moe = _combine_tile(pl.program_id(0), pl.program_id(1), start_ref, npass_ref, arow_ref, ys_ref,
                            expand_ref, lanemod_ref, buf_ref, sem)
        x = x_ref[0] + gt2_ref[0][5:6] * moe
        xo_ref[0] = x
    else:
        x_ref, mod_ref, g_ref, w_ref, p_ref = refs
        x = x_ref[0]
    mod = mod_ref[0]
    h = (_rms(x) * g_ref[...]) * (1.0 + mod[1:2]) + mod[0:1]
    p_ref[0] = jnp.dot(h.astype(BF16), w_ref[0], preferred_element_type=F32)


def _seg_spec(d, ctx_tiles):
    return pl.BlockSpec((1, 6, d), lambda b, t, *_: (2 * b + jnp.where(t < ctx_tiles, 0, 1), 0, 0))


def _combine_consts(n_e):
    wn = COMBINE_ROWS
    expand = np.zeros((n_e, n_e * wn), np.float32)
    for e in range(n_e):
        expand[e, e * wn:(e + 1) * wn] = 1.0
    lanemod = (np.arange(n_e * wn) % wn).astype(np.float32)[None, :]
    return jnp.asarray(expand, BF16), jnp.asarray(lanemod, F32)


def _combine_specs(comb, tm, first_tile):
    start, npass, arow, ys = comb
    n_e = arow.shape[2]
    expand, lanemod = _combine_consts(n_e)
    k = n_e * COMBINE_ROWS
    specs = [pl.BlockSpec((1, tm // LANES, n_e, LANES), lambda b, t, *_: (b, t + first_tile, 0, 0)),
             pl.BlockSpec(memory_space=pl.ANY),
             pl.BlockSpec((n_e, k), lambda b, t, *_: (0, 0)),
             pl.BlockSpec((1, k), lambda b, t, *_: (0, 0))]
    scratch = [pltpu.VMEM((k, ys.shape[2]), BF16), pltpu.SemaphoreType.DMA((n_e,))]
    return specs, [arow, ys, expand, lanemod], scratch


def _inproj(x, comb, modtab_prev, modtab, g, w_bf16, layer, ctx_len):
    b, l, d = x.shape
    n = w_bf16.shape[2]
    tm = ROW_TILE
    ctx_tiles = ctx_len // tm
    row = pl.BlockSpec((1, tm, d), lambda b, t, *_: (b, t, 0))
    has_moe = comb is not None
    in_specs = [row]
    args = [x]
    scratch = []
    prefetch = []
    if has_moe:
        specs, cargs, scratch = _combine_specs(comb, tm, 0)
        in_specs += specs + [_seg_spec(d, ctx_tiles)]
        args += cargs + [modtab_prev]
        prefetch = [comb[0], comb[1]]
    in_specs += [_seg_spec(d, ctx_tiles),
                 pl.BlockSpec((1, d), lambda b, t, *_: (0, 0)),
                 pl.BlockSpec((1, d, n), lambda b, t, *_: (layer, 0, 0))]
    args += [modtab, g.reshape(1, d), w_bf16]
    p_spec = pl.BlockSpec((1, tm, n), lambda b, t, *_: (b, t, 0))
    p_shape = jax.ShapeDtypeStruct((b, l, n), F32)
    if has_moe:
        out_specs = [row, p_spec]
        out_shape = [jax.ShapeDtypeStruct((b, l, d), F32), p_shape]
    else:
        out_specs = p_spec
        out_shape = p_shape
    res = pl.pallas_call(
        functools.partial(_inproj_kernel, has_moe),
        grid_spec=pltpu.PrefetchScalarGridSpec(
            num_scalar_prefetch=len(prefetch), grid=(b, l // tm),
            in_specs=in_specs, out_specs=out_specs, scratch_shapes=scratch),
        out_shape=out_shape,
        compiler_params=_cparams(("arbitrary", "arbitrary")),
        name="inproj",
    )(*prefetch, *args)
    if has_moe:
        return res[0], res[1]
    return x, res


def _bwd_chunk(j, ctx_chunks, n_chunks):
    return jnp.where(j < ctx_chunks, ctx_chunks - 1 - j, n_chunks - 1 + ctx_chunks - j)


def _decay_tables():
    c = SCAN_CHUNK
    w = np.zeros((W_ROWS, c), np.float32)
    masks = np.zeros((N_LEVELS + 1, c, c), np.float32)
    for lev in range(N_LEVELS):
        m = c >> (lev + 1)
        for t in range(c):
            g0 = (t // (2 * m)) * 2 * m
            bnd = g0 + m - 1
            if t > bnd:
                masks[lev, t, g0:bnd + 1] = 1.0
            if m in SMALL_HALVES:
                r0 = (1 + SMALL_HALVES.index(m)) * c
                if t > bnd:
                    w[r0 + t, bnd + 1:t + 1] = 1.0
                else:
                    w[r0 + t, t + 1:bnd + 1] = 1.0
    masks[N_LEVELS] = np.eye(c, dtype=np.float32)
    for t in range(c):
        w[t, :t + 1] = 1.0
    w_b = w.reshape(-1, c, c)[:, ::-1, ::-1].reshape(-1, c)
    masks_b = masks[:, ::-1, ::-1]
    return (jnp.asarray(np.stack([w, w_b]), BF16), jnp.asarray(np.stack([masks, masks_b]), F32))


def _hgrn_kernel(n_heads, qf_ref, ff_ref, if_ref, qb_ref, fb_ref, ib_ref, lb_ref, w_ref, mask_ref,
                 of_ref, ob_ref, st_ref, a_ref):
    c = SCAN_CHUNK

    @pl.when(pl.program_id(1) == 0)
    def _():
        st_ref[...] = jnp.zeros_like(st_ref)

    dirs = ((qf_ref, ff_ref, if_ref, of_ref), (qb_ref, fb_ref, ib_ref, ob_ref))
    for d, (q_ref, f_ref, i_ref, o_ref) in enumerate(dirs):
        for h in range(n_heads):
            sl = slice(h * HEAD_DIM, (h + 1) * HEAD_DIM)
            q = q_ref[0, :, sl]
            xf = f_ref[0, :, sl]
            v = i_ref[0, :, sl].astype(BF16)
            b_ = lb_ref[1:2, sl] + (jnp.minimum(xf, 0.0) - jnp.log(1.0 + jnp.exp(-jnp.abs(xf))))
            a_ = lb_ref[0:1, sl]
            lf2 = (jnp.maximum(a_, b_) + jnp.log(1.0 + jnp.exp(-jnp.abs(a_ - b_)))) * LOG2E
            f = jnp.exp2(lf2)
            k = 1.0 - f
            hi = lf2.astype(BF16)
            lo = (lf2 - hi.astype(F32)).astype(BF16)
            ex2 = jnp.dot(w_ref[d], jnp.concatenate([hi, lo], axis=1), preferred_element_type=F32)
            ex = ex2[:, :HEAD_DIM] + ex2[:, HEAD_DIM:]
            cum = ex[:c]
            a_ref[d, h] = cum
            qb16 = q.astype(BF16)
            kb16 = k.astype(BF16)
            scores = _nt_dot(qb16, kb16) * mask_ref[d, N_LEVELS]
            for lev in range(N_LEVELS):
                m = c >> (lev + 1)
                if m == 1:
                    lhs, rhs = (q * f).astype(BF16), kb16
                else:
                    if m in SMALL_HALVES:
                        r0 = (1 + SMALL_HALVES.index(m)) * c
                        e = jnp.exp2(ex[r0:r0 + c])
                    else:
                        parts = []
                        for g0 in range(0, c, 2 * m):
                            mid = g0 + m - 1 + d
                            parts.append(cum[g0:g0 + 2 * m] - a_ref[d, h, mid:mid + 1, :])
                        e = jnp.exp2(-jnp.abs(jnp.concatenate(parts, axis=0) if len(parts) > 1 else parts[0]))
                    lhs, rhs = (q * e).astype(BF16), (k * e).astype(BF16)
                scores = scores + _nt_dot(lhs, rhs) * mask_ref[d, lev]
            edge = (c - 1) * (1 - d)
            tot = a_ref[d, h, edge:edge + 1, :]
            st = st_ref[d, h]
            out = jnp.dot(scores.astype(BF16), v, preferred_element_type=F32)
            out = out + _nt_dot((q * jnp.exp2(cum)).astype(BF16), st.astype(BF16))
            o_ref[0, :, sl] = out
            kdec = (k * jnp.exp2(tot - cum)).astype(BF16)
            st_ref[d, h] = st * jnp.exp2(tot) + _tn_dot(v, kdec)


def _hgrn_scan(p, lb_tab, w_tab, mask_tab, ctx_len, width):
    b, l, _ = p.shape
    c = SCAN_CHUNK
    n_chunks = l // c
    ctx_chunks = ctx_len // c
    n_heads = width // HEAD_DIM

    def fwd(g):
        return pl.BlockSpec((1, c, width), lambda b, j: (b, j, g))

    def bwd(g):
        return pl.BlockSpec((1, c, width), lambda b, j: (b, _bwd_chunk(j, ctx_chunks, n_chunks), g))

    out_shape = jax.ShapeDtypeStruct((b, l, width), F32)
    return pl.pallas_call(
        functools.partial(_hgrn_kernel, n_heads),
        grid=(b, n_chunks),
        in_specs=[fwd(0), fwd(1), fwd(3), bwd(0), bwd(2), bwd(3),
                  pl.BlockSpec((2, width), lambda b, j: (0, 0)),
                  pl.BlockSpec(w_tab.shape, lambda b, j: (0, 0, 0)),
                  pl.BlockSpec(mask_tab.shape, lambda b, j: (0, 0, 0, 0))],
        out_specs=[fwd(0), bwd(0)],
        out_shape=[out_shape, out_shape],
        scratch_shapes=[pltpu.VMEM((2, n_heads, HEAD_DIM, HEAD_DIM), F32),
                        pltpu.VMEM((2, n_heads, c, HEAD_DIM), F32)],
        compiler_params=_cparams(("arbitrary", "arbitrary")),
        name="hgrn_scan",
    )(p, p, p, p, p, p, lb_tab, w_tab, mask_tab)


def _rope(z, cos_rep, sin_signed, even_lane):
    partner = jnp.where(even_lane, pltpu.roll(z, LANES - 1, 1), pltpu.roll(z, 1, 1))
    return z * cos_rep + partner * sin_signed


def _ret_kernel(n_heads, lg_ref, qf_ref, kf_ref, vf_ref, cf_ref, sf_ref,
                qb_ref, kb_ref, vb_ref, cb_ref, sb_ref, of_ref, ob_ref, st_ref, dm_ref, in_ref, tl_ref):
    c = SCAN_CHUNK
    k_scale = HEAD_DIM ** -0.5

    @pl.when(pl.program_id(1) == 0)
    def _():
        st_ref[...] = jnp.zeros_like(st_ref)
        ti = lax.broadcasted_iota(jnp.int32, (c, c), 0)
        si = lax.broadcasted_iota(jnp.int32, (c, c), 1)
        rowf = lax.broadcasted_iota(jnp.int32, (c, HEAD_DIM), 0).astype(F32)
        for d in range(2):
            rel = (ti - si) if d == 0 else (si - ti)
            relf = jnp.maximum(rel, 0).astype(F32)
            for h in range(n_heads):
                lg = lg_ref[d, h]
                dm_ref[d, h] = jnp.where(rel >= 0, jnp.exp(relf * lg), 0.0)
                if d == 0:
                    in_ref[d, h] = jnp.exp((rowf + 1.0) * lg)
                    tl_ref[d, h] = jnp.exp((c - 1.0 - rowf) * lg)
                else:
                    in_ref[d, h] = jnp.exp((c - rowf) * lg)
                    tl_ref[d, h] = jnp.exp(rowf * lg)

    even_lane = (lax.broadcasted_iota(jnp.int32, (c, HEAD_DIM), 1) & 1) == 0
    dirs = ((qf_ref, kf_ref, vf_ref, cf_ref, sf_ref, of_ref), (qb_ref, kb_ref, vb_ref, cb_ref, sb_ref, ob_ref))
    for d, (q_ref, k_ref, v_ref, c_ref, s_ref, o_ref) in enumerate(dirs):
        cos_rep = c_ref[...]
        sin_signed = s_ref[...]
        for h in range(n_heads):
            sl = slice(h * HEAD_DIM, (h + 1) * HEAD_DIM)
            q = _rope(q_ref[0, :, sl], cos_rep, sin_signed, even_lane)
            k = _rope(k_ref[0, :, sl], cos_rep, sin_signed, even_lane) * k_scale
            v = v_ref[0, :, sl].astype(BF16)
            qb16 = q.astype(BF16)
            scores = _nt_dot(qb16, k.astype(BF16)) * dm_ref[d, h]
            st = st_ref[d, h]
            out = jnp.dot(scores.astype(BF16), v, preferred_element_type=F32)
            out = out + _nt_dot(qb16, st.astype(BF16)) * in_ref[d, h]
            o_ref[0, :, sl] = out
            edge = (c - 1) * (1 - d)
            st_ref[d, h] = (st * in_ref[d, h, edge:edge + 1, :]
                            + _tn_dot(v, (k * tl_ref[d, h]).astype(BF16)))


def _ret_scan(p, lg, cos_rep, sin_signed, ctx_len, width, first_group):
    b, l, _ = p.shape
    c = SCAN_CHUNK
    n_chunks = l // c
    ctx_chunks = ctx_len // c
    n_heads = width // HEAD_DIM

    def fwd(g):
        return pl.BlockSpec((1, c, width), lambda b, j, lg: (b, j, g))

    def bwd(g):
        return pl.BlockSpec((1, c, width), lambda b, j, lg: (b, _bwd_chunk(j, ctx_chunks, n_chunks), g))

    tab_f = pl.BlockSpec((c, HEAD_DIM), lambda b, j, lg: (j, 0))
    tab_b = pl.BlockSpec((c, HEAD_DIM), lambda b, j, lg: (_bwd_chunk(j, ctx_chunks, n_chunks), 0))
    g0 = first_group
    out_shape = jax.ShapeDtypeStruct((b, l, width), F32)
    return pl.pallas_call(
        functools.partial(_ret_kernel, n_heads),
        grid_spec=pltpu.PrefetchScalarGridSpec(
            num_scalar_prefetch=1,
            grid=(b, n_chunks),
            in_specs=[fwd(g0), fwd(g0 + 1), fwd(g0 + 2), tab_f, tab_f,
                      bwd(g0), bwd(g0 + 1), bwd(g0 + 2), tab_b, tab_b],
            out_specs=[fwd(0), bwd(0)],
            scratch_shapes=[pltpu.VMEM((2, n_heads, HEAD_DIM, HEAD_DIM), F32),
                            pltpu.VMEM((2, n_heads, c, c), F32),
                            pltpu.VMEM((2, n_heads, c, HEAD_DIM), F32),
                            pltpu.VMEM((2, n_heads, c, HEAD_DIM), F32)]),
        out_shape=[out_shape, out_shape],
        compiler_params=_cparams(("arbitrary", "arbitrary")),
        name="ret_scan",
    )(lg, p, p, p, cos_rep, sin_signed, p, p, p, cos_rep, sin_signed)


def _outproj_kernel(n_hg, n_ret, x_ref, hf_ref, hb_ref, rf_ref, rb_ref, hg_ref, rg_ref, mod_ref,
                    on_ref, wo_ref, g2_ref, wrh_ref, wrl_ref, xo_ref, h2_ref, aff_ref):
    mod = mod_ref[0]
    o_hg = hf_ref[0] + hb_ref[0]
    o_rt = rf_ref[0] + rb_ref[0]
    parts = []
    for h in range(n_hg):
        sl = slice(h * HEAD_DIM, (h + 1) * HEAD_DIM)
        parts.append((_rms(o_hg[:, sl]) * on_ref[...]) * _silu(hg_ref[0, :, sl]))
    for h in range(n_ret):
        sl = slice(h * HEAD_DIM, (h + 1) * HEAD_DIM)
        parts.append(_rms(o_rt[:, sl]) * _silu(rg_ref[0, :, sl]))
    mix = jnp.concatenate(parts, axis=1).astype(BF16)
    x = x_ref[0] + mod[2:3] * jnp.dot(mix, wo_ref[0], preferred_element_type=F32)
    xo_ref[0] = x
    h2 = (_rms(x) * g2_ref[...]) * (1.0 + mod[4:5]) + mod[3:4]
    h2_ref[0] = h2.astype(BF16)
    hh = h2.astype(BF16)
    hl = (h2 - hh.astype(F32)).astype(BF16)
    logits = _nt_dot(wrh_ref[...], hh) + _nt_dot(wrh_ref[...], hl) + _nt_dot(wrl_ref[...], hh)
    mx = jnp.max(logits, axis=0, keepdims=True)
    ex = jnp.exp(logits - mx)
    aff = ex / jnp.sum(ex, axis=0, keepdims=True)
    for i in range(aff_ref.shape[1]):
        aff_ref[0, i] = aff[:, i * LANES:(i + 1) * LANES]


def _outproj(x, hf, hb, rf, rb, p, modtab, onorm, wo_bf16, layer, g2, w_router, ctx_len, hg_width):
    b, l, d = x.shape
    tm = ROW_TILE
    ctx_tiles = ctx_len // tm
    n_hg = hg_width // HEAD_DIM
    ret_width = d - hg_width
    n_ret = ret_width // HEAD_DIM
    n_e = w_router.shape[1]
    wrt = w_router.T
    wrh = wrt.astype(BF16)
    wrl = (wrt - wrh.astype(F32)).astype(BF16)
    row = pl.BlockSpec((1, tm, d), lambda b, t: (b, t, 0))
    hrow = pl.BlockSpec((1, tm, hg_width), lambda b, t: (b, t, 0))
    rrow = pl.BlockSpec((1, tm, ret_width), lambda b, t: (b, t, 0))
    full = lambda shape: pl.BlockSpec(shape, lambda b, t: tuple(0 for _ in shape))
    return pl.pallas_call(
        functools.partial(_outproj_kernel, n_hg, n_ret),
        grid=(b, l // tm),
        in_specs=[row, hrow, hrow, rrow, rrow,
                  pl.BlockSpec((1, tm, hg_width), lambda b, t: (b, t, 4)),
                  pl.BlockSpec((1, tm, ret_width), lambda b, t: (b, t, 8)),
                  _seg_spec(d, ctx_tiles),
                  full((1, HEAD_DIM)), pl.BlockSpec((1, d, d), lambda b, t: (layer, 0, 0)),
                  full((1, d)), full((n_e, d)), full((n_e, d))],
        out_specs=[row, row, pl.BlockSpec((1, tm // LANES, n_e, LANES), lambda b, t: (b, t, 0, 0))],
        out_shape=[jax.ShapeDtypeStruct((b, l, d), F32),
                   jax.ShapeDtypeStruct((b, l, d), BF16),
                   jax.ShapeDtypeStruct((b, l // LANES, n_e, LANES), F32)],
        compiler_params=_cparams(("arbitrary", "arbitrary")),
        name="outproj",
    )(x, hf, hb, rf, rb, p, p, modtab, onorm.reshape(1, HEAD_DIM), wo_bf16, g2.reshape(1, d), wrh, wrl)


def _ffn_kernel(x_ref, gate_ref, wg_ref, wu_ref, wd_ref, o_ref):
    x = x_ref[0]
    a = jnp.dot(x, wg_ref[0, 0], preferred_element_type=F32)
    u = jnp.dot(x, wu_ref[0, 0], preferred_element_type=F32)
    hid = (_silu(a) * u).astype(BF16)
    o_ref[0] = (jnp.dot(hid, wd_ref[0, 0], preferred_element_type=F32) * gate_ref[0]).astype(BF16)


def _row_tile(rows, cap=512):
    best = 16
    for t in range(16, cap + 1, 16):
        if rows % t == 0:
            best = t
    return best


def _expert_ffn(xs, gates, wg, wu, wd, layer):
    n_e, rows, d = xs.shape
    ff = wg.shape[3]
    tm = _row_tile(rows)
    return pl.pallas_call(
        _ffn_kernel,
        grid=(n_e, rows // tm),
        in_specs=[pl.BlockSpec((1, tm, d), lambda e, t: (e, t, 0)),
                  pl.BlockSpec((1, tm, 1), lambda e, t: (e, t, 0)),
                  pl.BlockSpec((1, 1, d, ff), lambda e, t: (layer, e, 0, 0)),
                  pl.BlockSpec((1, 1, d, ff), lambda e, t: (layer, e, 0, 0)),
                  pl.BlockSpec((1, 1, ff, d), lambda e, t: (layer, e, 0, 0))],
        out_specs=pl.BlockSpec((1, tm, d), lambda e, t: (e, t, 0)),
        out_shape=jax.ShapeDtypeStruct((n_e, rows, d), BF16),
        compiler_params=_cparams(("arbitrary", "arbitrary")),
        name="expert_ffn",
    )(xs, gates, wg, wu, wd)


def _final_kernel(first_tile, start_ref, npass_ref, x_ref, arow_ref, ys_ref, expand_ref, lanemod_ref,
                  gt2_ref, g_ref, o_ref, buf_ref, sem):
    moe = _combine_tile(pl.program_id(0), pl.program_id(1) + first_tile, start_ref, npass_ref, arow_ref,
                        ys_ref, expand_ref, lanemod_ref, buf_ref, sem)
    x = x_ref[0] + gt2_ref[0][5:6] * moe
    o_ref[0] = _rms(x) * g_ref[...]


def _final(x, comb, modtab, g, ctx_len):
    b, l, d = x.shape
    tm = ROW_TILE
    ctx_tiles = ctx_len // tm
    n_t = (l - ctx_len) // tm
    row = pl.BlockSpec((1, tm, d), lambda b, t, *_: (b, t + ctx_tiles, 0))
    specs, cargs, scratch = _combine_specs(comb, tm, ctx_tiles)
    return pl.pallas_call(
        functools.partial(_final_kernel, ctx_tiles),
        grid_spec=pltpu.PrefetchScalarGridSpec(
            num_scalar_prefetch=2, grid=(b, n_t),
            in_specs=[row] + specs + [pl.BlockSpec((1, 6, d), lambda b, t, *_: (2 * b + 1, 0, 0)),
                                      pl.BlockSpec((1, d), lambda b, t, *_: (0, 0))],
            out_specs=pl.BlockSpec((1, tm, d), lambda b, t, *_: (b, t, 0)),
            scratch_shapes=scratch),
        out_shape=jax.ShapeDtypeStruct((b, l - ctx_len, d), F32),
        compiler_params=_cparams(("arbitrary", "arbitrary")),
        name="final_norm",
    )(comb[0], comb[1], x, *cargs, modtab, g.reshape(1, d))


def _rope_tables(rows, ctx_len):
    n_freq = HEAD_DIM // 4
    inv = ROPE_BASE ** (-jnp.arange(n_freq, dtype=F32) / n_freq)
    r = jnp.repeat(jnp.arange(rows, dtype=F32), GRID_W)
    cc = jnp.tile(jnp.arange(GRID_W, dtype=F32), rows)
    lat = jnp.concatenate([r[:, None] * inv, cc[:, None] * inv], axis=-1)
    ang = jnp.concatenate([jnp.zeros((ctx_len, 2 * n_freq), F32), lat], axis=0)
    cos = jnp.cos(ang)
    sin = jnp.sin(ang)
    cos_rep = jnp.repeat(cos, 2, axis=-1)
    sin_signed = jnp.stack([-sin, sin], axis=-1).reshape(ang.shape[0], HEAD_DIM)
    return cos_rep, sin_signed


def _select_kernel(nb, cap, aff_ref, tri_ref, pos_ref, off_ref, tot_ref, cum_ref):
    n_e = aff_ref.shape[2]
    bits = lax.bitcast_convert_type(aff_ref[0], jnp.int32)
    ones = jnp.ones((LANES, LANES), BF16)
    tri = tri_ref[...]

    def total(flags):
        return jnp.dot(jnp.sum(flags, axis=0).astype(BF16), ones, preferred_element_type=F32)

    def thr_step(i, thr):
        cand = thr | (jnp.int32(1) << (30 - i))
        cnt = total((bits >= cand[None]).astype(F32))
        return jnp.where(cnt >= cap, cand, thr)

    thr = lax.fori_loop(0, 31, thr_step, jnp.zeros((n_e, LANES), jnp.int32))

    def prefix(flags):
        f2 = flags.reshape(nb * n_e, LANES).astype(BF16)
        incl = jnp.dot(f2, tri, preferred_element_type=F32).reshape(nb, n_e, LANES)
        tot_ref[...] = jnp.dot(f2, ones, preferred_element_type=F32).reshape(nb, n_e, LANES)

        def step(k, carry):
            cum_ref[k] = carry
            return carry + tot_ref[k]

        lax.fori_loop(0, nb, step, jnp.zeros((n_e, LANES), F32))
        return incl, cum_ref[...]

    gt = (bits > thr[None]).astype(F32)
    eq = (bits == thr[None]).astype(F32)
    need = cap - total(gt)
    eq_incl, eq_off = prefix(eq)
    sel = gt + eq * ((eq_off + eq_incl - eq) < need[None]).astype(F32)
    incl, off = prefix(sel)
    pos_ref[0] = jnp.where(sel > 0.5, off + incl - 1.0, -1.0)
    off_ref[0] = off


def _compact_kernel(nb, off_ref, pos_ref, aff_ref, out_ref):
    b = pl.program_id(0)
    n_e = pos_ref.shape[2]
    out_ref[...] = jnp.zeros_like(out_ref)
    slot_row = lax.broadcasted_iota(jnp.int32, (LANES, LANES), 0).astype(F32)
    lane8 = lax.broadcasted_iota(jnp.int32, (8, LANES), 1)
    row8 = lax.broadcasted_iota(jnp.int32, (8, LANES), 0)
    lane_f = lane8.astype(F32)

    def block(k, carry):
        for e in range(n_e):
            start = off_ref[b, k, e]
            blk = start >> 7
            p = pos_ref[0, k, e:e + 1, :]
            g = aff_ref[0, k, e:e + 1, :]
            r = p - jnp.asarray(blk * LANES, F32)
            r = jnp.where(r >= LANES, r - LANES, r)
            onehot = jnp.where((slot_row == r) & (p >= 0.0), 1.0, 0.0).astype(BF16)
            g_hi = g.astype(BF16).astype(F32)
            g_mid = (g - g_hi).astype(BF16).astype(F32)
            g_lo = (g - g_hi - g_mid).astype(BF16).astype(F32)
            vals = jnp.where(row8 == 0, lane_f,
                             jnp.where(row8 == 1, 1.0,
                                       jnp.where(row8 == 2, g_hi,
                                                 jnp.where(row8 == 3, g_mid,
                                                           jnp.where(row8 == 4, g_lo, 0.0)))))
            c = _nt_dot(vals.astype(BF16), onehot)
            tok = c[1:2] * jnp.asarray(k * LANES, F32) + c[0:1]
            gate = c[2:3] + c[3:4] + c[4:5]
            tile = jnp.where(row8 == 0, tok, jnp.where(row8 == 1, gate, 0.0))
            first = lane8 >= (start & (LANES - 1))
            out_ref[0, e, blk] += jnp.where(first, tile, 0.0)
            out_ref[0, e, blk + 1] += jnp.where(first, 0.0, tile)
        return carry

    lax.fori_loop(0, nb, block, 0)


def _route(aff_blocks, cap):
    bsz, nb, n_e, _ = aff_blocks.shape
    assert nb <= 256
    tri = jnp.asarray(np.triu(np.ones((LANES, LANES), np.float32)), BF16)
    blk = pl.BlockSpec((1, nb, n_e, LANES), lambda b: (b, 0, 0, 0))
    shp = jax.ShapeDtypeStruct((bsz, nb, n_e, LANES), F32)
    pos, off_rep = pl.pallas_call(
        functools.partial(_select_kernel, nb, cap),
        grid=(bsz,),
        in_specs=[blk, pl.BlockSpec((LANES, LANES), lambda b: (0, 0))],
        out_specs=[blk, blk],
        out_shape=[shp, shp],
        scratch_shapes=[pltpu.VMEM((nb, n_e, LANES), F32), pltpu.VMEM((nb, n_e, LANES), F32)],
        compiler_params=_cparams(("arbitrary",)),
        name="route_select",
    )(aff_blocks, tri)
    off = off_rep[:, :, :, 0].astype(jnp.int32)
    n_rows = cap // LANES + 2
    blk1 = pl.BlockSpec((1, nb, n_e, LANES), lambda b, off: (b, 0, 0, 0))
    lists = pl.pallas_call(
        functools.partial(_compact_kernel, nb),
        grid_spec=pltpu.PrefetchScalarGridSpec(
            num_scalar_prefetch=1,
            grid=(bsz,),
            in_specs=[blk1, blk1],
            out_specs=pl.BlockSpec((1, n_e, n_rows, 8, LANES), lambda b, off: (b, 0, 0, 0, 0))),
        out_shape=jax.ShapeDtypeStruct((bsz, n_e, n_rows, 8, LANES), F32),
        compiler_params=_cparams(("arbitrary",)),
        name="route_compact",
    )(off, pos, aff_blocks)
    tokens = lists[:, :, :, 0, :].reshape(bsz, n_e, n_rows * LANES)[:, :, :cap].astype(jnp.int32)
    gates = lists[:, :, :, 1, :].reshape(bsz, n_e, n_rows * LANES)[:, :, :cap]
    return gates, tokens, pos, off


def kernel(x, c, ctx, c_ctx, w_ada, b_ada, g_mix, g_ffn, w_in, w_out, hg_lb_logits, hg_onorm,
           ret_decay, w_router, w_e_gate, w_e_up, w_e_down, g_final):
    bsz, n, d = x.shape
    lc = ctx.shape[1]
    depth = w_ada.shape[0]
    l = lc + n
    hg_width = hg_lb_logits.shape[1]
    ret_width = d - hg_width
    n_e = w_router.shape[2]
    assert lc % ROW_TILE == 0 and n % ROW_TILE == 0 and lc % SCAN_CHUNK == 0 and n % SCAN_CHUNK == 0
    assert hg_width % HEAD_DIM == 0 and ret_width % HEAD_DIM == 0 and n % GRID_W == 0
    assert w_in.shape[2] == 5 * hg_width + 4 * ret_width and hg_width == ret_width

    cos_rep, sin_signed = _rope_tables(n // GRID_W, lc)
    w_tab, mask_tab = _decay_tables()

    gamma_cum = jnp.cumsum(jax.nn.softmax(hg_lb_logits.astype(F32), axis=0), axis=0)
    lbs = gamma_cum - gamma_cum[0:1]
    lb_tabs = jnp.stack([jnp.log(lbs), jnp.log1p(-lbs)], axis=1)
    log_gammas = -jnp.exp(ret_decay.astype(F32))

    cond8 = jnp.zeros((8, d), F32).at[:bsz].set(c).at[bsz].set(c_ctx)
    mods = _adaln(cond8, w_ada, b_ada).reshape(depth, 8, 6, d)
    modtabs = jnp.stack([jnp.broadcast_to(mods[:, bsz:bsz + 1], (depth, bsz, 6, d)), mods[:, :bsz]],
                        axis=2).reshape(depth, 2 * bsz, 6, d)

    cap_lat = CAPACITY_FACTOR * n // n_e
    cap_ctx = CAPACITY_FACTOR * lc // n_e
    boff = (jnp.arange(bsz, dtype=jnp.int32) * l)[:, None, None]

    w_in16, w_out16 = w_in.astype(BF16), w_out.astype(BF16)
    wg16, wu16, wd16 = w_e_gate.astype(BF16), w_e_up.astype(BF16), w_e_down.astype(BF16)
    xs = jnp.concatenate([ctx, x], axis=1)
    comb = None
    for layer in range(depth):
        last = layer == depth - 1
        xs, p = _inproj(xs, comb, modtabs[layer - 1] if layer else None, modtabs[layer], g_mix[layer],
                        w_in16, layer, lc)
        hf, hb = _hgrn_scan(p, lb_tabs[layer], w_tab, mask_tab, lc, hg_width)
        rf, rb = _ret_scan(p, log_gammas[layer], cos_rep, sin_signed, lc, ret_width, 5)
        xs, h2, aff_t = _outproj(xs, hf, hb, rf, rb, p, modtabs[layer], hg_onorm[layer],
                                 w_out16, layer, g_ffn[layer], w_router[layer], lc, hg_width)
        blocks = [lc // LANES, n // LANES]
        caps = [cap_ctx, cap_lat]
        bases = [bsz * cap_lat + jnp.arange(bsz, dtype=jnp.int32) * cap_ctx,
                 jnp.arange(bsz, dtype=jnp.int32) * cap_lat]
        affs = [aff_t[:, :blocks[0]], aff_t[:, blocks[0]:]]
        tok_off = [0, lc]
        rows, gts, arows, starts, cnts = {}, {}, [], [], []
        for seg in (0, 1):
            if seg == 0 and last:
                arows.append(jnp.full((bsz, blocks[0], n_e, LANES), -1.0, F32))
                starts.append(jnp.zeros((bsz, blocks[0], n_e), jnp.int32))
                cnts.append(jnp.zeros((bsz, blocks[0], n_e), jnp.int32))
                continue
            g_s, i_s, pos_s, off_s = _route(affs[seg], caps[seg])
            rows[seg] = i_s + tok_off[seg] + boff
            gts[seg] = g_s
            kept = pos_s >= 0.0
            arows.append(jnp.where(kept, pos_s + bases[seg][:, None, None, None].astype(F32), -1.0))
            starts.append(off_s + bases[seg][:, None, None])
            cnts.append(jnp.sum(kept, axis=-1).astype(jnp.int32))
        order = [s_ for s_ in (1, 0) if s_ in rows]
        flat = jnp.concatenate([rows[s_].transpose(1, 0, 2).reshape(n_e, -1) for s_ in order], axis=1)
        gate = jnp.concatenate([gts[s_].transpose(1, 0, 2).reshape(n_e, -1) for s_ in order], axis=1)
        n_rows = flat.shape[1]
        pad = (-n_rows) % 16
        if pad:
            flat = jnp.pad(flat, ((0, 0), (0, pad)))
            gate = jnp.pad(gate, ((0, 0), (0, pad)))
        h2f = h2.reshape(bsz * l, d)
        gathered = h2f[flat.reshape(-1)].reshape(n_e, n_rows + pad, d)
        ys = _expert_ffn(gathered, gate[:, :, None], wg16, wu16, wd16, layer)
        arow = jnp.concatenate(arows, axis=1)
        start = jnp.concatenate(starts, axis=1)
        nbt = ROW_TILE // LANES
        start_t = start[:, ::nbt]
        cnt_t = jnp.concatenate(cnts, axis=1).reshape(bsz, -1, nbt, n_e).sum(axis=2)
        need = start_t % YS_ALIGN + cnt_t
        npass = jnp.max(jnp.where(cnt_t > 0, -(-need // COMBINE_ROWS), 0), axis=-1).astype(jnp.int32)
        comb = (start.reshape(bsz, -1), npass, arow, ys)
    return _final(xs, comb, modtabs[depth - 1], g_final, lc)
```

```python
import functools

import numpy as np
import jax
import jax.numpy as jnp
from jax import lax
from jax.experimental import pallas as pl
from jax.experimental.pallas import tpu as pltpu

F32 = jnp.float32
BF16 = jnp.bfloat16

EPS = 1e-6
GRID_W = 64
ROPE_BASE = 10000.0
N_EXPERTS = 16
CAPACITY_FACTOR = 2
HEAD_DIM = 128
LANES = 128
SCAN_CHUNK = 128
ROW_TILE = 256
VMEM_LIMIT = 56 * 1024 * 1024

P_HQ, P_HI, P_HGT, P_RQ, P_RK, P_RV, P_RGT = range(7)

N_LEVELS = int(np.log2(SCAN_CHUNK))
SMALL_HALVES = (4, 2)
W_ROWS = (1 + len(SMALL_HALVES)) * SCAN_CHUNK
LOG2E = float(np.log2(np.e))


def _cparams(sem):
    return pltpu.CompilerParams(dimension_semantics=sem, vmem_limit_bytes=VMEM_LIMIT)


def _nt_dot(a, b):
    return lax.dot_general(a, b, (((1,), (1,)), ((), ())), preferred_element_type=F32)


def _tn_dot(a, b):
    return lax.dot_general(a, b, (((0,), (0,)), ((), ())), preferred_element_type=F32)


def _sigmoid(x):
    return 1.0 / (1.0 + jnp.exp(-x))


def _silu(x):
    return x * _sigmoid(x)


def _adaln_kernel(cond_ref, w_ref, b_ref, o_ref):
    s = _silu(cond_ref[...])
    o_ref[0] = jnp.dot(s.astype(BF16), w_ref[0].astype(BF16), preferred_element_type=F32) + b_ref[0]


def _adaln(cond8, w_ada, b_ada):
    depth, d, n6 = w_ada.shape
    tn = n6 // 4
    return pl.pallas_call(
        _adaln_kernel,
        grid=(depth, n6 // tn),
        in_specs=[pl.BlockSpec((8, d), lambda l, j: (0, 0)),
                  pl.BlockSpec((1, d, tn), lambda l, j: (l, 0, j)),
                  pl.BlockSpec((1, 1, tn), lambda l, j: (l, 0, j))],
        out_specs=pl.BlockSpec((1, 8, tn), lambda l, j: (l, 0, j)),
        out_shape=jax.ShapeDtypeStruct((depth, 8, n6), F32),
        compiler_params=_cparams(("arbitrary", "arbitrary")),
        name="adaln",
    )(cond8, w_ada, b_ada.reshape(depth, 1, n6))


def _rms(x):
    return x * lax.rsqrt(jnp.mean(x * x, axis=-1, keepdims=True) + EPS)


COMBINE_ROWS = 64
YS_ALIGN = 16


def _combine_tile(first_tile, start_ref, npass_ref, arow_ref, ys_ref, expand_ref, lanemod_ref, buf_ref, sem):
    n_e = arow_ref.shape[2]
    n_blk = arow_ref.shape[1]
    tm = n_blk * LANES
    wn = COMBINE_ROWS
    n_rows = ys_ref.shape[1]
    d = ys_ref.shape[2]
    b, t, n_t = pl.program_id(0), pl.program_id(1), pl.num_programs(1)
    step = b * n_t + t
    n_steps = pl.num_programs(0) * n_t
    slot = step % 2
    wrap = t + 1 == n_t
    b_next = jnp.where(wrap, b + 1, b)
    t_next = jnp.where(wrap, 0, t + 1)

    def window(bb, tt, e, p):
        begin = (start_ref[bb, (tt + first_tile) * n_blk * n_e + e] // YS_ALIGN) * YS_ALIGN
        src = jnp.minimum(begin + p * wn, n_rows - wn)
        return begin, pl.multiple_of(src, YS_ALIGN)

    def copy(to_slot, src, e):
        return pltpu.make_async_copy(ys_ref.at[e, pl.ds(src, wn), :],
                                     buf_ref.at[to_slot, pl.ds(e * wn, wn), :], sem.at[to_slot, e])

    @pl.when(step == 0)
    def _():
        for e in range(n_e):
            copy(slot, window(b, t, e, 0)[1], e).start()

    @pl.when(step + 1 < n_steps)
    def _():
        for e in range(n_e):
            copy(1 - slot, window(b_next, t_next, e, 0)[1], e).start()

    arow = jnp.concatenate([arow_ref[0, i] for i in range(n_blk)], axis=1)
    erow = lax.broadcasted_iota(jnp.int32, (n_e, tm), 0)

    def per_expert(values):
        out = jnp.zeros((n_e, tm), F32)
        for e in range(n_e):
            out = jnp.where(erow == e, values[e].astype(F32), out)
        return out

    rel0 = arow - per_expert([window(b, t, e, 0)[0] for e in range(n_e)])

    def place(p, srcs, acc):
        lo = jnp.asarray(p * wn, F32)
        mine = (arow >= 0.0) & (rel0 >= lo) & (rel0 < lo + wn)
        rel = jnp.where(mine, arow - per_expert(srcs), 255.0).astype(BF16)
        spread = _tn_dot(rel, expand_ref[...])
        onehot = jnp.where(spread == lanemod_ref[...], 1.0, 0.0).astype(BF16)
        for e in range(n_e):
            copy(slot, srcs[e], e).wait()
        return acc + jnp.dot(onehot, buf_ref[slot], preferred_element_type=F32)

    acc = place(0, [window(b, t, e, 0)[1] for e in range(n_e)], jnp.zeros((tm, d), F32))

    def extra_pass(p, acc):
        srcs = [window(b, t, e, p)[1] for e in range(n_e)]
        for e in range(n_e):
            copy(slot, srcs[e], e).start()
        return place(p, srcs, acc)

    return lax.fori_loop(1, npass_ref[b, t + first_tile], extra_pass, acc)


def _inproj_kernel(has_moe, *refs):
    if has_moe:
        (start_ref, npass_ref, x_ref, arow_ref, ys_ref, expand_ref, lanemod_ref, gt2_ref, mod_ref, g_ref, w_ref,
         xo_ref, pf_ref, p_ref, buf_ref, sem) = refs
        moe = _combine_tile(0, start_ref, npass_ref, arow_ref, ys_ref, expand_ref, lanemod_ref, buf_ref, sem)
        x = x_ref[0] + gt2_ref[0][5:6] * moe
        xo_ref[0] = x
    else:
        x_ref, mod_ref, g_ref, w_ref, pf_ref, p_ref = refs
        x = x_ref[0]
    mod = mod_ref[0]
    h = (_rms(x) * g_ref[...]) * (1.0 + mod[1:2]) + mod[0:1]
    res = jnp.dot(h.astype(BF16), w_ref[0], preferred_element_type=F32)
    n_f = pf_ref.shape[2]
    pf_ref[0] = res[:, :n_f]
    p_ref[0] = res[:, n_f:].astype(BF16)


def _seg_spec(d, ctx_tiles):
    return pl.BlockSpec((1, 6, d), lambda b, t, *_: (2 * b + jnp.where(t < ctx_tiles, 0, 1), 0, 0))


def _combine_consts(n_e):
    wn = COMBINE_ROWS
    expand = np.zeros((n_e, n_e * wn), np.float32)
    for e in range(n_e):
        expand[e, e * wn:(e + 1) * wn] = 1.0
    lanemod = (np.arange(n_e * wn) % wn).astype(np.float32)[None, :]
    return jnp.asarray(expand, BF16), jnp.asarray(lanemod, F32)


def _combine_specs(comb, tm, first_tile):
    start, npass, arow, ys = comb
    n_e = arow.shape[2]
    expand, lanemod = _combine_consts(n_e)
    k = n_e * COMBINE_ROWS
    specs = [pl.BlockSpec((1, tm // LANES, n_e, LANES), lambda b, t, *_: (b, t + first_tile, 0, 0)),
             pl.BlockSpec(memory_space=pl.ANY),
             pl.BlockSpec((n_e, k), lambda b, t, *_: (0, 0)),
             pl.BlockSpec((1, k), lambda b, t, *_: (0, 0))]
    scratch = [pltpu.VMEM((2, k, ys.shape[2]), BF16), pltpu.SemaphoreType.DMA((2, n_e))]
    return specs, [arow, ys, expand, lanemod], scratch


def _inproj(x, comb, modtab_prev, modtab, g, w_bf16, layer, ctx_len, n_f32):
    b, l, d = x.shape
    n = w_bf16.shape[2]
    tm = ROW_TILE
    ctx_tiles = ctx_len // tm
    row = pl.BlockSpec((1, tm, d), lambda b, t, *_: (b, t, 0))
    has_moe = comb is not None
    in_specs = [row]
    args = [x]
    scratch = []
    prefetch = []
    if has_moe:
        specs, cargs, scratch = _combine_specs(comb, tm, 0)
        in_specs += specs + [_seg_spec(d, ctx_tiles)]
        args += cargs + [modtab_prev]
        prefetch = [comb[0], comb[1]]
    in_specs += [_seg_spec(d, ctx_tiles),
                 pl.BlockSpec((1, d), lambda b, t, *_: (0, 0)),
                 pl.BlockSpec((1, d, n), lambda b, t, *_: (layer, 0, 0))]
    args += [modtab, g.reshape(1, d), w_bf16]
    out_specs = [pl.BlockSpec((1, tm, n_f32), lambda b, t, *_: (b, t, 0)),
                 pl.BlockSpec((1, tm, n - n_f32), lambda b, t, *_: (b, t, 0))]
    out_shape = [jax.ShapeDtypeStruct((b, l, n_f32), F32), jax.ShapeDtypeStruct((b, l, n - n_f32), BF16)]
    if has_moe:
        out_specs = [row] + out_specs
        out_shape = [jax.ShapeDtypeStruct((b, l, d), F32)] + out_shape
    res = pl.pallas_call(
        functools.partial(_inproj_kernel, has_moe),
        grid_spec=pltpu.PrefetchScalarGridSpec(
            num_scalar_prefetch=len(prefetch), grid=(b, l // tm),
            in_specs=in_specs, out_specs=out_specs, scratch_shapes=scratch),
        out_shape=out_shape,
        compiler_params=_cparams(("arbitrary", "arbitrary")),
        name="inproj",
    )(*prefetch, *args)
    if has_moe:
        return res[0], res[1], res[2]
    return x, res[0], res[1]


def _bwd_chunk(j, ctx_chunks, n_chunks):
    return jnp.where(j < ctx_chunks, ctx_chunks - 1 - j, n_chunks - 1 + ctx_chunks - j)


def _decay_tables():
    c = SCAN_CHUNK
    w = np.zeros((W_ROWS, c), np.float32)
    masks = np.zeros((N_LEVELS + 1, c, c), np.float32)
    for lev in range(N_LEVELS):
        m = c >> (lev + 1)
        for t in range(c):
            g0 = (t // (2 * m)) * 2 * m
            bnd = g0 + m - 1
            if t > bnd:
                masks[lev, t, g0:bnd + 1] = 1.0
            if m in SMALL_HALVES:
                r0 = (1 + SMALL_HALVES.index(m)) * c
                if t > bnd:
                    w[r0 + t, bnd + 1:t + 1] = 1.0
                else:
                    w[r0 + t, t + 1:bnd + 1] = 1.0
    masks[N_LEVELS] = np.eye(c, dtype=np.float32)
    for t in range(c):
        w[t, :t + 1] = 1.0
    w_b = w.reshape(-1, c, c)[:, ::-1, ::-1].reshape(-1, c)
    masks_b = masks[:, ::-1, ::-1]
    return (jnp.asarray(np.stack([w, w_b]), BF16), jnp.asarray(np.stack([masks, masks_b]), F32))


def _hgrn_kernel(n_heads, qf_ref, ff_ref, if_ref, qb_ref, fb_ref, ib_ref, lb_ref, w_ref, mask_ref,
                 of_ref, ob_ref, st_ref, a_ref):
    c = SCAN_CHUNK

    @pl.when(pl.program_id(1) == 0)
    def _():
        st_ref[...] = jnp.zeros_like(st_ref)

    dirs = ((qf_ref, ff_ref, if_ref, of_ref), (qb_ref, fb_ref, ib_ref, ob_ref))
    for d, (q_ref, f_ref, i_ref, o_ref) in enumerate(dirs):
        for h in range(n_heads):
            sl = slice(h * HEAD_DIM, (h + 1) * HEAD_DIM)
            qb16 = q_ref[0, :, sl]
            q = qb16.astype(F32)
            xf = f_ref[0, :, sl]
            v = i_ref[0, :, sl]
            b_ = lb_ref[1:2, sl] + (jnp.minimum(xf, 0.0) - jnp.log(1.0 + jnp.exp(-jnp.abs(xf))))
            a_ = lb_ref[0:1, sl]
            lf2 = (jnp.maximum(a_, b_) + jnp.log(1.0 + jnp.exp(-jnp.abs(a_ - b_)))) * LOG2E
            f = jnp.exp2(lf2)
            k = 1.0 - f
            hi = lf2.astype(BF16)
            lo = (lf2 - hi.astype(F32)).astype(BF16)
            ex2 = jnp.dot(w_ref[d], jnp.concatenate([hi, lo], axis=1), preferred_element_type=F32)
            ex = ex2[:, :HEAD_DIM] + ex2[:, HEAD_DIM:]
            cum = ex[:c]
            a_ref[d, h] = cum
            kb16 = k.astype(BF16)
            scores = _nt_dot(qb16, kb16) * mask_ref[d, N_LEVELS]
            for lev in range(N_LEVELS):
                m = c >> (lev + 1)
                if m == 1:
                    lhs, rhs = (q * f).astype(BF16), kb16
                else:
                    if m in SMALL_HALVES:
                        r0 = (1 + SMALL_HALVES.index(m)) * c
                        e = jnp.exp2(ex[r0:r0 + c])
                    else:
                        parts = []
                        for g0 in range(0, c, 2 * m):
                            mid = g0 + m - 1 + d
                            parts.append(cum[g0:g0 + 2 * m] - a_ref[d, h, mid:mid + 1, :])
                        e = jnp.exp2(-jnp.abs(jnp.concatenate(parts, axis=0) if len(parts) > 1 else parts[0]))
                    lhs, rhs = (q * e).astype(BF16), (k * e).astype(BF16)
                scores = scores + _nt_dot(lhs, rhs) * mask_ref[d, lev]
            edge = (c - 1) * (1 - d)
            tot = a_ref[d, h, edge:edge + 1, :]
            st = st_ref[d, h]
            out = jnp.dot(scores.astype(BF16), v, preferred_element_type=F32)
            out = out + _nt_dot((q * jnp.exp2(cum)).astype(BF16), st.astype(BF16))
            o_ref[0, :, sl] = out.astype(BF16)
            kdec = (k * jnp.exp2(tot - cum)).astype(BF16)
            st_ref[d, h] = st * jnp.exp2(tot) + _tn_dot(v, kdec)


def _hgrn_scan(pf, p, lb_tab, w_tab, mask_tab, ctx_len, width):
    b, l, _ = p.shape
    c = SCAN_CHUNK
    n_chunks = l // c
    ctx_chunks = ctx_len // c
    n_heads = width // HEAD_DIM

    def fwd(g):
        return pl.BlockSpec((1, c, width), lambda b, j: (b, j, g))

    def bwd(g):
        return pl.BlockSpec((1, c, width), lambda b, j: (b, _bwd_chunk(j, ctx_chunks, n_chunks), g))

    out_shape = jax.ShapeDtypeStruct((b, l, width), BF16)
    return pl.pallas_call(
        functools.partial(_hgrn_kernel, n_heads),
        grid=(b, n_chunks),
        in_specs=[fwd(P_HQ), fwd(0), fwd(P_HI), bwd(P_HQ), bwd(1), bwd(P_HI),
                  pl.BlockSpec((2, width), lambda b, j: (0, 0)),
                  pl.BlockSpec(w_tab.shape, lambda b, j: (0, 0, 0)),
                  pl.BlockSpec(mask_tab.shape, lambda b, j: (0, 0, 0, 0))],
        out_specs=[fwd(0), bwd(0)],
        out_shape=[out_shape, out_shape],
        scratch_shapes=[pltpu.VMEM((2, n_heads, HEAD_DIM, HEAD_DIM), F32),
                        pltpu.VMEM((2, n_heads, c, HEAD_DIM), F32)],
        compiler_params=_cparams(("arbitrary", "arbitrary")),
        name="hgrn_scan",
    )(p, pf, p, p, pf, p, lb_tab, w_tab, mask_tab)


def _rope(z, cos_rep, sin_signed, even_lane):
    partner = jnp.where(even_lane, pltpu.roll(z, LANES - 1, 1), pltpu.roll(z, 1, 1))
    return z * cos_rep + partner * sin_signed


def _ret_kernel(n_heads, lg_ref, qf_ref, kf_ref, vf_ref, cf_ref, sf_ref,
                qb_ref, kb_ref, vb_ref, cb_ref, sb_ref, of_ref, ob_ref, st_ref, dm_ref, in_ref, tl_ref):
    c = SCAN_CHUNK
    k_scale = HEAD_DIM ** -0.5

    @pl.when(pl.program_id(1) == 0)
    def _():
        st_ref[...] = jnp.zeros_like(st_ref)
        ti = lax.broadcasted_iota(jnp.int32, (c, c), 0)
        si = lax.broadcasted_iota(jnp.int32, (c, c), 1)
        rowf = lax.broadcasted_iota(jnp.int32, (c, HEAD_DIM), 0).astype(F32)
        for d in range(2):
            rel = (ti - si) if d == 0 else (si - ti)
            relf = jnp.maximum(rel, 0).astype(F32)
            for h in range(n_heads):
                lg = lg_ref[d, h]
                dm_ref[d, h] = jnp.where(rel >= 0, jnp.exp(relf * lg), 0.0)
                if d == 0:
                    in_ref[d, h] = jnp.exp((rowf + 1.0) * lg)
                    tl_ref[d, h] = jnp.exp((c - 1.0 - rowf) * lg)
                else:
                    in_ref[d, h] = jnp.exp((c - rowf) * lg)
                    tl_ref[d, h] = jnp.exp(rowf * lg)

    even_lane = (lax.broadcasted_iota(jnp.int32, (c, HEAD_DIM), 1) & 1) == 0
    dirs = ((qf_ref, kf_ref, vf_ref, cf_ref, sf_ref, of_ref), (qb_ref, kb_ref, vb_ref, cb_ref, sb_ref, ob_ref))
    for d, (q_ref, k_ref, v_ref, c_ref, s_ref, o_ref) in enumerate(dirs):
        cos_rep = c_ref[...]
        sin_signed = s_ref[...]
        for h in range(n_heads):
            sl = slice(h * HEAD_DIM, (h + 1) * HEAD_DIM)
            q = _rope(q_ref[0, :, sl].astype(F32), cos_rep, sin_signed, even_lane)
            k = _rope(k_ref[0, :, sl].astype(F32), cos_rep, sin_signed, even_lane) * k_scale
            v = v_ref[0, :, sl]
            qb16 = q.astype(BF16)
            scores = _nt_dot(qb16, k.astype(BF16)) * dm_ref[d, h]
            st = st_ref[d, h]
            out = jnp.dot(scores.astype(BF16), v, preferred_element_type=F32)
            out = out + _nt_dot(qb16, st.astype(BF16)) * in_ref[d, h]
            o_ref[0, :, sl] = out.astype(BF16)
            edge = (c - 1) * (1 - d)
            st_ref[d, h] = (st * in_ref[d, h, edge:edge + 1, :]
                            + _tn_dot(v, (k * tl_ref[d, h]).astype(BF16)))


def _ret_scan(p, lg, cos_rep, sin_signed, ctx_len, width):
    b, l, _ = p.shape
    c = SCAN_CHUNK
    n_chunks = l // c
    ctx_chunks = ctx_len // c
    n_heads = width // HEAD_DIM

    def fwd(g):
        return pl.BlockSpec((1, c, width), lambda b, j, lg: (b, j, g))

    def bwd(g):
        return pl.BlockSpec((1, c, width), lambda b, j, lg: (b, _bwd_chunk(j, ctx_chunks, n_chunks), g))

    tab_f = pl.BlockSpec((c, HEAD_DIM), lambda b, j, lg: (j, 0))
    tab_b = pl.BlockSpec((c, HEAD_DIM), lambda b, j, lg: (_bwd_chunk(j, ctx_chunks, n_chunks), 0))
    out_shape = jax.ShapeDtypeStruct((b, l, width), BF16)
    return pl.pallas_call(
        functools.partial(_ret_kernel, n_heads),
        grid_spec=pltpu.PrefetchScalarGridSpec(
            num_scalar_prefetch=1,
            grid=(b, n_chunks),
            in_specs=[fwd(P_RQ), fwd(P_RK), fwd(P_RV), tab_f, tab_f,
                      bwd(P_RQ), bwd(P_RK), bwd(P_RV), tab_b, tab_b],
            out_specs=[fwd(0), bwd(0)],
            scratch_shapes=[pltpu.VMEM((2, n_heads, HEAD_DIM, HEAD_DIM), F32),
                            pltpu.VMEM((2, n_heads, c, c), F32),
                            pltpu.VMEM((2, n_heads, c, HEAD_DIM), F32),
                            pltpu.VMEM((2, n_heads, c, HEAD_DIM), F32)]),
        out_shape=[out_shape, out_shape],
        compiler_params=_cparams(("arbitrary", "arbitrary")),
        name="ret_scan",
    )(lg, p, p, p, cos_rep, sin_signed, p, p, p, cos_rep, sin_signed)


def _outproj_kernel(n_hg, n_ret, x_ref, hf_ref, hb_ref, rf_ref, rb_ref, hg_ref, rg_ref, mod_ref,
                    on_ref, wo_ref, g2_ref, wrh_ref, wrl_ref, xo_ref, h2_ref, aff_ref):
    mod = mod_ref[0]
    o_hg = hf_ref[0].astype(F32) + hb_ref[0].astype(F32)
    o_rt = rf_ref[0].astype(F32) + rb_ref[0].astype(F32)
    parts = []
    for h in range(n_hg):
        sl = slice(h * HEAD_DIM, (h + 1) * HEAD_DIM)
        parts.append((_rms(o_hg[:, sl]) * on_ref[...]) * _silu(hg_ref[0, :, sl].astype(F32)))
    for h in range(n_ret):
        sl = slice(h * HEAD_DIM, (h + 1) * HEAD_DIM)
        parts.append(_rms(o_rt[:, sl]) * _silu(rg_ref[0, :, sl].astype(F32)))
    mix = jnp.concatenate(parts, axis=1).astype(BF16)
    x = x_ref[0] + mod[2:3] * jnp.dot(mix, wo_ref[0], preferred_element_type=F32)
    xo_ref[0] = x
    h2 = (_rms(x) * g2_ref[...]) * (1.0 + mod[4:5]) + mod[3:4]
    h2_ref[0] = h2.astype(BF16)
    hh = h2.astype(BF16)
    hl = (h2 - hh.astype(F32)).astype(BF16)
    logits = _nt_dot(wrh_ref[...], hh) + _nt_dot(wrh_ref[...], hl) + _nt_dot(wrl_ref[...], hh)
    mx = jnp.max(logits, axis=0, keepdims=True)
    ex = jnp.exp(logits - mx)
    aff = ex / jnp.sum(ex, axis=0, keepdims=True)
    for i in range(aff_ref.shape[1]):
        aff_ref[0, i] = aff[:, i * LANES:(i + 1) * LANES]


def _outproj(x, hf, hb, rf, rb, p, modtab, onorm, wo_bf16, layer, g2, w_router, ctx_len, hg_width):
    b, l, d = x.shape
    tm = ROW_TILE
    ctx_tiles = ctx_len // tm
    n_hg = hg_width // HEAD_DIM
    ret_width = d - hg_width
    n_ret = ret_width // HEAD_DIM
    n_e = w_router.shape[1]
    wrt = w_router.T
    wrh = wrt.astype(BF16)
    wrl = (wrt - wrh.astype(F32)).astype(BF16)
    row = pl.BlockSpec((1, tm, d), lambda b, t: (b, t, 0))
    hrow = pl.BlockSpec((1, tm, hg_width), lambda b, t: (b, t, 0))
    rrow = pl.BlockSpec((1, tm, ret_width), lambda b, t: (b, t, 0))
    full = lambda shape: pl.BlockSpec(shape, lambda b, t: tuple(0 for _ in shape))
    return pl.pallas_call(
        functools.partial(_outproj_kernel, n_hg, n_ret),
        grid=(b, l // tm),
        in_specs=[row, hrow, hrow, rrow, rrow,
                  pl.BlockSpec((1, tm, hg_width), lambda b, t: (b, t, P_HGT)),
                  pl.BlockSpec((1, tm, ret_width), lambda b, t: (b, t, P_RGT)),
                  _seg_spec(d, ctx_tiles),
                  full((1, HEAD_DIM)), pl.BlockSpec((1, d, d), lambda b, t: (layer, 0, 0)),
                  full((1, d)), full((n_e, d)), full((n_e, d))],
        out_specs=[row, row, pl.BlockSpec((1, tm // LANES, n_e, LANES), lambda b, t: (b, t, 0, 0))],
        out_shape=[jax.ShapeDtypeStruct((b, l, d), F32),
                   jax.ShapeDtypeStruct((b, l, d), BF16),
                   jax.ShapeDtypeStruct((b, l // LANES, n_e, LANES), F32)],
        compiler_params=_cparams(("arbitrary", "arbitrary")),
        name="outproj",
    )(x, hf, hb, rf, rb, p, p, modtab, onorm.reshape(1, HEAD_DIM), wo_bf16, g2.reshape(1, d), wrh, wrl)


def _ffn_kernel(x_ref, gate_ref, wg_ref, wu_ref, wd_ref, o_ref):
    x = x_ref[0]
    a = jnp.dot(x, wg_ref[0, 0], preferred_element_type=F32)
    u = jnp.dot(x, wu_ref[0, 0], preferred_element_type=F32)
    hid = (_silu(a) * u).astype(BF16)
    o_ref[0] = (jnp.dot(hid, wd_ref[0, 0], preferred_element_type=F32) * gate_ref[0]).astype(BF16)


def _row_tile(rows, cap=512):
    best = 16
    for t in range(16, cap + 1, 16):
        if rows % t == 0:
            best = t
    return best


def _expert_ffn(xs, gates, wg, wu, wd, layer):
    n_e, rows, d = xs.shape
    ff = wg.shape[3]
    tm = _row_tile(rows)
    return pl.pallas_call(
        _ffn_kernel,
        grid=(n_e, rows // tm),
        in_specs=[pl.BlockSpec((1, tm, d), lambda e, t: (e, t, 0)),
                  pl.BlockSpec((1, tm, 1), lambda e, t: (e, t, 0)),
                  pl.BlockSpec((1, 1, d, ff), lambda e, t: (layer, e, 0, 0)),
                  pl.BlockSpec((1, 1, d, ff), lambda e, t: (layer, e, 0, 0)),
                  pl.BlockSpec((1, 1, ff, d), lambda e, t: (layer, e, 0, 0))],
        out_specs=pl.BlockSpec((1, tm, d), lambda e, t: (e, t, 0)),
        out_shape=jax.ShapeDtypeStruct((n_e, rows, d), BF16),
        compiler_params=_cparams(("arbitrary", "arbitrary")),
        name="expert_ffn",
    )(xs, gates, wg, wu, wd)


def _final_kernel(first_tile, start_ref, npass_ref, x_ref, arow_ref, ys_ref, expand_ref, lanemod_ref,
                  gt2_ref, g_ref, o_ref, buf_ref, sem):
    moe = _combine_tile(first_tile, start_ref, npass_ref, arow_ref, ys_ref, expand_ref, lanemod_ref, buf_ref, sem)
    x = x_ref[0] + gt2_ref[0][5:6] * moe
    o_ref[0] = _rms(x) * g_ref[...]


def _final(x, comb, modtab, g, ctx_len):
    b, l, d = x.shape
    tm = ROW_TILE
    ctx_tiles = ctx_len // tm
    n_t = (l - ctx_len) // tm
    row = pl.BlockSpec((1, tm, d), lambda b, t, *_: (b, t + ctx_tiles, 0))
    specs, cargs, scratch = _combine_specs(comb, tm, ctx_tiles)
    return pl.pallas_call(
        functools.partial(_final_kernel, ctx_tiles),
        grid_spec=pltpu.PrefetchScalarGridSpec(
            num_scalar_prefetch=2, grid=(b, n_t),
            in_specs=[row] + specs + [pl.BlockSpec((1, 6, d), lambda b, t, *_: (2 * b + 1, 0, 0)),
                                      pl.BlockSpec((1, d), lambda b, t, *_: (0, 0))],
            out_specs=pl.BlockSpec((1, tm, d), lambda b, t, *_: (b, t, 0)),
            scratch_shapes=scratch),
        out_shape=jax.ShapeDtypeStruct((b, l - ctx_len, d), F32),
        compiler_params=_cparams(("arbitrary", "arbitrary")),
        name="final_norm",
    )(comb[0], comb[1], x, *cargs, modtab, g.reshape(1, d))


def _rope_tables(rows, ctx_len):
    n_freq = HEAD_DIM // 4
    inv = ROPE_BASE ** (-jnp.arange(n_freq, dtype=F32) / n_freq)
    r = jnp.repeat(jnp.arange(rows, dtype=F32), GRID_W)
    cc = jnp.tile(jnp.arange(GRID_W, dtype=F32), rows)
    lat = jnp.concatenate([r[:, None] * inv, cc[:, None] * inv], axis=-1)
    ang = jnp.concatenate([jnp.zeros((ctx_len, 2 * n_freq), F32), lat], axis=0)
    cos = jnp.cos(ang)
    sin = jnp.sin(ang)
    cos_rep = jnp.repeat(cos, 2, axis=-1)
    sin_signed = jnp.stack([-sin, sin], axis=-1).reshape(ang.shape[0], HEAD_DIM)
    return cos_rep, sin_signed


def _select_kernel(nb, cap, aff_ref, tri_ref, pos_ref, off_ref, tot_ref, cum_ref):
    n_e = aff_ref.shape[2]
    bits = lax.bitcast_convert_type(aff_ref[0], jnp.int32)
    ones = jnp.ones((LANES, LANES), BF16)
    tri = tri_ref[...]

    def total(flags):
        return jnp.dot(jnp.sum(flags, axis=0).astype(BF16), ones, preferred_element_type=F32)

    def thr_step(i, thr):
        cand = thr | (jnp.int32(1) << (30 - i))
        cnt = total((bits >= cand[None]).astype(F32))
        return jnp.where(cnt >= cap, cand, thr)

    thr = lax.fori_loop(0, 31, thr_step, jnp.zeros((n_e, LANES), jnp.int32))

    def prefix(flags):
        f2 = flags.reshape(nb * n_e, LANES).astype(BF16)
        incl = jnp.dot(f2, tri, preferred_element_type=F32).reshape(nb, n_e, LANES)
        tot_ref[...] = jnp.dot(f2, ones, preferred_element_type=F32).reshape(nb, n_e, LANES)

        def step(k, carry):
            cum_ref[k] = carry
            return carry + tot_ref[k]

        lax.fori_loop(0, nb, step, jnp.zeros((n_e, LANES), F32))
        return incl, cum_ref[...]

    gt = (bits > thr[None]).astype(F32)
    eq = (bits == thr[None]).astype(F32)
    need = cap - total(gt)
    eq_incl, eq_off = prefix(eq)
    sel = gt + eq * ((eq_off + eq_incl - eq) < need[None]).astype(F32)
    incl, off = prefix(sel)
    pos_ref[0] = jnp.where(sel > 0.5, off + incl - 1.0, -1.0)
    off_ref[0] = off


def _compact_kernel(nb, off_ref, pos_ref, aff_ref, out_ref):
    b = pl.program_id(0)
    n_e = pos_ref.shape[2]
    out_ref[...] = jnp.zeros_like(out_ref)
    slot_row = lax.broadcasted_iota(jnp.int32, (LANES, LANES), 0).astype(F32)
    lane8 = lax.broadcasted_iota(jnp.int32, (8, LANES), 1)
    row8 = lax.broadcasted_iota(jnp.int32, (8, LANES), 0)
    lane_f = lane8.astype(F32)

    def block(k, carry):
        for e in range(n_e):
            start = off_ref[b, k, e]
            blk = start >> 7
            p = pos_ref[0, k, e:e + 1, :]
            g = aff_ref[0, k, e:e + 1, :]
            r = p - jnp.asarray(blk * LANES, F32)
            r = jnp.where(r >= LANES, r - LANES, r)
            onehot = jnp.where((slot_row == r) & (p >= 0.0), 1.0, 0.0).astype(BF16)
            g_hi = g.astype(BF16).astype(F32)
            g_mid = (g - g_hi).astype(BF16).astype(F32)
            g_lo = (g - g_hi - g_mid).astype(BF16).astype(F32)
            vals = jnp.where(row8 == 0, lane_f,
                             jnp.where(row8 == 1, 1.0,
                                       jnp.where(row8 == 2, g_hi,
                                                 jnp.where(row8 == 3, g_mid,
                                                           jnp.where(row8 == 4, g_lo, 0.0)))))
            c = _nt_dot(vals.astype(BF16), onehot)
            tok = c[1:2] * jnp.asarray(k * LANES, F32) + c[0:1]
            gate = c[2:3] + c[3:4] + c[4:5]
            tile = jnp.where(row8 == 0, tok, jnp.where(row8 == 1, gate, 0.0))
            first = lane8 >= (start & (LANES - 1))
            out_ref[0, e, blk] += jnp.where(first, tile, 0.0)
            out_ref[0, e, blk + 1] += jnp.where(first, 0.0, tile)
        return carry

    lax.fori_loop(0, nb, block, 0)


def _route(aff_blocks, cap):
    bsz, nb, n_e, _ = aff_blocks.shape
    assert nb <= 256
    tri = jnp.asarray(np.triu(np.ones((LANES, LANES), np.float32)), BF16)
    blk = pl.BlockSpec((1, nb, n_e, LANES), lambda b: (b, 0, 0, 0))
    shp = jax.ShapeDtypeStruct((bsz, nb, n_e, LANES), F32)
    pos, off_rep = pl.pallas_call(
        functools.partial(_select_kernel, nb, cap),
        grid=(bsz,),
        in_specs=[blk, pl.BlockSpec((LANES, LANES), lambda b: (0, 0))],
        out_specs=[blk, blk],
        out_shape=[shp, shp],
        scratch_shapes=[pltpu.VMEM((nb, n_e, LANES), F32), pltpu.VMEM((nb, n_e, LANES), F32)],
        compiler_params=_cparams(("arbitrary",)),
        name="route_select",
    )(aff_blocks, tri)
    off = off_rep[:, :, :, 0].astype(jnp.int32)
    n_rows = cap // LANES + 2
    blk1 = pl.BlockSpec((1, nb, n_e, LANES), lambda b, off: (b, 0, 0, 0))
    lists = pl.pallas_call(
        functools.partial(_compact_kernel, nb),
        grid_spec=pltpu.PrefetchScalarGridSpec(
            num_scalar_prefetch=1,
            grid=(bsz,),
            in_specs=[blk1, blk1],
            out_specs=pl.BlockSpec((1, n_e, n_rows, 8, LANES), lambda b, off: (b, 0, 0, 0, 0))),
        out_shape=jax.ShapeDtypeStruct((bsz, n_e, n_rows, 8, LANES), F32),
        compiler_params=_cparams(("arbitrary",)),
        name="route_compact",
    )(off, pos, aff_blocks)
    tokens = lists[:, :, :, 0, :].reshape(bsz, n_e, n_rows * LANES)[:, :, :cap].astype(jnp.int32)
    gates = lists[:, :, :, 1, :].reshape(bsz, n_e, n_rows * LANES)[:, :, :cap]
    return gates, tokens, pos, off


def kernel(x, c, ctx, c_ctx, w_ada, b_ada, g_mix, g_ffn, w_in, w_out, hg_lb_logits, hg_onorm,
           ret_decay, w_router, w_e_gate, w_e_up, w_e_down, g_final):
    bsz, n, d = x.shape
    lc = ctx.shape[1]
    depth = w_ada.shape[0]
    l = lc + n
    hg_width = hg_lb_logits.shape[1]
    ret_width = d - hg_width
    n_e = w_router.shape[2]
    assert lc % ROW_TILE == 0 and n % ROW_TILE == 0 and lc % SCAN_CHUNK == 0 and n % SCAN_CHUNK == 0
    assert hg_width % HEAD_DIM == 0 and ret_width % HEAD_DIM == 0 and n % GRID_W == 0
    assert w_in.shape[2] == 5 * hg_width + 4 * ret_width and hg_width == ret_width

    cos_rep, sin_signed = _rope_tables(n // GRID_W, lc)
    w_tab, mask_tab = _decay_tables()

    gamma_cum = jnp.cumsum(jax.nn.softmax(hg_lb_logits.astype(F32), axis=0), axis=0)
    lbs = gamma_cum - gamma_cum[0:1]
    lb_tabs = jnp.stack([jnp.log(lbs), jnp.log1p(-lbs)], axis=1)
    log_gammas = -jnp.exp(ret_decay.astype(F32))

    cond8 = jnp.zeros((8, d), F32).at[:bsz].set(c).at[bsz].set(c_ctx)
    mods = _adaln(cond8, w_ada, b_ada).reshape(depth, 8, 6, d)
    modtabs = jnp.stack([jnp.broadcast_to(mods[:, bsz:bsz + 1], (depth, bsz, 6, d)), mods[:, :bsz]],
                        axis=2).reshape(depth, 2 * bsz, 6, d)

    cap_lat = CAPACITY_FACTOR * n // n_e
    cap_ctx = CAPACITY_FACTOR * lc // n_e
    boff = (jnp.arange(bsz, dtype=jnp.int32) * l)[:, None, None]

    gw = hg_width
    col_groups = [1, 2, 0, 3, 4, 5, 6, 7, 8]
    cols = np.concatenate([np.arange(g * gw, (g + 1) * gw) for g in col_groups])
    w_in16, w_out16 = w_in[:, :, cols].astype(BF16), w_out.astype(BF16)
    wg16, wu16, wd16 = w_e_gate.astype(BF16), w_e_up.astype(BF16), w_e_down.astype(BF16)
    xs = jnp.concatenate([ctx, x], axis=1)
    comb = None
    for layer in range(depth):
        last = layer == depth - 1
        xs, pf, p = _inproj(xs, comb, modtabs[layer - 1] if layer else None, modtabs[layer], g_mix[layer],
                            w_in16, layer, lc, 2 * hg_width)
        hf, hb = _hgrn_scan(pf, p, lb_tabs[layer], w_tab, mask_tab, lc, hg_width)
        rf, rb = _ret_scan(p, log_gammas[layer], cos_rep, sin_signed, lc, ret_width)
        xs, h2, aff_t = _outproj(xs, hf, hb, rf, rb, p, modtabs[layer], hg_onorm[layer],
                                 w_out16, layer, g_ffn[layer], w_router[layer], lc, hg_width)
        blocks = [lc // LANES, n // LANES]
        caps = [cap_ctx, cap_lat]
        bases = [bsz * cap_lat + jnp.arange(bsz, dtype=jnp.int32) * cap_ctx,
                 jnp.arange(bsz, dtype=jnp.int32) * cap_lat]
        affs = [aff_t[:, :blocks[0]], aff_t[:, blocks[0]:]]
        tok_off = [0, lc]
        rows, gts, arows, starts, cnts = {}, {}, [], [], []
        for seg in (0, 1):
            if seg == 0 and last:
                arows.append(jnp.full((bsz, blocks[0], n_e, LANES), -1.0, F32))
                starts.append(jnp.zeros((bsz, blocks[0], n_e), jnp.int32))
                cnts.append(jnp.zeros((bsz, blocks[0], n_e), jnp.int32))
                continue
            g_s, i_s, pos_s, off_s = _route(affs[seg], caps[seg])
            rows[seg] = i_s + tok_off[seg] + boff
            gts[seg] = g_s
            kept = pos_s >= 0.0
            arows.append(jnp.where(kept, pos_s + bases[seg][:, None, None, None].astype(F32), -1.0))
            starts.append(off_s + bases[seg][:, None, None])
            cnts.append(jnp.sum(kept, axis=-1).astype(jnp.int32))
        order = [s_ for s_ in (1, 0) if s_ in rows]
        flat = jnp.concatenate([rows[s_].transpose(1, 0, 2).reshape(n_e, -1) for s_ in order], axis=1)
        gate = jnp.concatenate([gts[s_].transpose(1, 0, 2).reshape(n_e, -1) for s_ in order], axis=1)
        n_rows = flat.shape[1]
        pad = (-n_rows) % 16
        if pad:
            flat = jnp.pad(flat, ((0, 0), (0, pad)))
            gate = jnp.pad(gate, ((0, 0), (0, pad)))
        h2f = h2.reshape(bsz * l, d)
        gathered = h2f[flat.reshape(-1)].reshape(n_e, n_rows + pad, d)
        ys = _expert_ffn(gathered, gate[:, :, None], wg16, wu16, wd16, layer)
        arow = jnp.concatenate(arows, axis=1)
        start = jnp.concatenate(starts, axis=1)
        nbt = ROW_TILE // LANES
        start_t = start[:, ::nbt]
        cnt_t = jnp.concatenate(cnts, axis=1).reshape(bsz, -1, nbt, n_e).sum(axis=2)
        need = start_t % YS_ALIGN + cnt_t
        npass = jnp.max(jnp.where(cnt_t > 0, -(-need // COMBINE_ROWS), 0), axis=-1).astype(jnp.int32)
        comb = (start.reshape(bsz, -1), npass, arow, ys)
    return _final(xs, comb, modtabs[depth - 1], g_final, lc)
```

```python
import functools

import numpy as np
import jax
import jax.numpy as jnp
from jax import lax
from jax.experimental import pallas as pl
from jax.experimental.pallas import tpu as pltpu

F32 = jnp.float32
BF16 = jnp.bfloat16

EPS = 1e-6
GRID_W = 64
ROPE_BASE = 10000.0
N_EXPERTS = 16
CAPACITY_FACTOR = 2
HEAD_DIM = 128
LANES = 128
SCAN_CHUNK = 128
ROW_TILE = 256
VMEM_LIMIT = 56 * 1024 * 1024

P_HQ, P_HI, P_HGT, P_RQ, P_RK, P_RV, P_RGT = range(7)

N_LEVELS = int(np.log2(SCAN_CHUNK))
SMALL_HALVES = (4, 2)
W_ROWS = (1 + len(SMALL_HALVES)) * SCAN_CHUNK
LOG2E = float(np.log2(np.e))


def _cparams(sem):
    return pltpu.CompilerParams(dimension_semantics=sem, vmem_limit_bytes=VMEM_LIMIT)


def _nt_dot(a, b):
    return lax.dot_general(a, b, (((1,), (1,)), ((), ())), preferred_element_type=F32)


def _tn_dot(a, b):
    return lax.dot_general(a, b, (((0,), (0,)), ((), ())), preferred_element_type=F32)


def _sigmoid(x):
    return 1.0 / (1.0 + jnp.exp(-x))


def _silu(x):
    return x * _sigmoid(x)


def _adaln_kernel(cond_ref, w_ref, b_ref, o_ref):
    s = _silu(cond_ref[...])
    o_ref[0] = jnp.dot(s.astype(BF16), w_ref[0].astype(BF16), preferred_element_type=F32) + b_ref[0]


def _adaln(cond8, w_ada, b_ada):
    depth, d, n6 = w_ada.shape
    tn = n6 // 4
    return pl.pallas_call(
        _adaln_kernel,
        grid=(depth, n6 // tn),
        in_specs=[pl.BlockSpec((8, d), lambda l, j: (0, 0)),
                  pl.BlockSpec((1, d, tn), lambda l, j: (l, 0, j)),
                  pl.BlockSpec((1, 1, tn), lambda l, j: (l, 0, j))],
        out_specs=pl.BlockSpec((1, 8, tn), lambda l, j: (l, 0, j)),
        out_shape=jax.ShapeDtypeStruct((depth, 8, n6), F32),
        compiler_params=_cparams(("arbitrary", "arbitrary")),
        name="adaln",
    )(cond8, w_ada, b_ada.reshape(depth, 1, n6))


def _rms(x):
    return x * lax.rsqrt(jnp.mean(x * x, axis=-1, keepdims=True) + EPS)


COMBINE_ROWS = 64
YS_ALIGN = 16


def _combine_tile(first_tile, start_ref, npass_ref, arow_ref, ys_ref, expand_ref, lanemod_ref, buf_ref, sem):
    n_e = arow_ref.shape[2]
    n_blk = arow_ref.shape[1]
    tm = n_blk * LANES
    wn = COMBINE_ROWS
    n_rows = ys_ref.shape[1]
    d = ys_ref.shape[2]
    b, t, n_t = pl.program_id(0), pl.program_id(1), pl.num_programs(1)
    step = b * n_t + t
    n_steps = pl.num_programs(0) * n_t
    slot = step % 2
    wrap = t + 1 == n_t
    b_next = jnp.where(wrap, b + 1, b)
    t_next = jnp.where(wrap, 0, t + 1)

    def window(bb, tt, e, p):
        begin = (start_ref[bb, (tt + first_tile) * n_blk * n_e + e] // YS_ALIGN) * YS_ALIGN
        src = jnp.minimum(begin + p * wn, n_rows - wn)
        return begin, pl.multiple_of(src, YS_ALIGN)

    def copy(to_slot, src, e):
        return pltpu.make_async_copy(ys_ref.at[e, pl.ds(src, wn), :],
                                     buf_ref.at[to_slot, pl.ds(e * wn, wn), :], sem.at[to_slot, e])

    @pl.when(step == 0)
    def _():
        for e in range(n_e):
            copy(slot, window(b, t, e, 0)[1], e).start()

    @pl.when(step + 1 < n_steps)
    def _():
        for e in range(n_e):
            copy(1 - slot, window(b_next, t_next, e, 0)[1], e).start()

    arow = jnp.concatenate([arow_ref[0, i] for i in range(n_blk)], axis=1)
    erow = lax.broadcasted_iota(jnp.int32, (n_e, tm), 0)

    def per_expert(values):
        out = jnp.zeros((n_e, tm), F32)
        for e in range(n_e):
            out = jnp.where(erow == e, values[e].astype(F32), out)
        return out

    rel0 = arow - per_expert([window(b, t, e, 0)[0] for e in range(n_e)])

    def place(p, srcs, acc):
        lo = jnp.asarray(p * wn, F32)
        mine = (arow >= 0.0) & (rel0 >= lo) & (rel0 < lo + wn)
        rel = jnp.where(mine, arow - per_expert(srcs), 255.0).astype(BF16)
        spread = _tn_dot(rel, expand_ref[...])
        onehot = jnp.where(spread == lanemod_ref[...], 1.0, 0.0).astype(BF16)
        for e in range(n_e):
            copy(slot, srcs[e], e).wait()
        return acc + jnp.dot(onehot, buf_ref[slot], preferred_element_type=F32)

    acc = place(0, [window(b, t, e, 0)[1] for e in range(n_e)], jnp.zeros((tm, d), F32))

    def extra_pass(p, acc):
        srcs = [window(b, t, e, p)[1] for e in range(n_e)]
        for e in range(n_e):
            copy(slot, srcs[e], e).start()
        return place(p, srcs, acc)

    return lax.fori_loop(1, npass_ref[b, t + first_tile], extra_pass, acc)


def _inproj_kernel(has_moe, *refs):
    if has_moe:
        (start_ref, npass_ref, x_ref, arow_ref, ys_ref, expand_ref, lanemod_ref, gt2_ref, mod_ref, g_ref, w_ref,
         xo_ref, pf_ref, p_ref, buf_ref, sem) = refs
        moe = _combine_tile(0, start_ref, npass_ref, arow_ref, ys_ref, expand_ref, lanemod_ref, buf_ref, sem)
        x = x_ref[0] + gt2_ref[0][5:6] * moe
        xo_ref[0] = x
    else:
        x_ref, mod_ref, g_ref, w_ref, pf_ref, p_ref = refs
        x = x_ref[0]
    mod = mod_ref[0]
    h = (_rms(x) * g_ref[...]) * (1.0 + mod[1:2]) + mod[0:1]
    res = jnp.dot(h.astype(BF16), w_ref[0], preferred_element_type=F32)
    n_f = pf_ref.shape[2]
    pf_ref[0] = res[:, :n_f]
    p_ref[0] = res[:, n_f:].astype(BF16)


def _seg_spec(d, ctx_tiles):
    return pl.BlockSpec((1, 6, d), lambda b, t, *_: (2 * b + jnp.where(t < ctx_tiles, 0, 1), 0, 0))


def _combine_consts(n_e):
    wn = COMBINE_ROWS
    expand = np.zeros((n_e, n_e * wn), np.float32)
    for e in range(n_e):
        expand[e, e * wn:(e + 1) * wn] = 1.0
    lanemod = (np.arange(n_e * wn) % wn).astype(np.float32)[None, :]
    return jnp.asarray(expand, BF16), jnp.asarray(lanemod, F32)


def _combine_specs(comb, tm, first_tile):
    start, npass, arow, ys = comb
    n_e = arow.shape[2]
    expand, lanemod = _combine_consts(n_e)
    k = n_e * COMBINE_ROWS
    specs = [pl.BlockSpec((1, tm // LANES, n_e, LANES), lambda b, t, *_: (b, t + first_tile, 0, 0)),
             pl.BlockSpec(memory_space=pl.ANY),
             pl.BlockSpec((n_e, k), lambda b, t, *_: (0, 0)),
             pl.BlockSpec((1, k), lambda b, t, *_: (0, 0))]
    scratch = [pltpu.VMEM((2, k, ys.shape[2]), BF16), pltpu.SemaphoreType.DMA((2, n_e))]
    return specs, [arow, ys, expand, lanemod], scratch


def _inproj(x, comb, modtab_prev, modtab, g, w_bf16, layer, ctx_len, n_f32):
    b, l, d = x.shape
    n = w_bf16.shape[2]
    tm = ROW_TILE
    ctx_tiles = ctx_len // tm
    row = pl.BlockSpec((1, tm, d), lambda b, t, *_: (b, t, 0))
    has_moe = comb is not None
    in_specs = [row]
    args = [x]
    scratch = []
    prefetch = []
    if has_moe:
        specs, cargs, scratch = _combine_specs(comb, tm, 0)
        in_specs += specs + [_seg_spec(d, ctx_tiles)]
        args += cargs + [modtab_prev]
        prefetch = [comb[0], comb[1]]
    in_specs += [_seg_spec(d, ctx_tiles),
                 pl.BlockSpec((1, d), lambda b, t, *_: (0, 0)),
                 pl.BlockSpec((1, d, n), lambda b, t, *_: (layer, 0, 0))]
    args += [modtab, g.reshape(1, d), w_bf16]
    out_specs = [pl.BlockSpec((1, tm, n_f32), lambda b, t, *_: (b, t, 0)),
                 pl.BlockSpec((1, tm, n - n_f32), lambda b, t, *_: (b, t, 0))]
    out_shape = [jax.ShapeDtypeStruct((b, l, n_f32), F32), jax.ShapeDtypeStruct((b, l, n - n_f32), BF16)]
    if has_moe:
        out_specs = [row] + out_specs
        out_shape = [jax.ShapeDtypeStruct((b, l, d), F32)] + out_shape
    res = pl.pallas_call(
        functools.partial(_inproj_kernel, has_moe),
        grid_spec=pltpu.PrefetchScalarGridSpec(
            num_scalar_prefetch=len(prefetch), grid=(b, l // tm),
            in_specs=in_specs, out_specs=out_specs, scratch_shapes=scratch),
        out_shape=out_shape,
        compiler_params=_cparams(("arbitrary", "arbitrary")),
        name="inproj",
    )(*prefetch, *args)
    if has_moe:
        return res[0], res[1], res[2]
    return x, res[0], res[1]


def _bwd_chunk(j, ctx_chunks, n_chunks):
    return jnp.where(j < ctx_chunks, ctx_chunks - 1 - j, n_chunks - 1 + ctx_chunks - j)


def _decay_tables():
    c = SCAN_CHUNK
    w = np.zeros((W_ROWS, c), np.float32)
    masks = np.zeros((N_LEVELS + 1, c, c), np.float32)
    for lev in range(N_LEVELS):
        m = c >> (lev + 1)
        for t in range(c):
            g0 = (t // (2 * m)) * 2 * m
            bnd = g0 + m - 1
            if t > bnd:
                masks[lev, t, g0:bnd + 1] = 1.0
            if m in SMALL_HALVES:
                r0 = (1 + SMALL_HALVES.index(m)) * c
                if t > bnd:
                    w[r0 + t, bnd + 1:t + 1] = 1.0
                else:
                    w[r0 + t, t + 1:bnd + 1] = 1.0
    masks[N_LEVELS] = np.eye(c, dtype=np.float32)
    for t in range(c):
        w[t, :t + 1] = 1.0
    w_b = w.reshape(-1, c, c)[:, ::-1, ::-1].reshape(-1, c)
    masks_b = masks[:, ::-1, ::-1]
    return (jnp.asarray(np.stack([w, w_b]), BF16), jnp.asarray(np.stack([masks, masks_b]), F32))


def _hgrn_kernel(n_heads, qf_ref, ff_ref, if_ref, qb_ref, fb_ref, ib_ref, lb_ref, w_ref, mask_ref,
                 of_ref, ob_ref, st_ref, a_ref):
    c = SCAN_CHUNK

    @pl.when(pl.program_id(1) == 0)
    def _():
        st_ref[...] = jnp.zeros_like(st_ref)

    dirs = ((qf_ref, ff_ref, if_ref, of_ref), (qb_ref, fb_ref, ib_ref, ob_ref))
    for d, (q_ref, f_ref, i_ref, o_ref) in enumerate(dirs):
        for h in range(n_heads):
            sl = slice(h * HEAD_DIM, (h + 1) * HEAD_DIM)
            qb16 = q_ref[0, :, sl]
            q = qb16.astype(F32)
            xf = f_ref[0, :, sl]
            v = i_ref[0, :, sl]
            b_ = lb_ref[1:2, sl] + (jnp.minimum(xf, 0.0) - jnp.log(1.0 + jnp.exp(-jnp.abs(xf))))
            a_ = lb_ref[0:1, sl]
            lf2 = (jnp.maximum(a_, b_) + jnp.log(1.0 + jnp.exp(-jnp.abs(a_ - b_)))) * LOG2E
            f = jnp.exp2(lf2)
            k = 1.0 - f
            hi = lf2.astype(BF16)
            lo = (lf2 - hi.astype(F32)).astype(BF16)
            ex2 = jnp.dot(w_ref[d], jnp.concatenate([hi, lo], axis=1), preferred_element_type=F32)
            ex = ex2[:, :HEAD_DIM] + ex2[:, HEAD_DIM:]
            cum = ex[:c]
            a_ref[d, h] = cum
            kb16 = k.astype(BF16)
            scores = _nt_dot(qb16, kb16) * mask_ref[d, N_LEVELS]
            for lev in range(N_LEVELS):
                m = c >> (lev + 1)
                if m == 1:
                    lhs, rhs = (q * f).astype(BF16), kb16
                else:
                    if m in SMALL_HALVES:
                        r0 = (1 + SMALL_HALVES.index(m)) * c
                        e = jnp.exp2(ex[r0:r0 + c])
                    else:
                        parts = []
                        for g0 in range(0, c, 2 * m):
                            mid = g0 + m - 1 + d
                            parts.append(cum[g0:g0 + 2 * m] - a_ref[d, h, mid:mid + 1, :])
                        e = jnp.exp2(-jnp.abs(jnp.concatenate(parts, axis=0) if len(parts) > 1 else parts[0]))
                    lhs, rhs = (q * e).astype(BF16), (k * e).astype(BF16)
                scores = scores + _nt_dot(lhs, rhs) * mask_ref[d, lev]
            edge = (c - 1) * (1 - d)
            tot = a_ref[d, h, edge:edge + 1, :]
            st = st_ref[d, h]
            out = jnp.dot(scores.astype(BF16), v, preferred_element_type=F32)
            out = out + _nt_dot((q * jnp.exp2(cum)).astype(BF16), st.astype(BF16))
            o_ref[0, :, sl] = out.astype(BF16)
            kdec = (k * jnp.exp2(tot - cum)).astype(BF16)
            st_ref[d, h] = st * jnp.exp2(tot) + _tn_dot(v, kdec)


def _hgrn_scan(pf, p, lb_tab, w_tab, mask_tab, ctx_len, width):
    b, l, _ = p.shape
    c = SCAN_CHUNK
    n_chunks = l // c
    ctx_chunks = ctx_len // c
    n_heads = width // HEAD_DIM

    def fwd(g):
        return pl.BlockSpec((1, c, width), lambda b, j: (b, j, g))

    def bwd(g):
        return pl.BlockSpec((1, c, width), lambda b, j: (b, _bwd_chunk(j, ctx_chunks, n_chunks), g))

    out_shape = jax.ShapeDtypeStruct((b, l, width), BF16)
    return pl.pallas_call(
        functools.partial(_hgrn_kernel, n_heads),
        grid=(b, n_chunks),
        in_specs=[fwd(P_HQ), fwd(0), fwd(P_HI), bwd(P_HQ), bwd(1), bwd(P_HI),
                  pl.BlockSpec((2, width), lambda b, j: (0, 0)),
                  pl.BlockSpec(w_tab.shape, lambda b, j: (0, 0, 0)),
                  pl.BlockSpec(mask_tab.shape, lambda b, j: (0, 0, 0, 0))],
        out_specs=[fwd(0), bwd(0)],
        out_shape=[out_shape, out_shape],
        scratch_shapes=[pltpu.VMEM((2, n_heads, HEAD_DIM, HEAD_DIM), F32),
                        pltpu.VMEM((2, n_heads, c, HEAD_DIM), F32)],
        compiler_params=_cparams(("arbitrary", "arbitrary")),
        name="hgrn_scan",
    )(p, pf, p, p, pf, p, lb_tab, w_tab, mask_tab)


def _rope(z, cos_rep, sin_signed, even_lane):
    partner = jnp.where(even_lane, pltpu.roll(z, LANES - 1, 1), pltpu.roll(z, 1, 1))
    return z * cos_rep + partner * sin_signed


def _ret_kernel(n_heads, lg_ref, qf_ref, kf_ref, vf_ref, cf_ref, sf_ref,
                qb_ref, kb_ref, vb_ref, cb_ref, sb_ref, of_ref, ob_ref, st_ref, dm_ref, in_ref, tl_ref):
    c = SCAN_CHUNK
    k_scale = HEAD_DIM ** -0.5

    @pl.when(pl.program_id(1) == 0)
    def _():
        st_ref[...] = jnp.zeros_like(st_ref)
        ti = lax.broadcasted_iota(jnp.int32, (c, c), 0)
        si = lax.broadcasted_iota(jnp.int32, (c, c), 1)
        rowf = lax.broadcasted_iota(jnp.int32, (c, HEAD_DIM), 0).astype(F32)
        for d in range(2):
            rel = (ti - si) if d == 0 else (si - ti)
            relf = jnp.maximum(rel, 0).astype(F32)
            for h in range(n_heads):
                lg = lg_ref[d, h]
                dm_ref[d, h] = jnp.where(rel >= 0, jnp.exp(relf * lg), 0.0)
                if d == 0:
                    in_ref[d, h] = jnp.exp((rowf + 1.0) * lg)
                    tl_ref[d, h] = jnp.exp((c - 1.0 - rowf) * lg)
                else:
                    in_ref[d, h] = jnp.exp((c - rowf) * lg)
                    tl_ref[d, h] = jnp.exp(rowf * lg)

    even_lane = (lax.broadcasted_iota(jnp.int32, (c, HEAD_DIM), 1) & 1) == 0
    dirs = ((qf_ref, kf_ref, vf_ref, cf_ref, sf_ref, of_ref), (qb_ref, kb_ref, vb_ref, cb_ref, sb_ref, ob_ref))
    for d, (q_ref, k_ref, v_ref, c_ref, s_ref, o_ref) in enumerate(dirs):
        cos_rep = c_ref[...]
        sin_signed = s_ref[...]
        for h in range(n_heads):
            sl = slice(h * HEAD_DIM, (h + 1) * HEAD_DIM)
            q = _rope(q_ref[0, :, sl].astype(F32), cos_rep, sin_signed, even_lane)
            k = _rope(k_ref[0, :, sl].astype(F32), cos_rep, sin_signed, even_lane) * k_scale
            v = v_ref[0, :, sl]
            qb16 = q.astype(BF16)
            scores = _nt_dot(qb16, k.astype(BF16)) * dm_ref[d, h]
            st = st_ref[d, h]
            out = jnp.dot(scores.astype(BF16), v, preferred_element_type=F32)
            out = out + _nt_dot(qb16, st.astype(BF16)) * in_ref[d, h]
            o_ref[0, :, sl] = out.astype(BF16)
            edge = (c - 1) * (1 - d)
            st_ref[d, h] = (st * in_ref[d, h, edge:edge + 1, :]
                            + _tn_dot(v, (k * tl_ref[d, h]).astype(BF16)))


def _ret_scan(p, lg, cos_rep, sin_signed, ctx_len, width):
    b, l, _ = p.shape
    c = SCAN_CHUNK
    n_chunks = l // c
    ctx_chunks = ctx_len // c
    n_heads = width // HEAD_DIM

    def fwd(g):
        return pl.BlockSpec((1, c, width), lambda b, j, lg: (b, j, g))

    def bwd(g):
        return pl.BlockSpec((1, c, width), lambda b, j, lg: (b, _bwd_chunk(j, ctx_chunks, n_chunks), g))

    tab_f = pl.BlockSpec((c, HEAD_DIM), lambda b, j, lg: (j, 0))
    tab_b = pl.BlockSpec((c, HEAD_DIM), lambda b, j, lg: (_bwd_chunk(j, ctx_chunks, n_chunks), 0))
    out_shape = jax.ShapeDtypeStruct((b, l, width), BF16)
    return pl.pallas_call(
        functools.partial(_ret_kernel, n_heads),
        grid_spec=pltpu.PrefetchScalarGridSpec(
            num_scalar_prefetch=1,
            grid=(b, n_chunks),
            in_specs=[fwd(P_RQ), fwd(P_RK), fwd(P_RV), tab_f, tab_f,
                      bwd(P_RQ), bwd(P_RK), bwd(P_RV), tab_b, tab_b],
            out_specs=[fwd(0), bwd(0)],
            scratch_shapes=[pltpu.VMEM((2, n_heads, HEAD_DIM, HEAD_DIM), F32),
                            pltpu.VMEM((2, n_heads, c, c), F32),
                            pltpu.VMEM((2, n_heads, c, HEAD_DIM), F32),
                            pltpu.VMEM((2, n_heads, c, HEAD_DIM), F32)]),
        out_shape=[out_shape, out_shape],
        compiler_params=_cparams(("arbitrary", "arbitrary")),
        name="ret_scan",
    )(lg, p, p, p, cos_rep, sin_signed, p, p, p, cos_rep, sin_signed)


def _outproj_kernel(n_hg, n_ret, x_ref, hf_ref, hb_ref, rf_ref, rb_ref, hg_ref, rg_ref, mod_ref,
                    on_ref, wo_ref, g2_ref, wrh_ref, wrl_ref, xo_ref, h2_ref, aff_ref):
    mod = mod_ref[0]
    o_hg = hf_ref[0].astype(F32) + hb_ref[0].astype(F32)
    o_rt = rf_ref[0].astype(F32) + rb_ref[0].astype(F32)
    parts = []
    for h in range(n_hg):
        sl = slice(h * HEAD_DIM, (h + 1) * HEAD_DIM)
        parts.append((_rms(o_hg[:, sl]) * on_ref[...]) * _silu(hg_ref[0, :, sl].astype(F32)))
    for h in range(n_ret):
        sl = slice(h * HEAD_DIM, (h + 1) * HEAD_DIM)
        parts.append(_rms(o_rt[:, sl]) * _silu(rg_ref[0, :, sl].astype(F32)))
    mix = jnp.concatenate(parts, axis=1).astype(BF16)
    x = x_ref[0] + mod[2:3] * jnp.dot(mix, wo_ref[0], preferred_element_type=F32)
    xo_ref[0] = x
    h2 = (_rms(x) * g2_ref[...]) * (1.0 + mod[4:5]) + mod[3:4]
    h2_ref[0] = h2.astype(BF16)
    hh = h2.astype(BF16)
    hl = (h2 - hh.astype(F32)).astype(BF16)
    logits = _nt_dot(wrh_ref[...], hh) + _nt_dot(wrh_ref[...], hl) + _nt_dot(wrl_ref[...], hh)
    mx = jnp.max(logits, axis=0, keepdims=True)
    ex = jnp.exp(logits - mx)
    aff = ex / jnp.sum(ex, axis=0, keepdims=True)
    for i in range(aff_ref.shape[1]):
        aff_ref[0, i] = aff[:, i * LANES:(i + 1) * LANES]


def _outproj(x, hf, hb, rf, rb, p, modtab, onorm, wo_bf16, layer, g2, w_router, ctx_len, hg_width):
    b, l, d = x.shape
    tm = ROW_TILE
    ctx_tiles = ctx_len // tm
    n_hg = hg_width // HEAD_DIM
    ret_width = d - hg_width
    n_ret = ret_width // HEAD_DIM
    n_e = w_router.shape[1]
    wrt = w_router.T
    wrh = wrt.astype(BF16)
    wrl = (wrt - wrh.astype(F32)).astype(BF16)
    row = pl.BlockSpec((1, tm, d), lambda b, t: (b, t, 0))
    hrow = pl.BlockSpec((1, tm, hg_width), lambda b, t: (b, t, 0))
    rrow = pl.BlockSpec((1, tm, ret_width), lambda b, t: (b, t, 0))
    full = lambda shape: pl.BlockSpec(shape, lambda b, t: tuple(0 for _ in shape))
    return pl.pallas_call(
        functools.partial(_outproj_kernel, n_hg, n_ret),
        grid=(b, l // tm),
        in_specs=[row, hrow, hrow, rrow, rrow,
                  pl.BlockSpec((1, tm, hg_width), lambda b, t: (b, t, P_HGT)),
                  pl.BlockSpec((1, tm, ret_width), lambda b, t: (b, t, P_RGT)),
                  _seg_spec(d, ctx_tiles),
                  full((1, HEAD_DIM)), pl.BlockSpec((1, d, d), lambda b, t: (layer, 0, 0)),
                  full((1, d)), full((n_e, d)), full((n_e, d))],
        out_specs=[row, row, pl.BlockSpec((1, tm // LANES, n_e, LANES), lambda b, t: (b, t, 0, 0))],
        out_shape=[jax.ShapeDtypeStruct((b, l, d), F32),
                   jax.ShapeDtypeStruct((b, l, d), BF16),
                   jax.ShapeDtypeStruct((b, l // LANES, n_e, LANES), F32)],
        compiler_params=_cparams(("arbitrary", "arbitrary")),
        name="outproj",
    )(x, hf, hb, rf, rb, p, p, modtab, onorm.reshape(1, HEAD_DIM), wo_bf16, g2.reshape(1, d), wrh, wrl)


FFN_SPLIT = 2
CAST_ROWS = 64


def _ffn_kernel(layer, x_ref, gate_ref, wg_hbm, wu_hbm, wd_hbm, o_ref, sg_ref, su_ref, sd_ref,
                wg_ref, wu_ref, wd_ref, sem):
    e, t, n_e = pl.program_id(0), pl.program_id(1), pl.num_programs(0)
    pairs = ((wg_hbm, sg_ref, wg_ref), (wu_hbm, su_ref, wu_ref), (wd_hbm, sd_ref, wd_ref))

    def fetch(expert):
        return [pltpu.make_async_copy(hbm.at[layer, expert], stage, sem.at[i])
                for i, (hbm, stage, _) in enumerate(pairs)]

    @pl.when((e == 0) & (t == 0))
    def _():
        for cp in fetch(0):
            cp.start()

    @pl.when(t == 0)
    def _():
        for cp in fetch(e):
            cp.wait()
        for _, stage, w16 in pairs:
            def cast(i, carry, stage=stage, w16=w16):
                rows = pl.ds(pl.multiple_of(i * CAST_ROWS, CAST_ROWS), CAST_ROWS)
                w16[rows, :] = stage[rows, :].astype(BF16)
                return carry
            lax.fori_loop(0, stage.shape[0] // CAST_ROWS, cast, 0)

        @pl.when(e + 1 < n_e)
        def _():
            for cp in fetch(e + 1):
                cp.start()

    x = x_ref[0]
    ff = wg_ref.shape[1]
    piece = ff // FFN_SPLIT
    y = jnp.zeros((x.shape[0], wd_ref.shape[1]), F32)
    for i in range(FFN_SPLIT):
        cs = slice(i * piece, (i + 1) * piece)
        a = jnp.dot(x, wg_ref[:, cs], preferred_element_type=F32)
        u = jnp.dot(x, wu_ref[:, cs], preferred_element_type=F32)
        hid = (_silu(a) * u).astype(BF16)
        y = y + jnp.dot(hid, wd_ref[cs, :], preferred_element_type=F32)
    o_ref[0] = (y * gate_ref[0]).astype(BF16)


def _row_tile(rows, cap=512):
    best = 16
    for t in range(16, cap + 1, 16):
        if rows % t == 0:
            best = t
    return best


def _expert_ffn(xs, gates, wg, wu, wd, layer):
    n_e, rows, d = xs.shape
    ff = wg.shape[3]
    tm = _row_tile(rows)
    assert ff % FFN_SPLIT == 0 and d % CAST_ROWS == 0 and ff % CAST_ROWS == 0
    hbm = pl.BlockSpec(memory_space=pl.ANY)
    return pl.pallas_call(
        functools.partial(_ffn_kernel, layer),
        grid=(n_e, rows // tm),
        in_specs=[pl.BlockSpec((1, tm, d), lambda e, t: (e, t, 0)),
                  pl.BlockSpec((1, tm, 1), lambda e, t: (e, t, 0)),
                  hbm, hbm, hbm],
        out_specs=pl.BlockSpec((1, tm, d), lambda e, t: (e, t, 0)),
        out_shape=jax.ShapeDtypeStruct((n_e, rows, d), BF16),
        scratch_shapes=[pltpu.VMEM((d, ff), F32), pltpu.VMEM((d, ff), F32), pltpu.VMEM((ff, d), F32),
                        pltpu.VMEM((d, ff), BF16), pltpu.VMEM((d, ff), BF16), pltpu.VMEM((ff, d), BF16),
                        pltpu.SemaphoreType.DMA((3,))],
        compiler_params=_cparams(("arbitrary", "arbitrary")),
        name="expert_ffn",
    )(xs, gates, wg, wu, wd)


def _final_kernel(first_tile, start_ref, npass_ref, x_ref, arow_ref, ys_ref, expand_ref, lanemod_ref,
                  gt2_ref, g_ref, o_ref, buf_ref, sem):
    moe = _combine_tile(first_tile, start_ref, npass_ref, arow_ref, ys_ref, expand_ref, lanemod_ref, buf_ref, sem)
    x = x_ref[0] + gt2_ref[0][5:6] * moe
    o_ref[0] = _rms(x) * g_ref[...]


def _final(x, comb, modtab, g, ctx_len):
    b, l, d = x.shape
    tm = ROW_TILE
    ctx_tiles = ctx_len // tm
    n_t = (l - ctx_len) // tm
    row = pl.BlockSpec((1, tm, d), lambda b, t, *_: (b, t + ctx_tiles, 0))
    specs, cargs, scratch = _combine_specs(comb, tm, ctx_tiles)
    return pl.pallas_call(
        functools.partial(_final_kernel, ctx_tiles),
        grid_spec=pltpu.PrefetchScalarGridSpec(
            num_scalar_prefetch=2, grid=(b, n_t),
            in_specs=[row] + specs + [pl.BlockSpec((1, 6, d), lambda b, t, *_: (2 * b + 1, 0, 0)),
                                      pl.BlockSpec((1, d), lambda b, t, *_: (0, 0))],
            out_specs=pl.BlockSpec((1, tm, d), lambda b, t, *_: (b, t, 0)),
            scratch_shapes=scratch),
        out_shape=jax.ShapeDtypeStruct((b, l - ctx_len, d), F32),
        compiler_params=_cparams(("arbitrary", "arbitrary")),
        name="final_norm",
    )(comb[0], comb[1], x, *cargs, modtab, g.reshape(1, d))


def _rope_tables(rows, ctx_len):
    n_freq = HEAD_DIM // 4
    inv = ROPE_BASE ** (-jnp.arange(n_freq, dtype=F32) / n_freq)
    r = jnp.repeat(jnp.arange(rows, dtype=F32), GRID_W)
    cc = jnp.tile(jnp.arange(GRID_W, dtype=F32), rows)
    lat = jnp.concatenate([r[:, None] * inv, cc[:, None] * inv], axis=-1)
    ang = jnp.concatenate([jnp.zeros((ctx_len, 2 * n_freq), F32), lat], axis=0)
    cos = jnp.cos(ang)
    sin = jnp.sin(ang)
    cos_rep = jnp.repeat(cos, 2, axis=-1)
    sin_signed = jnp.stack([-sin, sin], axis=-1).reshape(ang.shape[0], HEAD_DIM)
    return cos_rep, sin_signed


def _select_kernel(nb, cap, aff_ref, tri_ref, pos_ref, off_ref, tot_ref, cum_ref):
    n_e = aff_ref.shape[2]
    bits = lax.bitcast_convert_type(aff_ref[0], jnp.int32)
    ones = jnp.ones((LANES, LANES), BF16)
    tri = tri_ref[...]

    def total(flags):
        return jnp.dot(jnp.sum(flags, axis=0).astype(BF16), ones, preferred_element_type=F32)

    def thr_step(i, thr):
        cand = thr | (jnp.int32(1) << (30 - i))
        cnt = total((bits >= cand[None]).astype(F32))
        return jnp.where(cnt >= cap, cand, thr)

    thr = lax.fori_loop(0, 31, thr_step, jnp.zeros((n_e, LANES), jnp.int32))

    def prefix(flags):
        f2 = flags.reshape(nb * n_e, LANES).astype(BF16)
        incl = jnp.dot(f2, tri, preferred_element_type=F32).reshape(nb, n_e, LANES)
        tot_ref[...] = jnp.dot(f2, ones, preferred_element_type=F32).reshape(nb, n_e, LANES)

        def step(k, carry):
            cum_ref[k] = carry
            return carry + tot_ref[k]

        lax.fori_loop(0, nb, step, jnp.zeros((n_e, LANES), F32))
        return incl, cum_ref[...]

    gt = (bits > thr[None]).astype(F32)
    eq = (bits == thr[None]).astype(F32)
    need = cap - total(gt)
    eq_incl, eq_off = prefix(eq)
    sel = gt + eq * ((eq_off + eq_incl - eq) < need[None]).astype(F32)
    incl, off = prefix(sel)
    pos_ref[0] = jnp.where(sel > 0.5, off + incl - 1.0, -1.0)
    off_ref[0] = off


def _compact_kernel(nb, off_ref, pos_ref, aff_ref, out_ref):
    b = pl.program_id(0)
    n_e = pos_ref.shape[2]
    out_ref[...] = jnp.zeros_like(out_ref)
    slot_row = lax.broadcasted_iota(jnp.int32, (LANES, LANES), 0).astype(F32)
    lane8 = lax.broadcasted_iota(jnp.int32, (8, LANES), 1)
    row8 = lax.broadcasted_iota(jnp.int32, (8, LANES), 0)
    lane_f = lane8.astype(F32)

    def block(k, carry):
        for e in range(n_e):
            start = off_ref[b, k, e]
            blk = start >> 7
            p = pos_ref[0, k, e:e + 1, :]
            g = aff_ref[0, k, e:e + 1, :]
            r = p - jnp.asarray(blk * LANES, F32)
            r = jnp.where(r >= LANES, r - LANES, r)
            onehot = jnp.where((slot_row == r) & (p >= 0.0), 1.0, 0.0).astype(BF16)
            g_hi = g.astype(BF16).astype(F32)
            g_mid = (g - g_hi).astype(BF16).astype(F32)
            g_lo = (g - g_hi - g_mid).astype(BF16).astype(F32)
            vals = jnp.where(row8 == 0, lane_f,
                             jnp.where(row8 == 1, 1.0,
                                       jnp.where(row8 == 2, g_hi,
                                                 jnp.where(row8 == 3, g_mid,
                                                           jnp.where(row8 == 4, g_lo, 0.0)))))
            c = _nt_dot(vals.astype(BF16), onehot)
            tok = c[1:2] * jnp.asarray(k * LANES, F32) + c[0:1]
            gate = c[2:3] + c[3:4] + c[4:5]
            tile = jnp.where(row8 == 0, tok, jnp.where(row8 == 1, gate, 0.0))
            first = lane8 >= (start & (LANES - 1))
            out_ref[0, e, blk] += jnp.where(first, tile, 0.0)
            out_ref[0, e, blk + 1] += jnp.where(first, 0.0, tile)
        return carry

    lax.fori_loop(0, nb, block, 0)


def _route(aff_blocks, cap):
    bsz, nb, n_e, _ = aff_blocks.shape
    assert nb <= 256
    tri = jnp.asarray(np.triu(np.ones((LANES, LANES), np.float32)), BF16)
    blk = pl.BlockSpec((1, nb, n_e, LANES), lambda b: (b, 0, 0, 0))
    shp = jax.ShapeDtypeStruct((bsz, nb, n_e, LANES), F32)
    pos, off_rep = pl.pallas_call(
        functools.partial(_select_kernel, nb, cap),
        grid=(bsz,),
        in_specs=[blk, pl.BlockSpec((LANES, LANES), lambda b: (0, 0))],
        out_specs=[blk, blk],
        out_shape=[shp, shp],
        scratch_shapes=[pltpu.VMEM((nb, n_e, LANES), F32), pltpu.VMEM((nb, n_e, LANES), F32)],
        compiler_params=_cparams(("arbitrary",)),
        name="route_select",
    )(aff_blocks, tri)
    off = off_rep[:, :, :, 0].astype(jnp.int32)
    n_rows = cap // LANES + 2
    blk1 = pl.BlockSpec((1, nb, n_e, LANES), lambda b, off: (b, 0, 0, 0))
    lists = pl.pallas_call(
        functools.partial(_compact_kernel, nb),
        grid_spec=pltpu.PrefetchScalarGridSpec(
            num_scalar_prefetch=1,
            grid=(bsz,),
            in_specs=[blk1, blk1],
            out_specs=pl.BlockSpec((1, n_e, n_rows, 8, LANES), lambda b, off: (b, 0, 0, 0, 0))),
        out_shape=jax.ShapeDtypeStruct((bsz, n_e, n_rows, 8, LANES), F32),
        compiler_params=_cparams(("arbitrary",)),
        name="route_compact",
    )(off, pos, aff_blocks)
    tokens = lists[:, :, :, 0, :].reshape(bsz, n_e, n_rows * LANES)[:, :, :cap].astype(jnp.int32)
    gates = lists[:, :, :, 1, :].reshape(bsz, n_e, n_rows * LANES)[:, :, :cap]
    return gates, tokens, pos, off


def kernel(x, c, ctx, c_ctx, w_ada, b_ada, g_mix, g_ffn, w_in, w_out, hg_lb_logits, hg_onorm,
           ret_decay, w_router, w_e_gate, w_e_up, w_e_down, g_final):
    bsz, n, d = x.shape
    lc = ctx.shape[1]
    depth = w_ada.shape[0]
    l = lc + n
    hg_width = hg_lb_logits.shape[1]
    ret_width = d - hg_width
    n_e = w_router.shape[2]
    assert lc % ROW_TILE == 0 and n % ROW_TILE == 0 and lc % SCAN_CHUNK == 0 and n % SCAN_CHUNK == 0
    assert hg_width % HEAD_DIM == 0 and ret_width % HEAD_DIM == 0 and n % GRID_W == 0
    assert w_in.shape[2] == 5 * hg_width + 4 * ret_width and hg_width == ret_width

    cos_rep, sin_signed = _rope_tables(n // GRID_W, lc)
    w_tab, mask_tab = _decay_tables()

    gamma_cum = jnp.cumsum(jax.nn.softmax(hg_lb_logits.astype(F32), axis=0), axis=0)
    lbs = gamma_cum - gamma_cum[0:1]
    lb_tabs = jnp.stack([jnp.log(lbs), jnp.log1p(-lbs)], axis=1)
    log_gammas = -jnp.exp(ret_decay.astype(F32))

    cond8 = jnp.zeros((8, d), F32).at[:bsz].set(c).at[bsz].set(c_ctx)
    mods = _adaln(cond8, w_ada, b_ada).reshape(depth, 8, 6, d)
    modtabs = jnp.stack([jnp.broadcast_to(mods[:, bsz:bsz + 1], (depth, bsz, 6, d)), mods[:, :bsz]],
                        axis=2).reshape(depth, 2 * bsz, 6, d)

    cap_lat = CAPACITY_FACTOR * n // n_e
    cap_ctx = CAPACITY_FACTOR * lc // n_e
    boff = (jnp.arange(bsz, dtype=jnp.int32) * l)[:, None, None]

    gw = hg_width
    w_in16 = jnp.concatenate([w_in[:, :, gw:3 * gw], w_in[:, :, :gw], w_in[:, :, 3 * gw:]], axis=2).astype(BF16)
    w_out16 = w_out.astype(BF16)
    xs = jnp.concatenate([ctx, x], axis=1)
    comb = None
    for layer in range(depth):
        last = layer == depth - 1
        xs, pf, p = _inproj(xs, comb, modtabs[layer - 1] if layer else None, modtabs[layer], g_mix[layer],
                            w_in16, layer, lc, 2 * hg_width)
        hf, hb = _hgrn_scan(pf, p, lb_tabs[layer], w_tab, mask_tab, lc, hg_width)
        rf, rb = _ret_scan(p, log_gammas[layer], cos_rep, sin_signed, lc, ret_width)
        xs, h2, aff_t = _outproj(xs, hf, hb, rf, rb, p, modtabs[layer], hg_onorm[layer],
                                 w_out16, layer, g_ffn[layer], w_router[layer], lc, hg_width)
        blocks = [lc // LANES, n // LANES]
        caps = [cap_ctx, cap_lat]
        bases = [bsz * cap_lat + jnp.arange(bsz, dtype=jnp.int32) * cap_ctx,
                 jnp.arange(bsz, dtype=jnp.int32) * cap_lat]
        affs = [aff_t[:, :blocks[0]], aff_t[:, blocks[0]:]]
        tok_off = [0, lc]
        rows, gts, arows, starts, cnts = {}, {}, [], [], []
        for seg in (0, 1):
            if seg == 0 and last:
                arows.append(jnp.full((bsz, blocks[0], n_e, LANES), -1.0, F32))
                starts.append(jnp.zeros((bsz, blocks[0], n_e), jnp.int32))
                cnts.append(jnp.zeros((bsz, blocks[0], n_e), jnp.int32))
                continue
            g_s, i_s, pos_s, off_s = _route(affs[seg], caps[seg])
            rows[seg] = i_s + tok_off[seg] + boff
            gts[seg] = g_s
            kept = pos_s >= 0.0
            arows.append(jnp.where(kept, pos_s + bases[seg][:, None, None, None].astype(F32), -1.0))
            starts.append(off_s + bases[seg][:, None, None])
            cnts.append(jnp.sum(kept, axis=-1).astype(jnp.int32))
        order = [s_ for s_ in (1, 0) if s_ in rows]
        flat = jnp.concatenate([rows[s_].transpose(1, 0, 2).reshape(n_e, -1) for s_ in order], axis=1)
        gate = jnp.concatenate([gts[s_].transpose(1, 0, 2).reshape(n_e, -1) for s_ in order], axis=1)
        n_rows = flat.shape[1]
        pad = (-n_rows) % 16
        if pad:
            flat = jnp.pad(flat, ((0, 0), (0, pad)))
            gate = jnp.pad(gate, ((0, 0), (0, pad)))
        h2f = h2.reshape(bsz * l, d)
        gathered = h2f[flat.reshape(-1)].reshape(n_e, n_rows + pad, d)
        ys = _expert_ffn(gathered, gate[:, :, None], w_e_gate, w_e_up, w_e_down, layer)
        arow = jnp.concatenate(arows, axis=1)
        start = jnp.concatenate(starts, axis=1)
        nbt = ROW_TILE // LANES
        start_t = start[:, ::nbt]
        cnt_t = jnp.concatenate(cnts, axis=1).reshape(bsz, -1, nbt, n_e).sum(axis=2)
        need = start_t % YS_ALIGN + cnt_t
        npass = jnp.max(jnp.where(cnt_t > 0, -(-need // COMBINE_ROWS), 0), axis=-1).astype(jnp.int32)
        comb = (start.reshape(bsz, -1), npass, arow, ys)
    return _final(xs, comb, modtabs[depth - 1], g_final, lc)
```

```python
import functools

import numpy as np
import jax
import jax.numpy as jnp
from jax import lax
from jax.experimental import pallas as pl
from jax.experimental.pallas import tpu as pltpu

F32 = jnp.float32
BF16 = jnp.bfloat16

EPS = 1e-6
GRID_W = 64
ROPE_BASE = 10000.0
N_EXPERTS = 16
CAPACITY_FACTOR = 2
HEAD_DIM = 128
LANES = 128
SCAN_CHUNK = 128
ROW_TILE = 256
VMEM_LIMIT = 56 * 1024 * 1024

P_HQ, P_HI, P_HGT, P_RQ, P_RK, P_RV, P_RGT = range(7)

N_LEVELS = int(np.log2(SCAN_CHUNK))
SMALL_HALVES = (4, 2)
W_ROWS = (1 + len(SMALL_HALVES)) * SCAN_CHUNK
LOG2E = float(np.log2(np.e))
SCAN_INTERLEAVE = 4


def _cparams(sem, **kw):
    return pltpu.CompilerParams(dimension_semantics=sem, vmem_limit_bytes=VMEM_LIMIT, **kw)


def _nt_dot(a, b):
    return lax.dot_general(a, b, (((1,), (1,)), ((), ())), preferred_element_type=F32)


def _tn_dot(a, b):
    return lax.dot_general(a, b, (((0,), (0,)), ((), ())), preferred_element_type=F32)


def _sigmoid(x):
    return 1.0 / (1.0 + jnp.exp(-x))


def _silu(x):
    return x * _sigmoid(x)


def _adaln_kernel(cond_ref, w_ref, b_ref, o_ref):
    s = _silu(cond_ref[...])
    o_ref[0] = jnp.dot(s.astype(BF16), w_ref[0].astype(BF16), preferred_element_type=F32) + b_ref[0]


def _adaln(cond8, w_ada, b_ada):
    depth, d, n6 = w_ada.shape
    tn = n6 // 4
    return pl.pallas_call(
        _adaln_kernel,
        grid=(depth, n6 // tn),
        in_specs=[pl.BlockSpec((8, d), lambda l, j: (0, 0)),
                  pl.BlockSpec((1, d, tn), lambda l, j: (l, 0, j)),
                  pl.BlockSpec((1, 1, tn), lambda l, j: (l, 0, j))],
        out_specs=pl.BlockSpec((1, 8, tn), lambda l, j: (l, 0, j)),
        out_shape=jax.ShapeDtypeStruct((depth, 8, n6), F32),
        compiler_params=_cparams(("arbitrary", "arbitrary")),
        name="adaln",
    )(cond8, w_ada, b_ada.reshape(depth, 1, n6))


def _rms(x):
    return x * lax.rsqrt(jnp.mean(x * x, axis=-1, keepdims=True) + EPS)


COMBINE_ROWS = 64
YS_ALIGN = 16


def _combine_tile(first_tile, start_ref, npass_ref, arow_ref, ys_ref, expand_ref, lanemod_ref, buf_ref, sem):
    n_e = arow_ref.shape[2]
    n_blk = arow_ref.shape[1]
    tm = n_blk * LANES
    wn = COMBINE_ROWS
    n_rows = ys_ref.shape[1]
    d = ys_ref.shape[2]
    b, t, n_t = pl.program_id(0), pl.program_id(1), pl.num_programs(1)
    step = b * n_t + t
    n_steps = pl.num_programs(0) * n_t
    slot = step % 2
    wrap = t + 1 == n_t
    b_next = jnp.where(wrap, b + 1, b)
    t_next = jnp.where(wrap, 0, t + 1)

    def window(bb, tt, e, p):
        begin = (start_ref[bb, (tt + first_tile) * n_blk * n_e + e] // YS_ALIGN) * YS_ALIGN
        src = jnp.minimum(begin + p * wn, n_rows - wn)
        return begin, pl.multiple_of(src, YS_ALIGN)

    def copy(to_slot, src, e):
        return pltpu.make_async_copy(ys_ref.at[e, pl.ds(src, wn), :],
                                     buf_ref.at[to_slot, pl.ds(e * wn, wn), :], sem.at[to_slot, e])

    @pl.when(step == 0)
    def _():
        for e in range(n_e):
            copy(slot, window(b, t, e, 0)[1], e).start()

    @pl.when(step + 1 < n_steps)
    def _():
        for e in range(n_e):
            copy(1 - slot, window(b_next, t_next, e, 0)[1], e).start()

    arow = jnp.concatenate([arow_ref[0, i] for i in range(n_blk)], axis=1)
    erow = lax.broadcasted_iota(jnp.int32, (n_e, tm), 0)

    def per_expert(values):
        out = jnp.zeros((n_e, tm), F32)
        for e in range(n_e):
            out = jnp.where(erow == e, values[e].astype(F32), out)
        return out

    rel0 = arow - per_expert([window(b, t, e, 0)[0] for e in range(n_e)])

    def place(p, srcs, acc):
        lo = jnp.asarray(p * wn, F32)
        mine = (arow >= 0.0) & (rel0 >= lo) & (rel0 < lo + wn)
        rel = jnp.where(mine, arow - per_expert(srcs), 255.0).astype(BF16)
        spread = _tn_dot(rel, expand_ref[...])
        onehot = jnp.where(spread == lanemod_ref[...], 1.0, 0.0).astype(BF16)
        for e in range(n_e):
            copy(slot, srcs[e], e).wait()
        return acc + jnp.dot(onehot, buf_ref[slot], preferred_element_type=F32)

    acc = place(0, [window(b, t, e, 0)[1] for e in range(n_e)], jnp.zeros((tm, d), F32))

    def extra_pass(p, acc):
        srcs = [window(b, t, e, p)[1] for e in range(n_e)]
        for e in range(n_e):
            copy(slot, srcs[e], e).start()
        return place(p, srcs, acc)

    return lax.fori_loop(1, npass_ref[b, t + first_tile], extra_pass, acc)


def _inproj_kernel(has_moe, *refs):
    if has_moe:
        (start_ref, npass_ref, x_ref, arow_ref, ys_ref, expand_ref, lanemod_ref, gt2_ref, mod_ref, g_ref, w_ref,
         xo_ref, pf_ref, p_ref, buf_ref, sem) = refs
        moe = _combine_tile(0, start_ref, npass_ref, arow_ref, ys_ref, expand_ref, lanemod_ref, buf_ref, sem)
        x = x_ref[0] + gt2_ref[0][5:6] * moe
        xo_ref[0] = x
    else:
        x_ref, mod_ref, g_ref, w_ref, pf_ref, p_ref = refs
        x = x_ref[0]
    mod = mod_ref[0]
    h = (_rms(x) * g_ref[...]) * (1.0 + mod[1:2]) + mod[0:1]
    res = jnp.dot(h.astype(BF16), w_ref[0], preferred_element_type=F32)
    n_f = pf_ref.shape[2]
    pf_ref[0] = res[:, :n_f]
    p_ref[0] = res[:, n_f:].astype(BF16)


def _seg_spec(d, ctx_tiles):
    return pl.BlockSpec((1, 6, d), lambda b, t, *_: (2 * b + jnp.where(t < ctx_tiles, 0, 1), 0, 0))


def _combine_consts(n_e):
    wn = COMBINE_ROWS
    expand = np.zeros((n_e, n_e * wn), np.float32)
    for e in range(n_e):
        expand[e, e * wn:(e + 1) * wn] = 1.0
    lanemod = (np.arange(n_e * wn) % wn).astype(np.float32)[None, :]
    return jnp.asarray(expand, BF16), jnp.asarray(lanemod, F32)


def _combine_specs(comb, tm, first_tile):
    start, npass, arow, ys = comb
    n_e = arow.shape[2]
    expand, lanemod = _combine_consts(n_e)
    k = n_e * COMBINE_ROWS
    specs = [pl.BlockSpec((1, tm // LANES, n_e, LANES), lambda b, t, *_: (b, t + first_tile, 0, 0)),
             pl.BlockSpec(memory_space=pl.ANY),
             pl.BlockSpec((n_e, k), lambda b, t, *_: (0, 0)),
             pl.BlockSpec((1, k), lambda b, t, *_: (0, 0))]
    scratch = [pltpu.VMEM((2, k, ys.shape[2]), BF16), pltpu.SemaphoreType.DMA((2, n_e))]
    return specs, [arow, ys, expand, lanemod], scratch


def _inproj(x, comb, modtab_prev, modtab, g, w_bf16, layer, ctx_len, n_f32):
    b, l, d = x.shape
    n = w_bf16.shape[2]
    tm = ROW_TILE
    ctx_tiles = ctx_len // tm
    row = pl.BlockSpec((1, tm, d), lambda b, t, *_: (b, t, 0))
    has_moe = comb is not None
    in_specs = [row]
    args = [x]
    scratch = []
    prefetch = []
    if has_moe:
        specs, cargs, scratch = _combine_specs(comb, tm, 0)
        in_specs += specs + [_seg_spec(d, ctx_tiles)]
        args += cargs + [modtab_prev]
        prefetch = [comb[0], comb[1]]
    in_specs += [_seg_spec(d, ctx_tiles),
                 pl.BlockSpec((1, d), lambda b, t, *_: (0, 0)),
                 pl.BlockSpec((1, d, n), lambda b, t, *_: (layer, 0, 0))]
    args += [modtab, g.reshape(1, d), w_bf16]
    out_specs = [pl.BlockSpec((1, tm, n_f32), lambda b, t, *_: (b, t, 0)),
                 pl.BlockSpec((1, tm, n - n_f32), lambda b, t, *_: (b, t, 0))]
    out_shape = [jax.ShapeDtypeStruct((b, l, n_f32), F32), jax.ShapeDtypeStruct((b, l, n - n_f32), BF16)]
    if has_moe:
        out_specs = [row] + out_specs
        out_shape = [jax.ShapeDtypeStruct((b, l, d), F32)] + out_shape
    res = pl.pallas_call(
        functools.partial(_inproj_kernel, has_moe),
        grid_spec=pltpu.PrefetchScalarGridSpec(
            num_scalar_prefetch=len(prefetch), grid=(b, l // tm),
            in_specs=in_specs, out_specs=out_specs, scratch_shapes=scratch),
        out_shape=out_shape,
        compiler_params=_cparams(("arbitrary", "arbitrary")),
        name="inproj",
    )(*prefetch, *args)
    if has_moe:
        return res[0], res[1], res[2]
    return x, res[0], res[1]


def _bwd_chunk(j, ctx_chunks, n_chunks):
    return jnp.where(j < ctx_chunks, ctx_chunks - 1 - j, n_chunks - 1 + ctx_chunks - j)


def _decay_tables():
    c = SCAN_CHUNK
    w = np.zeros((W_ROWS, c), np.float32)
    masks = np.zeros((N_LEVELS + 1, c, c), np.float32)
    for lev in range(N_LEVELS):
        m = c >> (lev + 1)
        for t in range(c):
            g0 = (t // (2 * m)) * 2 * m
            bnd = g0 + m - 1
            if t > bnd:
                masks[lev, t, g0:bnd + 1] = 1.0
            if m in SMALL_HALVES:
                r0 = (1 + SMALL_HALVES.index(m)) * c
                if t > bnd:
                    w[r0 + t, bnd + 1:t + 1] = 1.0
                else:
                    w[r0 + t, t + 1:bnd + 1] = 1.0
    masks[N_LEVELS] = np.eye(c, dtype=np.float32)
    for t in range(c):
        w[t, :t + 1] = 1.0
    w_b = w.reshape(-1, c, c)[:, ::-1, ::-1].reshape(-1, c)
    masks_b = masks[:, ::-1, ::-1]
    return (jnp.asarray(np.stack([w, w_b]), BF16), jnp.asarray(np.stack([masks, masks_b]), F32))


def _hgrn_kernel(n_heads, qf_ref, ff_ref, if_ref, qb_ref, fb_ref, ib_ref, lb_ref, w_ref, mask_ref,
                 of_ref, ob_ref, st_ref, a_ref):
    c = SCAN_CHUNK

    @pl.when(pl.program_id(1) == 0)
    def _():
        st_ref[...] = jnp.zeros_like(st_ref)

    dirs = ((qf_ref, ff_ref, if_ref, of_ref), (qb_ref, fb_ref, ib_ref, ob_ref))

    def gates(d, h):
        q_ref, f_ref, i_ref, _ = dirs[d]
        sl = slice(h * HEAD_DIM, (h + 1) * HEAD_DIM)
        qb16 = q_ref[0, :, sl]
        xf = f_ref[0, :, sl]
        b_ = lb_ref[1:2, sl] + (jnp.minimum(xf, 0.0) - jnp.log(1.0 + jnp.exp(-jnp.abs(xf))))
        a_ = lb_ref[0:1, sl]
        lf2 = (jnp.maximum(a_, b_) + jnp.log(1.0 + jnp.exp(-jnp.abs(a_ - b_)))) * LOG2E
        f = jnp.exp2(lf2)
        k = 1.0 - f
        hi = lf2.astype(BF16)
        lo = (lf2 - hi.astype(F32)).astype(BF16)
        ex2 = jnp.dot(w_ref[d], jnp.concatenate([hi, lo], axis=1), preferred_element_type=F32)
        ex = ex2[:, :HEAD_DIM] + ex2[:, HEAD_DIM:]
        cum = ex[:c]
        a_ref[d, h] = cum
        return dict(d=d, h=h, sl=sl, qb16=qb16, q=qb16.astype(F32), v=i_ref[0, :, sl], f=f, k=k, ex=ex, cum=cum)

    def diagonal(s):
        s["kb16"] = s["k"].astype(BF16)
        s["scores"] = _nt_dot(s["qb16"], s["kb16"]) * mask_ref[s["d"], N_LEVELS]

    def level(s, lev):
        d, h, q, k, ex, cum = s["d"], s["h"], s["q"], s["k"], s["ex"], s["cum"]
        m = c >> (lev + 1)
        if m == 1:
            lhs, rhs = (q * s["f"]).astype(BF16), s["kb16"]
        else:
            if m in SMALL_HALVES:
                r0 = (1 + SMALL_HALVES.index(m)) * c
                e = jnp.exp2(ex[r0:r0 + c])
            else:
                parts = []
                for g0 in range(0, c, 2 * m):
                    mid = g0 + m - 1 + d
                    parts.append(cum[g0:g0 + 2 * m] - a_ref[d, h, mid:mid + 1, :])
                e = jnp.exp2(-jnp.abs(jnp.concatenate(parts, axis=0) if len(parts) > 1 else parts[0]))
            lhs, rhs = (q * e).astype(BF16), (k * e).astype(BF16)
        s["scores"] = s["scores"] + _nt_dot(lhs, rhs) * mask_ref[d, lev]

    def carry(s):
        d, h, q, k, v, cum = s["d"], s["h"], s["q"], s["k"], s["v"], s["cum"]
        edge = (c - 1) * (1 - d)
        tot = a_ref[d, h, edge:edge + 1, :]
        st = st_ref[d, h]
        out = jnp.dot(s["scores"].astype(BF16), v, preferred_element_type=F32)
        out = out + _nt_dot((q * jnp.exp2(cum)).astype(BF16), st.astype(BF16))
        dirs[d][3][0, :, s["sl"]] = out.astype(BF16)
        kdec = (k * jnp.exp2(tot - cum)).astype(BF16)
        st_ref[d, h] = st * jnp.exp2(tot) + _tn_dot(v, kdec)

    units = [(d, h) for d in range(2) for h in range(n_heads)]
    for g in range(0, len(units), SCAN_INTERLEAVE):
        states = [gates(d, h) for d, h in units[g:g + SCAN_INTERLEAVE]]
        for s_ in states:
            diagonal(s_)
            for lev in range(N_LEVELS):
                level(s_, lev)
        for s_ in states:
            carry(s_)


def _hgrn_scan(pf, p, lb_tab, w_tab, mask_tab, ctx_len, width):
    b, l, _ = p.shape
    c = SCAN_CHUNK
    n_chunks = l // c
    ctx_chunks = ctx_len // c
    n_heads = width // HEAD_DIM

    def fwd(g):
        return pl.BlockSpec((1, c, width), lambda b, j: (b, j, g))

    def bwd(g):
        return pl.BlockSpec((1, c, width), lambda b, j: (b, _bwd_chunk(j, ctx_chunks, n_chunks), g))

    out_shape = jax.ShapeDtypeStruct((b, l, width), BF16)
    return pl.pallas_call(
        functools.partial(_hgrn_kernel, n_heads),
        grid=(b, n_chunks),
        in_specs=[fwd(P_HQ), fwd(0), fwd(P_HI), bwd(P_HQ), bwd(1), bwd(P_HI),
                  pl.BlockSpec((2, width), lambda b, j: (0, 0)),
                  pl.BlockSpec(w_tab.shape, lambda b, j: (0, 0, 0)),
                  pl.BlockSpec(mask_tab.shape, lambda b, j: (0, 0, 0, 0))],
        out_specs=[fwd(0), bwd(0)],
        out_shape=[out_shape, out_shape],
        scratch_shapes=[pltpu.VMEM((2, n_heads, HEAD_DIM, HEAD_DIM), F32),
                        pltpu.VMEM((2, n_heads, c, HEAD_DIM), F32)],
        compiler_params=_cparams(("arbitrary", "arbitrary")),
        name="hgrn_scan",
    )(p, pf, p, p, pf, p, lb_tab, w_tab, mask_tab)


def _rope(z, cos_rep, sin_signed, even_lane):
    partner = jnp.where(even_lane, pltpu.roll(z, LANES - 1, 1), pltpu.roll(z, 1, 1))
    return z * cos_rep + partner * sin_signed


def _ret_kernel(n_heads, lg_ref, qf_ref, kf_ref, vf_ref, cf_ref, sf_ref,
                qb_ref, kb_ref, vb_ref, cb_ref, sb_ref, of_ref, ob_ref, st_ref, dm_ref, in_ref, tl_ref):
    c = SCAN_CHUNK
    k_scale = HEAD_DIM ** -0.5

    @pl.when(pl.program_id(1) == 0)
    def _():
        st_ref[...] = jnp.zeros_like(st_ref)
        ti = lax.broadcasted_iota(jnp.int32, (c, c), 0)
        si = lax.broadcasted_iota(jnp.int32, (c, c), 1)
        rowf = lax.broadcasted_iota(jnp.int32, (c, HEAD_DIM), 0).astype(F32)
        for d in range(2):
            rel = (ti - si) if d == 0 else (si - ti)
            relf = jnp.maximum(rel, 0).astype(F32)
            for h in range(n_heads):
                lg = lg_ref[d, h]
                dm_ref[d, h] = jnp.where(rel >= 0, jnp.exp(relf * lg), 0.0)
                if d == 0:
                    in_ref[d, h] = jnp.exp((rowf + 1.0) * lg)
                    tl_ref[d, h] = jnp.exp((c - 1.0 - rowf) * lg)
                else:
                    in_ref[d, h] = jnp.exp((c - rowf) * lg)
                    tl_ref[d, h] = jnp.exp(rowf * lg)

    even_lane = (lax.broadcasted_iota(jnp.int32, (c, HEAD_DIM), 1) & 1) == 0
    dirs = ((qf_ref, kf_ref, vf_ref, cf_ref, sf_ref, of_ref), (qb_ref, kb_ref, vb_ref, cb_ref, sb_ref, ob_ref))
    def pairs(d, h):
        q_ref, k_ref, v_ref, c_ref, s_ref, _ = dirs[d]
        sl = slice(h * HEAD_DIM, (h + 1) * HEAD_DIM)
        q = _rope(q_ref[0, :, sl].astype(F32), c_ref[...], s_ref[...], even_lane)
        k = _rope(k_ref[0, :, sl].astype(F32), c_ref[...], s_ref[...], even_lane) * k_scale
        qb16 = q.astype(BF16)
        scores = _nt_dot(qb16, k.astype(BF16)) * dm_ref[d, h]
        return dict(d=d, h=h, sl=sl, qb16=qb16, k=k, v=v_ref[0, :, sl], scores=scores)

    def carry(s):
        d, h, v = s["d"], s["h"], s["v"]
        st = st_ref[d, h]
        out = jnp.dot(s["scores"].astype(BF16), v, preferred_element_type=F32)
        out = out + _nt_dot(s["qb16"], st.astype(BF16)) * in_ref[d, h]
        dirs[d][5][0, :, s["sl"]] = out.astype(BF16)
        edge = (c - 1) * (1 - d)
        st_ref[d, h] = (st * in_ref[d, h, edge:edge + 1, :]
                        + _tn_dot(v, (s["k"] * tl_ref[d, h]).astype(BF16)))

    units = [(d, h) for d in range(2) for h in range(n_heads)]
    for g in range(0, len(units), SCAN_INTERLEAVE):
        states = [pairs(d, h) for d, h in units[g:g + SCAN_INTERLEAVE]]
        for s_ in states:
            carry(s_)


def _ret_scan(p, lg, cos_rep, sin_signed, ctx_len, width):
    b, l, _ = p.shape
    c = SCAN_CHUNK
    n_chunks = l // c
    ctx_chunks = ctx_len // c
    n_heads = width // HEAD_DIM

    def fwd(g):
        return pl.BlockSpec((1, c, width), lambda b, j, lg: (b, j, g))

    def bwd(g):
        return pl.BlockSpec((1, c, width), lambda b, j, lg: (b, _bwd_chunk(j, ctx_chunks, n_chunks), g))

    tab_f = pl.BlockSpec((c, HEAD_DIM), lambda b, j, lg: (j, 0))
    tab_b = pl.BlockSpec((c, HEAD_DIM), lambda b, j, lg: (_bwd_chunk(j, ctx_chunks, n_chunks), 0))
    out_shape = jax.ShapeDtypeStruct((b, l, width), BF16)
    return pl.pallas_call(
        functools.partial(_ret_kernel, n_heads),
        grid_spec=pltpu.PrefetchScalarGridSpec(
            num_scalar_prefetch=1,
            grid=(b, n_chunks),
            in_specs=[fwd(P_RQ), fwd(P_RK), fwd(P_RV), tab_f, tab_f,
                      bwd(P_RQ), bwd(P_RK), bwd(P_RV), tab_b, tab_b],
            out_specs=[fwd(0), bwd(0)],
            scratch_shapes=[pltpu.VMEM((2, n_heads, HEAD_DIM, HEAD_DIM), F32),
                            pltpu.VMEM((2, n_heads, c, c), F32),
                            pltpu.VMEM((2, n_heads, c, HEAD_DIM), F32),
                            pltpu.VMEM((2, n_heads, c, HEAD_DIM), F32)]),
        out_shape=[out_shape, out_shape],
        compiler_params=_cparams(("arbitrary", "arbitrary")),
        name="ret_scan",
    )(lg, p, p, p, cos_rep, sin_signed, p, p, p, cos_rep, sin_signed)


def _outproj_kernel(n_hg, n_ret, x_ref, hf_ref, hb_ref, rf_ref, rb_ref, hg_ref, rg_ref, mod_ref,
                    on_ref, wo_ref, g2_ref, wrh_ref, wrl_ref, xo_ref, h2_ref, aff_ref):
    mod = mod_ref[0]

    def mixed(rows):
        o_hg = hf_ref[0, rows, :].astype(F32) + hb_ref[0, rows, :].astype(F32)
        o_rt = rf_ref[0, rows, :].astype(F32) + rb_ref[0, rows, :].astype(F32)
        parts = []
        for h in range(n_hg):
            sl = slice(h * HEAD_DIM, (h + 1) * HEAD_DIM)
            parts.append((_rms(o_hg[:, sl]) * on_ref[...]) * _silu(hg_ref[0, rows, sl].astype(F32)))
        for h in range(n_ret):
            sl = slice(h * HEAD_DIM, (h + 1) * HEAD_DIM)
            parts.append(_rms(o_rt[:, sl]) * _silu(rg_ref[0, rows, sl].astype(F32)))
        return jnp.concatenate(parts, axis=1).astype(BF16)

    def residual(rows, mix):
        x = x_ref[0, rows, :] + mod[2:3] * jnp.dot(mix, wo_ref[0], preferred_element_type=F32)
        xo_ref[0, rows, :] = x
        h2 = (_rms(x) * g2_ref[...]) * (1.0 + mod[4:5]) + mod[3:4]
        h2_ref[0, rows, :] = h2.astype(BF16)
        return h2

    def route(i, h2):
        hh = h2.astype(BF16)
        hl = (h2 - hh.astype(F32)).astype(BF16)
        logits = _nt_dot(wrh_ref[...], hh) + _nt_dot(wrh_ref[...], hl) + _nt_dot(wrl_ref[...], hh)
        mx = jnp.max(logits, axis=0, keepdims=True)
        ex = jnp.exp(logits - mx)
        aff_ref[0, i] = ex / jnp.sum(ex, axis=0, keepdims=True)

    blocks = [slice(i * LANES, (i + 1) * LANES) for i in range(aff_ref.shape[1])]
    mixes = [mixed(rows) for rows in blocks]
    h2s = [residual(rows, mix) for rows, mix in zip(blocks, mixes)]
    for i, h2 in enumerate(h2s):
        route(i, h2)


def _outproj(x, hf, hb, rf, rb, p, modtab, onorm, wo_bf16, layer, g2, w_router, ctx_len, hg_width):
    b, l, d = x.shape
    tm = ROW_TILE
    ctx_tiles = ctx_len // tm
    n_hg = hg_width // HEAD_DIM
    ret_width = d - hg_width
    n_ret = ret_width // HEAD_DIM
    n_e = w_router.shape[1]
    wrt = w_router.T
    wrh = wrt.astype(BF16)
    wrl = (wrt - wrh.astype(F32)).astype(BF16)
    row = pl.BlockSpec((1, tm, d), lambda b, t: (b, t, 0))
    hrow = pl.BlockSpec((1, tm, hg_width), lambda b, t: (b, t, 0))
    rrow = pl.BlockSpec((1, tm, ret_width), lambda b, t: (b, t, 0))
    full = lambda shape: pl.BlockSpec(shape, lambda b, t: tuple(0 for _ in shape))
    return pl.pallas_call(
        functools.partial(_outproj_kernel, n_hg, n_ret),
        grid=(b, l // tm),
        in_specs=[row, hrow, hrow, rrow, rrow,
                  pl.BlockSpec((1, tm, hg_width), lambda b, t: (b, t, P_HGT)),
                  pl.BlockSpec((1, tm, ret_width), lambda b, t: (b, t, P_RGT)),
                  _seg_spec(d, ctx_tiles),
                  full((1, HEAD_DIM)), pl.BlockSpec((1, d, d), lambda b, t: (layer, 0, 0)),
                  full((1, d)), full((n_e, d)), full((n_e, d))],
        out_specs=[row, row, pl.BlockSpec((1, tm // LANES, n_e, LANES), lambda b, t: (b, t, 0, 0))],
        out_shape=[jax.ShapeDtypeStruct((b, l, d), F32),
                   jax.ShapeDtypeStruct((b, l, d), BF16),
                   jax.ShapeDtypeStruct((b, l // LANES, n_e, LANES), F32)],
        compiler_params=_cparams(("arbitrary", "arbitrary")),
        name="outproj",
    )(x, hf, hb, rf, rb, p, p, modtab, onorm.reshape(1, HEAD_DIM), wo_bf16, g2.reshape(1, d), wrh, wrl)


FFN_SPLIT = 2
CAST_ROWS = 64


def _ffn_kernel(layer, x_ref, gate_ref, wg_hbm, wu_hbm, wd_hbm, o_ref, sg_ref, su_ref, sd_ref,
                wg_ref, wu_ref, wd_ref, sem):
    e, t, n_e = pl.program_id(0), pl.program_id(1), pl.num_programs(0)
    pairs = ((wg_hbm, sg_ref, wg_ref), (wu_hbm, su_ref, wu_ref), (wd_hbm, sd_ref, wd_ref))

    def fetch(expert):
        return [pltpu.make_async_copy(hbm.at[layer, expert], stage, sem.at[i])
                for i, (hbm, stage, _) in enumerate(pairs)]

    @pl.when((e == 0) & (t == 0))
    def _():
        for cp in fetch(0):
            cp.start()

    @pl.when(t == 0)
    def _():
        for cp in fetch(e):
            cp.wait()
        for _, stage, w16 in pairs:
            def cast(i, carry, stage=stage, w16=w16):
                rows = pl.ds(pl.multiple_of(i * CAST_ROWS, CAST_ROWS), CAST_ROWS)
                w16[rows, :] = stage[rows, :].astype(BF16)
                return carry
            lax.fori_loop(0, stage.shape[0] // CAST_ROWS, cast, 0)

        @pl.when(e + 1 < n_e)
        def _():
            for cp in fetch(e + 1):
                cp.start()

    x = x_ref[0]
    ff = wg_ref.shape[1]
    piece = ff // FFN_SPLIT
    y = jnp.zeros((x.shape[0], wd_ref.shape[1]), F32)
    for i in range(FFN_SPLIT):
        cs = slice(i * piece, (i + 1) * piece)
        a = jnp.dot(x, wg_ref[:, cs], preferred_element_type=F32)
        u = jnp.dot(x, wu_ref[:, cs], preferred_element_type=F32)
        hid = (_silu(a) * u).astype(BF16)
        y = y + jnp.dot(hid, wd_ref[cs, :], preferred_element_type=F32)
    o_ref[0] = (y * gate_ref[0]).astype(BF16)


def _row_tile(rows, cap=512):
    best = 16
    for t in range(16, cap + 1, 16):
        if rows % t == 0:
            best = t
    return best


def _expert_ffn(xs, gates, wg, wu, wd, layer):
    n_e, rows, d = xs.shape
    ff = wg.shape[3]
    tm = _row_tile(rows)
    assert ff % FFN_SPLIT == 0 and d % CAST_ROWS == 0 and ff % CAST_ROWS == 0
    hbm = pl.BlockSpec(memory_space=pl.ANY)
    return pl.pallas_call(
        functools.partial(_ffn_kernel, layer),
        grid=(n_e, rows // tm),
        in_specs=[pl.BlockSpec((1, tm, d), lambda e, t: (e, t, 0)),
                  pl.BlockSpec((1, tm, 1), lambda e, t: (e, t, 0)),
                  hbm, hbm, hbm],
        out_specs=pl.BlockSpec((1, tm, d), lambda e, t: (e, t, 0)),
        out_shape=jax.ShapeDtypeStruct((n_e, rows, d), BF16),
        scratch_shapes=[pltpu.VMEM((d, ff), F32), pltpu.VMEM((d, ff), F32), pltpu.VMEM((ff, d), F32),
                        pltpu.VMEM((d, ff), BF16), pltpu.VMEM((d, ff), BF16), pltpu.VMEM((ff, d), BF16),
                        pltpu.SemaphoreType.DMA((3,))],
        compiler_params=_cparams(("arbitrary", "arbitrary")),
        name="expert_ffn",
    )(xs, gates, wg, wu, wd)


def _final_kernel(first_tile, start_ref, npass_ref, x_ref, arow_ref, ys_ref, expand_ref, lanemod_ref,
                  gt2_ref, g_ref, o_ref, buf_ref, sem):
    moe = _combine_tile(first_tile, start_ref, npass_ref, arow_ref, ys_ref, expand_ref, lanemod_ref, buf_ref, sem)
    x = x_ref[0] + gt2_ref[0][5:6] * moe
    o_ref[0] = _rms(x) * g_ref[...]


def _final(x, comb, modtab, g, ctx_len):
    b, l, d = x.shape
    tm = ROW_TILE
    ctx_tiles = ctx_len // tm
    n_t = (l - ctx_len) // tm
    row = pl.BlockSpec((1, tm, d), lambda b, t, *_: (b, t + ctx_tiles, 0))
    specs, cargs, scratch = _combine_specs(comb, tm, ctx_tiles)
    return pl.pallas_call(
        functools.partial(_final_kernel, ctx_tiles),
        grid_spec=pltpu.PrefetchScalarGridSpec(
            num_scalar_prefetch=2, grid=(b, n_t),
            in_specs=[row] + specs + [pl.BlockSpec((1, 6, d), lambda b, t, *_: (2 * b + 1, 0, 0)),
                                      pl.BlockSpec((1, d), lambda b, t, *_: (0, 0))],
            out_specs=pl.BlockSpec((1, tm, d), lambda b, t, *_: (b, t, 0)),
            scratch_shapes=scratch),
        out_shape=jax.ShapeDtypeStruct((b, l - ctx_len, d), F32),
        compiler_params=_cparams(("arbitrary", "arbitrary")),
        name="final_norm",
    )(comb[0], comb[1], x, *cargs, modtab, g.reshape(1, d))


def _rope_tables(rows, ctx_len):
    n_freq = HEAD_DIM // 4
    inv = ROPE_BASE ** (-jnp.arange(n_freq, dtype=F32) / n_freq)
    r = jnp.repeat(jnp.arange(rows, dtype=F32), GRID_W)
    cc = jnp.tile(jnp.arange(GRID_W, dtype=F32), rows)
    lat = jnp.concatenate([r[:, None] * inv, cc[:, None] * inv], axis=-1)
    ang = jnp.concatenate([jnp.zeros((ctx_len, 2 * n_freq), F32), lat], axis=0)
    cos = jnp.cos(ang)
    sin = jnp.sin(ang)
    cos_rep = jnp.repeat(cos, 2, axis=-1)
    sin_signed = jnp.stack([-sin, sin], axis=-1).reshape(ang.shape[0], HEAD_DIM)
    return cos_rep, sin_signed


def _select_kernel(nb, cap, aff_ref, tri_ref, pos_ref, off_ref, tot_ref, cum_ref):
    n_e = aff_ref.shape[2]
    bits = lax.bitcast_convert_type(aff_ref[0], jnp.int32)
    ones = jnp.ones((LANES, LANES), BF16)
    tri = tri_ref[...]

    def total(flags):
        return jnp.dot(jnp.sum(flags, axis=0).astype(BF16), ones, preferred_element_type=F32)

    def thr_step(i, thr):
        cand = thr | (jnp.int32(1) << (30 - i))
        cnt = total((bits >= cand[None]).astype(F32))
        return jnp.where(cnt >= cap, cand, thr)

    thr = lax.fori_loop(0, 31, thr_step, jnp.zeros((n_e, LANES), jnp.int32))

    def prefix(flags):
        f2 = flags.reshape(nb * n_e, LANES).astype(BF16)
        incl = jnp.dot(f2, tri, preferred_element_type=F32).reshape(nb, n_e, LANES)
        tot_ref[...] = jnp.dot(f2, ones, preferred_element_type=F32).reshape(nb, n_e, LANES)

        def step(k, carry):
            cum_ref[k] = carry
            return carry + tot_ref[k]

        lax.fori_loop(0, nb, step, jnp.zeros((n_e, LANES), F32))
        return incl, cum_ref[...]

    gt = (bits > thr[None]).astype(F32)
    eq = (bits == thr[None]).astype(F32)
    need = cap - total(gt)
    eq_incl, eq_off = prefix(eq)
    sel = gt + eq * ((eq_off + eq_incl - eq) < need[None]).astype(F32)
    incl, off = prefix(sel)
    pos_ref[0] = jnp.where(sel > 0.5, off + incl - 1.0, -1.0)
    off_ref[0] = off


def _compact_kernel(nb, off_ref, pos_ref, aff_ref, out_ref):
    b = pl.program_id(0)
    n_e = pos_ref.shape[2]
    out_ref[...] = jnp.zeros_like(out_ref)
    slot_row = lax.broadcasted_iota(jnp.int32, (LANES, LANES), 0).astype(F32)
    lane8 = lax.broadcasted_iota(jnp.int32, (8, LANES), 1)
    row8 = lax.broadcasted_iota(jnp.int32, (8, LANES), 0)
    lane_f = lane8.astype(F32)

    def block(k, carry):
        for e in range(n_e):
            start = off_ref[b, k, e]
            blk = start >> 7
            p = pos_ref[0, k, e:e + 1, :]
            g = aff_ref[0, k, e:e + 1, :]
            r = p - jnp.asarray(blk * LANES, F32)
            r = jnp.where(r >= LANES, r - LANES, r)
            onehot = jnp.where((slot_row == r) & (p >= 0.0), 1.0, 0.0).astype(BF16)
            g_hi = g.astype(BF16).astype(F32)
            g_mid = (g - g_hi).astype(BF16).astype(F32)
            g_lo = (g - g_hi - g_mid).astype(BF16).astype(F32)
            vals = jnp.where(row8 == 0, lane_f,
                             jnp.where(row8 == 1, 1.0,
                                       jnp.where(row8 == 2, g_hi,
                                                 jnp.where(row8 == 3, g_mid,
                                                           jnp.where(row8 == 4, g_lo, 0.0)))))
            c = _nt_dot(vals.astype(BF16), onehot)
            tok = c[1:2] * jnp.asarray(k * LANES, F32) + c[0:1]
            gate = c[2:3] + c[3:4] + c[4:5]
            tile = jnp.where(row8 == 0, tok, jnp.where(row8 == 1, gate, 0.0))
            first = lane8 >= (start & (LANES - 1))
            out_ref[0, e, blk] += jnp.where(first, tile, 0.0)
            out_ref[0, e, blk + 1] += jnp.where(first, 0.0, tile)
        return carry

    lax.fori_loop(0, nb, block, 0)


def _route(aff_blocks, cap):
    bsz, nb, n_e, _ = aff_blocks.shape
    assert nb <= 256
    tri = jnp.asarray(np.triu(np.ones((LANES, LANES), np.float32)), BF16)
    blk = pl.BlockSpec((1, nb, n_e, LANES), lambda b: (b, 0, 0, 0))
    shp = jax.ShapeDtypeStruct((bsz, nb, n_e, LANES), F32)
    pos, off_rep = pl.pallas_call(
        functools.partial(_select_kernel, nb, cap),
        grid=(bsz,),
        in_specs=[blk, pl.BlockSpec((LANES, LANES), lambda b: (0, 0))],
        out_specs=[blk, blk],
        out_shape=[shp, shp],
        scratch_shapes=[pltpu.VMEM((nb, n_e, LANES), F32), pltpu.VMEM((nb, n_e, LANES), F32)],
        compiler_params=_cparams(("arbitrary",)),
        name="route_select",
    )(aff_blocks, tri)
    off = off_rep[:, :, :, 0].astype(jnp.int32)
    n_rows = cap // LANES + 2
    blk1 = pl.BlockSpec((1, nb, n_e, LANES), lambda b, off: (b, 0, 0, 0))
    lists = pl.pallas_call(
        functools.partial(_compact_kernel, nb),
        grid_spec=pltpu.PrefetchScalarGridSpec(
            num_scalar_prefetch=1,
            grid=(bsz,),
            in_specs=[blk1, blk1],
            out_specs=pl.BlockSpec((1, n_e, n_rows, 8, LANES), lambda b, off: (b, 0, 0, 0, 0))),
        out_shape=jax.ShapeDtypeStruct((bsz, n_e, n_rows, 8, LANES), F32),
        compiler_params=_cparams(("arbitrary",)),
        name="route_compact",
    )(off, pos, aff_blocks)
    tokens = lists[:, :, :, 0, :].reshape(bsz, n_e, n_rows * LANES)[:, :, :cap].astype(jnp.int32)
    gates = lists[:, :, :, 1, :].reshape(bsz, n_e, n_rows * LANES)[:, :, :cap]
    return gates, tokens, pos, off


def kernel(x, c, ctx, c_ctx, w_ada, b_ada, g_mix, g_ffn, w_in, w_out, hg_lb_logits, hg_onorm,
           ret_decay, w_router, w_e_gate, w_e_up, w_e_down, g_final):
    bsz, n, d = x.shape
    lc = ctx.shape[1]
    depth = w_ada.shape[0]
    l = lc + n
    hg_width = hg_lb_logits.shape[1]
    ret_width = d - hg_width
    n_e = w_router.shape[2]
    assert lc % ROW_TILE == 0 and n % ROW_TILE == 0 and lc % SCAN_CHUNK == 0 and n % SCAN_CHUNK == 0
    assert hg_width % HEAD_DIM == 0 and ret_width % HEAD_DIM == 0 and n % GRID_W == 0
    assert w_in.shape[2] == 5 * hg_width + 4 * ret_width and hg_width == ret_width

    cos_rep, sin_signed = _rope_tables(n // GRID_W, lc)
    w_tab, mask_tab = _decay_tables()

    gamma_cum = jnp.cumsum(jax.nn.softmax(hg_lb_logits.astype(F32), axis=0), axis=0)
    lbs = gamma_cum - gamma_cum[0:1]
    lb_tabs = jnp.stack([jnp.log(lbs), jnp.log1p(-lbs)], axis=1)
    log_gammas = -jnp.exp(ret_decay.astype(F32))

    cond8 = jnp.zeros((8, d), F32).at[:bsz].set(c).at[bsz].set(c_ctx)
    mods = _adaln(cond8, w_ada, b_ada).reshape(depth, 8, 6, d)
    modtabs = jnp.stack([jnp.broadcast_to(mods[:, bsz:bsz + 1], (depth, bsz, 6, d)), mods[:, :bsz]],
                        axis=2).reshape(depth, 2 * bsz, 6, d)

    cap_lat = CAPACITY_FACTOR * n // n_e
    cap_ctx = CAPACITY_FACTOR * lc // n_e
    boff = (jnp.arange(bsz, dtype=jnp.int32) * l)[:, None, None]

    gw = hg_width
    w_in16 = jnp.concatenate([w_in[:, :, gw:3 * gw], w_in[:, :, :gw], w_in[:, :, 3 * gw:]], axis=2).astype(BF16)
    w_out16 = w_out.astype(BF16)
    xs = jnp.concatenate([ctx, x], axis=1)
    comb = None
    for layer in range(depth):
        last = layer == depth - 1
        xs, pf, p = _inproj(xs, comb, modtabs[layer - 1] if layer else None, modtabs[layer], g_mix[layer],
                            w_in16, layer, lc, 2 * hg_width)
        hf, hb = _hgrn_scan(pf, p, lb_tabs[layer], w_tab, mask_tab, lc, hg_width)
        rf, rb = _ret_scan(p, log_gammas[layer], cos_rep, sin_signed, lc, ret_width)
        xs, h2, aff_t = _outproj(xs, hf, hb, rf, rb, p, modtabs[layer], hg_onorm[layer],
                                 w_out16, layer, g_ffn[layer], w_router[layer], lc, hg_width)
        blocks = [lc // LANES, n // LANES]
        caps = [cap_ctx, cap_lat]
        bases = [bsz * cap_lat + jnp.arange(bsz, dtype=jnp.int32) * cap_ctx,
                 jnp.arange(bsz, dtype=jnp.int32) * cap_lat]
        affs = [aff_t[:, :blocks[0]], aff_t[:, blocks[0]:]]
        tok_off = [0, lc]
        rows, gts, arows, starts, cnts = {}, {}, [], [], []
        for seg in (0, 1):
            if seg == 0 and last:
                arows.append(jnp.full((bsz, blocks[0], n_e, LANES), -1.0, F32))
                starts.append(jnp.zeros((bsz, blocks[0], n_e), jnp.int32))
                cnts.append(jnp.zeros((bsz, blocks[0], n_e), jnp.int32))
                continue
            g_s, i_s, pos_s, off_s = _route(affs[seg], caps[seg])
            rows[seg] = i_s + tok_off[seg] + boff
            gts[seg] = g_s
            kept = pos_s >= 0.0
            arows.append(jnp.where(kept, pos_s + bases[seg][:, None, None, None].astype(F32), -1.0))
            starts.append(off_s + bases[seg][:, None, None])
            cnts.append(jnp.sum(kept, axis=-1).astype(jnp.int32))
        order = [s_ for s_ in (1, 0) if s_ in rows]
        flat = jnp.concatenate([rows[s_].transpose(1, 0, 2).reshape(n_e, -1) for s_ in order], axis=1)
        gate = jnp.concatenate([gts[s_].transpose(1, 0, 2).reshape(n_e, -1) for s_ in order], axis=1)
        n_rows = flat.shape[1]
        pad = (-n_rows) % 16
        if pad:
            flat = jnp.pad(flat, ((0, 0), (0, pad)))
            gate = jnp.pad(gate, ((0, 0), (0, pad)))
        h2f = h2.reshape(bsz * l, d)
        gathered = h2f[flat.reshape(-1)].reshape(n_e, n_rows + pad, d)
        ys = _expert_ffn(gathered, gate[:, :, None], w_e_gate, w_e_up, w_e_down, layer)
        arow = jnp.concatenate(arows, axis=1)
        start = jnp.concatenate(starts, axis=1)
        nbt = ROW_TILE // LANES
        start_t = start[:, ::nbt]
        cnt_t = jnp.concatenate(cnts, axis=1).reshape(bsz, -1, nbt, n_e).sum(axis=2)
        need = start_t % YS_ALIGN + cnt_t
        npass = jnp.max(jnp.where(cnt_t > 0, -(-need // COMBINE_ROWS), 0), axis=-1).astype(jnp.int32)
        comb = (start.reshape(bsz, -1), npass, arow, ys)
    return _final(xs, comb, modtabs[depth - 1], g_final, lc)
```

```python
import functools

import numpy as np
import jax
import jax.numpy as jnp
from jax import lax
from jax.experimental import pallas as pl
from jax.experimental.pallas import tpu as pltpu

F32 = jnp.float32
BF16 = jnp.bfloat16

EPS = 1e-6
GRID_W = 64
ROPE_BASE = 10000.0
N_EXPERTS = 16
CAPACITY_FACTOR = 2
HEAD_DIM = 128
LANES = 128
SCAN_CHUNK = 128
ROW_TILE = 256
VMEM_LIMIT = 56 * 1024 * 1024

P_HQ, P_HI, P_HGT, P_RQ, P_RK, P_RV, P_RGT = range(7)

N_LEVELS = int(np.log2(SCAN_CHUNK))
SMALL_HALVES = (4, 2)
W_ROWS = (1 + len(SMALL_HALVES)) * SCAN_CHUNK
LOG2E = float(np.log2(np.e))
SCAN_INTERLEAVE = 4
SCAN_SUB = 2


def _cparams(sem, **kw):
    return pltpu.CompilerParams(dimension_semantics=sem, vmem_limit_bytes=VMEM_LIMIT, **kw)


def _nt_dot(a, b):
    return lax.dot_general(a, b, (((1,), (1,)), ((), ())), preferred_element_type=F32)


def _tn_dot(a, b):
    return lax.dot_general(a, b, (((0,), (0,)), ((), ())), preferred_element_type=F32)


def _sigmoid(x):
    return 1.0 / (1.0 + jnp.exp(-x))


def _silu(x):
    return x * _sigmoid(x)


def _adaln_kernel(cond_ref, w_ref, b_ref, o_ref):
    s = _silu(cond_ref[...])
    o_ref[0] = jnp.dot(s.astype(BF16), w_ref[0].astype(BF16), preferred_element_type=F32) + b_ref[0]


def _adaln(cond8, w_ada, b_ada):
    depth, d, n6 = w_ada.shape
    tn = n6 // 4
    return pl.pallas_call(
        _adaln_kernel,
        grid=(depth, n6 // tn),
        in_specs=[pl.BlockSpec((8, d), lambda l, j: (0, 0)),
                  pl.BlockSpec((1, d, tn), lambda l, j: (l, 0, j)),
                  pl.BlockSpec((1, 1, tn), lambda l, j: (l, 0, j))],
        out_specs=pl.BlockSpec((1, 8, tn), lambda l, j: (l, 0, j)),
        out_shape=jax.ShapeDtypeStruct((depth, 8, n6), F32),
        compiler_params=_cparams(("arbitrary", "arbitrary")),
        name="adaln",
    )(cond8, w_ada, b_ada.reshape(depth, 1, n6))


def _rms(x):
    return x * lax.rsqrt(jnp.mean(x * x, axis=-1, keepdims=True) + EPS)


COMBINE_ROWS = 64
YS_ALIGN = 16


def _combine_tile(first_tile, start_ref, npass_ref, arow_ref, ys_ref, expand_ref, lanemod_ref, buf_ref, sem):
    n_e = arow_ref.shape[2]
    n_blk = arow_ref.shape[1]
    tm = n_blk * LANES
    wn = COMBINE_ROWS
    n_rows = ys_ref.shape[1]
    d = ys_ref.shape[2]
    b, t, n_t = pl.program_id(0), pl.program_id(1), pl.num_programs(1)
    step = b * n_t + t
    n_steps = pl.num_programs(0) * n_t
    slot = step % 2
    wrap = t + 1 == n_t
    b_next = jnp.where(wrap, b + 1, b)
    t_next = jnp.where(wrap, 0, t + 1)

    def window(bb, tt, e, p):
        begin = (start_ref[bb, (tt + first_tile) * n_blk * n_e + e] // YS_ALIGN) * YS_ALIGN
        src = jnp.minimum(begin + p * wn, n_rows - wn)
        return begin, pl.multiple_of(src, YS_ALIGN)

    def copy(to_slot, src, e):
        return pltpu.make_async_copy(ys_ref.at[e, pl.ds(src, wn), :],
                                     buf_ref.at[to_slot, pl.ds(e * wn, wn), :], sem.at[to_slot, e])

    @pl.when(step == 0)
    def _():
        for e in range(n_e):
            copy(slot, window(b, t, e, 0)[1], e).start()

    @pl.when(step + 1 < n_steps)
    def _():
        for e in range(n_e):
            copy(1 - slot, window(b_next, t_next, e, 0)[1], e).start()

    arow = jnp.concatenate([arow_ref[0, i] for i in range(n_blk)], axis=1)
    erow = lax.broadcasted_iota(jnp.int32, (n_e, tm), 0)

    def per_expert(values):
        out = jnp.zeros((n_e, tm), F32)
        for e in range(n_e):
            out = jnp.where(erow == e, values[e].astype(F32), out)
        return out

    rel0 = arow - per_expert([window(b, t, e, 0)[0] for e in range(n_e)])

    def place(p, srcs, acc):
        lo = jnp.asarray(p * wn, F32)
        mine = (arow >= 0.0) & (rel0 >= lo) & (rel0 < lo + wn)
        rel = jnp.where(mine, arow - per_expert(srcs), 255.0).astype(BF16)
        spread = _tn_dot(rel, expand_ref[...])
        onehot = jnp.where(spread == lanemod_ref[...], 1.0, 0.0).astype(BF16)
        for e in range(n_e):
            copy(slot, srcs[e], e).wait()
        return acc + jnp.dot(onehot, buf_ref[slot], preferred_element_type=F32)

    acc = place(0, [window(b, t, e, 0)[1] for e in range(n_e)], jnp.zeros((tm, d), F32))

    def extra_pass(p, acc):
        srcs = [window(b, t, e, p)[1] for e in range(n_e)]
        for e in range(n_e):
            copy(slot, srcs[e], e).start()
        return place(p, srcs, acc)

    return lax.fori_loop(1, npass_ref[b, t + first_tile], extra_pass, acc)


def _inproj_kernel(has_moe, *refs):
    if has_moe:
        (start_ref, npass_ref, x_ref, arow_ref, ys_ref, expand_ref, lanemod_ref, gt2_ref, mod_ref, g_ref, w_ref,
         xo_ref, pf_ref, p_ref, buf_ref, sem) = refs
        moe = _combine_tile(0, start_ref, npass_ref, arow_ref, ys_ref, expand_ref, lanemod_ref, buf_ref, sem)
        x = x_ref[0] + gt2_ref[0][5:6] * moe
        xo_ref[0] = x
    else:
        x_ref, mod_ref, g_ref, w_ref, pf_ref, p_ref = refs
        x = x_ref[0]
    mod = mod_ref[0]
    h = (_rms(x) * g_ref[...]) * (1.0 + mod[1:2]) + mod[0:1]
    res = jnp.dot(h.astype(BF16), w_ref[0], preferred_element_type=F32)
    n_f = pf_ref.shape[2]
    pf_ref[0] = res[:, :n_f]
    p_ref[0] = res[:, n_f:].astype(BF16)


def _seg_spec(d, ctx_tiles):
    return pl.BlockSpec((1, 6, d), lambda b, t, *_: (2 * b + jnp.where(t < ctx_tiles, 0, 1), 0, 0))


def _combine_consts(n_e):
    wn = COMBINE_ROWS
    expand = np.zeros((n_e, n_e * wn), np.float32)
    for e in range(n_e):
        expand[e, e * wn:(e + 1) * wn] = 1.0
    lanemod = (np.arange(n_e * wn) % wn).astype(np.float32)[None, :]
    return jnp.asarray(expand, BF16), jnp.asarray(lanemod, F32)


def _combine_specs(comb, tm, first_tile):
    start, npass, arow, ys = comb
    n_e = arow.shape[2]
    expand, lanemod = _combine_consts(n_e)
    k = n_e * COMBINE_ROWS
    specs = [pl.BlockSpec((1, tm // LANES, n_e, LANES), lambda b, t, *_: (b, t + first_tile, 0, 0)),
             pl.BlockSpec(memory_space=pl.ANY),
             pl.BlockSpec((n_e, k), lambda b, t, *_: (0, 0)),
             pl.BlockSpec((1, k), lambda b, t, *_: (0, 0))]
    scratch = [pltpu.VMEM((2, k, ys.shape[2]), BF16), pltpu.SemaphoreType.DMA((2, n_e))]
    return specs, [arow, ys, expand, lanemod], scratch


def _inproj(x, comb, modtab_prev, modtab, g, w_bf16, layer, ctx_len, n_f32):
    b, l, d = x.shape
    n = w_bf16.shape[2]
    tm = ROW_TILE
    ctx_tiles = ctx_len // tm
    row = pl.BlockSpec((1, tm, d), lambda b, t, *_: (b, t, 0))
    has_moe = comb is not None
    in_specs = [row]
    args = [x]
    scratch = []
    prefetch = []
    if has_moe:
        specs, cargs, scratch = _combine_specs(comb, tm, 0)
        in_specs += specs + [_seg_spec(d, ctx_tiles)]
        args += cargs + [modtab_prev]
        prefetch = [comb[0], comb[1]]
    in_specs += [_seg_spec(d, ctx_tiles),
                 pl.BlockSpec((1, d), lambda b, t, *_: (0, 0)),
                 pl.BlockSpec((1, d, n), lambda b, t, *_: (layer, 0, 0))]
    args += [modtab, g.reshape(1, d), w_bf16]
    out_specs = [pl.BlockSpec((1, tm, n_f32), lambda b, t, *_: (b, t, 0)),
                 pl.BlockSpec((1, tm, n - n_f32), lambda b, t, *_: (b, t, 0))]
    out_shape = [jax.ShapeDtypeStruct((b, l, n_f32), F32), jax.ShapeDtypeStruct((b, l, n - n_f32), BF16)]
    if has_moe:
        out_specs = [row] + out_specs
        out_shape = [jax.ShapeDtypeStruct((b, l, d), F32)] + out_shape
    res = pl.pallas_call(
        functools.partial(_inproj_kernel, has_moe),
        grid_spec=pltpu.PrefetchScalarGridSpec(
            num_scalar_prefetch=len(prefetch), grid=(b, l // tm),
            in_specs=in_specs, out_specs=out_specs, scratch_shapes=scratch),
        out_shape=out_shape,
        compiler_params=_cparams(("arbitrary", "arbitrary")),
        name="inproj",
    )(*prefetch, *args)
    if has_moe:
        return res[0], res[1], res[2]
    return x, res[0], res[1]


def _bwd_chunk(j, ctx_chunks, n_chunks):
    return jnp.where(j < ctx_chunks, ctx_chunks - 1 - j, n_chunks - 1 + ctx_chunks - j)


def _sub_rows(sub):
    c = SCAN_CHUNK
    return (slice(sub * c, (sub + 1) * c), slice((SCAN_SUB - 1 - sub) * c, (SCAN_SUB - sub) * c))


def _decay_tables():
    c = SCAN_CHUNK
    w = np.zeros((W_ROWS, c), np.float32)
    masks = np.zeros((N_LEVELS + 1, c, c), np.float32)
    for lev in range(N_LEVELS):
        m = c >> (lev + 1)
        for t in range(c):
            g0 = (t // (2 * m)) * 2 * m
            bnd = g0 + m - 1
            if t > bnd:
                masks[lev, t, g0:bnd + 1] = 1.0
            if m in SMALL_HALVES:
                r0 = (1 + SMALL_HALVES.index(m)) * c
                if t > bnd:
                    w[r0 + t, bnd + 1:t + 1] = 1.0
                else:
                    w[r0 + t, t + 1:bnd + 1] = 1.0
    masks[N_LEVELS] = np.eye(c, dtype=np.float32)
    for t in range(c):
        w[t, :t + 1] = 1.0
    w_b = w.reshape(-1, c, c)[:, ::-1, ::-1].reshape(-1, c)
    masks_b = masks[:, ::-1, ::-1]
    return (jnp.asarray(np.stack([w, w_b]), BF16), jnp.asarray(np.stack([masks, masks_b]), F32))


def _hgrn_kernel(n_heads, qf_ref, ff_ref, if_ref, qb_ref, fb_ref, ib_ref, lb_ref, w_ref, mask_ref,
                 of_ref, ob_ref, st_ref, a_ref):
    c = SCAN_CHUNK

    @pl.when(pl.program_id(1) == 0)
    def _():
        st_ref[...] = jnp.zeros_like(st_ref)

    dirs = ((qf_ref, ff_ref, if_ref, of_ref), (qb_ref, fb_ref, ib_ref, ob_ref))

    def gates(d, h, rows):
        q_ref, f_ref, i_ref, _ = dirs[d]
        sl = slice(h * HEAD_DIM, (h + 1) * HEAD_DIM)
        qb16 = q_ref[0, rows, sl]
        xf = f_ref[0, rows, sl]
        b_ = lb_ref[1:2, sl] + (jnp.minimum(xf, 0.0) - jnp.log(1.0 + jnp.exp(-jnp.abs(xf))))
        a_ = lb_ref[0:1, sl]
        lf2 = (jnp.maximum(a_, b_) + jnp.log(1.0 + jnp.exp(-jnp.abs(a_ - b_)))) * LOG2E
        f = jnp.exp2(lf2)
        k = 1.0 - f
        hi = lf2.astype(BF16)
        lo = (lf2 - hi.astype(F32)).astype(BF16)
        ex2 = jnp.dot(w_ref[d], jnp.concatenate([hi, lo], axis=1), preferred_element_type=F32)
        ex = ex2[:, :HEAD_DIM] + ex2[:, HEAD_DIM:]
        cum = ex[:c]
        a_ref[d, h] = cum
        return dict(d=d, h=h, sl=sl, rows=rows, qb16=qb16, q=qb16.astype(F32), v=i_ref[0, rows, sl], f=f, k=k,
                    ex=ex, cum=cum)

    def diagonal(s):
        s["kb16"] = s["k"].astype(BF16)
        s["scores"] = _nt_dot(s["qb16"], s["kb16"]) * mask_ref[s["d"], N_LEVELS]

    def level(s, lev):
        d, h, q, k, ex, cum = s["d"], s["h"], s["q"], s["k"], s["ex"], s["cum"]
        m = c >> (lev + 1)
        if m == 1:
            lhs, rhs = (q * s["f"]).astype(BF16), s["kb16"]
        else:
            if m in SMALL_HALVES:
                r0 = (1 + SMALL_HALVES.index(m)) * c
                e = jnp.exp2(ex[r0:r0 + c])
            else:
                parts = []
                for g0 in range(0, c, 2 * m):
                    mid = g0 + m - 1 + d
                    parts.append(cum[g0:g0 + 2 * m] - a_ref[d, h, mid:mid + 1, :])
                e = jnp.exp2(-jnp.abs(jnp.concatenate(parts, axis=0) if len(parts) > 1 else parts[0]))
            lhs, rhs = (q * e).astype(BF16), (k * e).astype(BF16)
        s["scores"] = s["scores"] + _nt_dot(lhs, rhs) * mask_ref[d, lev]

    def carry(s):
        d, h, q, k, v, cum = s["d"], s["h"], s["q"], s["k"], s["v"], s["cum"]
        edge = (c - 1) * (1 - d)
        tot = a_ref[d, h, edge:edge + 1, :]
        st = st_ref[d, h]
        out = jnp.dot(s["scores"].astype(BF16), v, preferred_element_type=F32)
        out = out + _nt_dot((q * jnp.exp2(cum)).astype(BF16), st.astype(BF16))
        dirs[d][3][0, s["rows"], s["sl"]] = out.astype(BF16)
        kdec = (k * jnp.exp2(tot - cum)).astype(BF16)
        st_ref[d, h] = st * jnp.exp2(tot) + _tn_dot(v, kdec)

    units = [(d, h) for d in range(2) for h in range(n_heads)]
    for sub in range(SCAN_SUB):
        rows = _sub_rows(sub)
        for g in range(0, len(units), SCAN_INTERLEAVE):
            states = [gates(d, h, rows[d]) for d, h in units[g:g + SCAN_INTERLEAVE]]
            for s_ in states:
                diagonal(s_)
                for lev in range(N_LEVELS):
                    level(s_, lev)
            for s_ in states:
                carry(s_)


def _hgrn_scan(pf, p, lb_tab, w_tab, mask_tab, ctx_len, width):
    b, l, _ = p.shape
    c = SCAN_CHUNK
    rows = SCAN_SUB * c
    n_chunks = l // rows
    ctx_chunks = ctx_len // rows
    n_heads = width // HEAD_DIM

    def fwd(g):
        return pl.BlockSpec((1, rows, width), lambda b, j: (b, j, g))

    def bwd(g):
        return pl.BlockSpec((1, rows, width), lambda b, j: (b, _bwd_chunk(j, ctx_chunks, n_chunks), g))

    out_shape = jax.ShapeDtypeStruct((b, l, width), BF16)
    return pl.pallas_call(
        functools.partial(_hgrn_kernel, n_heads),
        grid=(b, n_chunks),
        in_specs=[fwd(P_HQ), fwd(0), fwd(P_HI), bwd(P_HQ), bwd(1), bwd(P_HI),
                  pl.BlockSpec((2, width), lambda b, j: (0, 0)),
                  pl.BlockSpec(w_tab.shape, lambda b, j: (0, 0, 0)),
                  pl.BlockSpec(mask_tab.shape, lambda b, j: (0, 0, 0, 0))],
        out_specs=[fwd(0), bwd(0)],
        out_shape=[out_shape, out_shape],
        scratch_shapes=[pltpu.VMEM((2, n_heads, HEAD_DIM, HEAD_DIM), F32),
                        pltpu.VMEM((2, n_heads, c, HEAD_DIM), F32)],
        compiler_params=_cparams(("arbitrary", "arbitrary")),
        name="hgrn_scan",
    )(p, pf, p, p, pf, p, lb_tab, w_tab, mask_tab)


def _rope(z, cos_rep, sin_signed, even_lane):
    partner = jnp.where(even_lane, pltpu.roll(z, LANES - 1, 1), pltpu.roll(z, 1, 1))
    return z * cos_rep + partner * sin_signed


def _ret_kernel(n_heads, lg_ref, qf_ref, kf_ref, vf_ref, cf_ref, sf_ref,
                qb_ref, kb_ref, vb_ref, cb_ref, sb_ref, of_ref, ob_ref, st_ref, dm_ref, in_ref, tl_ref):
    c = SCAN_CHUNK
    k_scale = HEAD_DIM ** -0.5

    @pl.when(pl.program_id(1) == 0)
    def _():
        st_ref[...] = jnp.zeros_like(st_ref)
        ti = lax.broadcasted_iota(jnp.int32, (c, c), 0)
        si = lax.broadcasted_iota(jnp.int32, (c, c), 1)
        rowf = lax.broadcasted_iota(jnp.int32, (c, HEAD_DIM), 0).astype(F32)
        for d in range(2):
            rel = (ti - si) if d == 0 else (si - ti)
            relf = jnp.maximum(rel, 0).astype(F32)
            for h in range(n_heads):
                lg = lg_ref[d, h]
                dm_ref[d, h] = jnp.where(rel >= 0, jnp.exp(relf * lg), 0.0)
                if d == 0:
                    in_ref[d, h] = jnp.exp((rowf + 1.0) * lg)
                    tl_ref[d, h] = jnp.exp((c - 1.0 - rowf) * lg)
                else:
                    in_ref[d, h] = jnp.exp((c - rowf) * lg)
                    tl_ref[d, h] = jnp.exp(rowf * lg)

    even_lane = (lax.broadcasted_iota(jnp.int32, (c, HEAD_DIM), 1) & 1) == 0
    dirs = ((qf_ref, kf_ref, vf_ref, cf_ref, sf_ref, of_ref), (qb_ref, kb_ref, vb_ref, cb_ref, sb_ref, ob_ref))
    def pairs(d, h, rows):
        q_ref, k_ref, v_ref, c_ref, s_ref, _ = dirs[d]
        sl = slice(h * HEAD_DIM, (h + 1) * HEAD_DIM)
        q = _rope(q_ref[0, rows, sl].astype(F32), c_ref[rows, :], s_ref[rows, :], even_lane)
        k = _rope(k_ref[0, rows, sl].astype(F32), c_ref[rows, :], s_ref[rows, :], even_lane) * k_scale
        qb16 = q.astype(BF16)
        scores = _nt_dot(qb16, k.astype(BF16)) * dm_ref[d, h]
        return dict(d=d, h=h, sl=sl, rows=rows, qb16=qb16, k=k, v=v_ref[0, rows, sl], scores=scores)

    def carry(s):
        d, h, v = s["d"], s["h"], s["v"]
        st = st_ref[d, h]
        out = jnp.dot(s["scores"].astype(BF16), v, preferred_element_type=F32)
        out = out + _nt_dot(s["qb16"], st.astype(BF16)) * in_ref[d, h]
        dirs[d][5][0, s["rows"], s["sl"]] = out.astype(BF16)
        edge = (c - 1) * (1 - d)
        st_ref[d, h] = (st * in_ref[d, h, edge:edge + 1, :]
                        + _tn_dot(v, (s["k"] * tl_ref[d, h]).astype(BF16)))

    units = [(d, h) for d in range(2) for h in range(n_heads)]
    for sub in range(SCAN_SUB):
        rows = _sub_rows(sub)
        for g in range(0, len(units), SCAN_INTERLEAVE):
            states = [pairs(d, h, rows[d]) for d, h in units[g:g + SCAN_INTERLEAVE]]
            for s_ in states:
                carry(s_)


def _ret_scan(p, lg, cos_rep, sin_signed, ctx_len, width):
    b, l, _ = p.shape
    c = SCAN_CHUNK
    rows = SCAN_SUB * c
    n_chunks = l // rows
    ctx_chunks = ctx_len // rows
    n_heads = width // HEAD_DIM

    def fwd(g):
        return pl.BlockSpec((1, rows, width), lambda b, j, lg: (b, j, g))

    def bwd(g):
        return pl.BlockSpec((1, rows, width), lambda b, j, lg: (b, _bwd_chunk(j, ctx_chunks, n_chunks), g))

    tab_f = pl.BlockSpec((rows, HEAD_DIM), lambda b, j, lg: (j, 0))
    tab_b = pl.BlockSpec((rows, HEAD_DIM), lambda b, j, lg: (_bwd_chunk(j, ctx_chunks, n_chunks), 0))
    out_shape = jax.ShapeDtypeStruct((b, l, width), BF16)
    return pl.pallas_call(
        functools.partial(_ret_kernel, n_heads),
        grid_spec=pltpu.PrefetchScalarGridSpec(
            num_scalar_prefetch=1,
            grid=(b, n_chunks),
            in_specs=[fwd(P_RQ), fwd(P_RK), fwd(P_RV), tab_f, tab_f,
                      bwd(P_RQ), bwd(P_RK), bwd(P_RV), tab_b, tab_b],
            out_specs=[fwd(0), bwd(0)],
            scratch_shapes=[pltpu.VMEM((2, n_heads, HEAD_DIM, HEAD_DIM), F32),
                            pltpu.VMEM((2, n_heads, c, c), F32),
                            pltpu.VMEM((2, n_heads, c, HEAD_DIM), F32),
                            pltpu.VMEM((2, n_heads, c, HEAD_DIM), F32)]),
        out_shape=[out_shape, out_shape],
        compiler_params=_cparams(("arbitrary", "arbitrary")),
        name="ret_scan",
    )(lg, p, p, p, cos_rep, sin_signed, p, p, p, cos_rep, sin_signed)


def _outproj_kernel(n_hg, n_ret, x_ref, hf_ref, hb_ref, rf_ref, rb_ref, hg_ref, rg_ref, mod_ref,
                    on_ref, wo_ref, g2_ref, wrh_ref, wrl_ref, xo_ref, h2_ref, aff_ref):
    mod = mod_ref[0]

    def mixed(rows):
        o_hg = hf_ref[0, rows, :].astype(F32) + hb_ref[0, rows, :].astype(F32)
        o_rt = rf_ref[0, rows, :].astype(F32) + rb_ref[0, rows, :].astype(F32)
        parts = []
        for h in range(n_hg):
            sl = slice(h * HEAD_DIM, (h + 1) * HEAD_DIM)
            parts.append((_rms(o_hg[:, sl]) * on_ref[...]) * _silu(hg_ref[0, rows, sl].astype(F32)))
        for h in range(n_ret):
            sl = slice(h * HEAD_DIM, (h + 1) * HEAD_DIM)
            parts.append(_rms(o_rt[:, sl]) * _silu(rg_ref[0, rows, sl].astype(F32)))
        return jnp.concatenate(parts, axis=1).astype(BF16)

    def residual(rows, mix):
        x = x_ref[0, rows, :] + mod[2:3] * jnp.dot(mix, wo_ref[0], preferred_element_type=F32)
        xo_ref[0, rows, :] = x
        h2 = (_rms(x) * g2_ref[...]) * (1.0 + mod[4:5]) + mod[3:4]
        h2_ref[0, rows, :] = h2.astype(BF16)
        return h2

    def route(i, h2):
        hh = h2.astype(BF16)
        hl = (h2 - hh.astype(F32)).astype(BF16)
        logits = _nt_dot(wrh_ref[...], hh) + _nt_dot(wrh_ref[...], hl) + _nt_dot(wrl_ref[...], hh)
        mx = jnp.max(logits, axis=0, keepdims=True)
        ex = jnp.exp(logits - mx)
        aff_ref[0, i] = ex / jnp.sum(ex, axis=0, keepdims=True)

    blocks = [slice(i * LANES, (i + 1) * LANES) for i in range(aff_ref.shape[1])]
    mixes = [mixed(rows) for rows in blocks]
    h2s = [residual(rows, mix) for rows, mix in zip(blocks, mixes)]
    for i, h2 in enumerate(h2s):
        route(i, h2)


def _outproj(x, hf, hb, rf, rb, p, modtab, onorm, wo_bf16, layer, g2, w_router, ctx_len, hg_width):
    b, l, d = x.shape
    tm = ROW_TILE
    ctx_tiles = ctx_len // tm
    n_hg = hg_width // HEAD_DIM
    ret_width = d - hg_width
    n_ret = ret_width // HEAD_DIM
    n_e = w_router.shape[1]
    wrt = w_router.T
    wrh = wrt.astype(BF16)
    wrl = (wrt - wrh.astype(F32)).astype(BF16)
    row = pl.BlockSpec((1, tm, d), lambda b, t: (b, t, 0))
    hrow = pl.BlockSpec((1, tm, hg_width), lambda b, t: (b, t, 0))
    rrow = pl.BlockSpec((1, tm, ret_width), lambda b, t: (b, t, 0))
    full = lambda shape: pl.BlockSpec(shape, lambda b, t: tuple(0 for _ in shape))
    return pl.pallas_call(
        functools.partial(_outproj_kernel, n_hg, n_ret),
        grid=(b, l // tm),
        in_specs=[row, hrow, hrow, rrow, rrow,
                  pl.BlockSpec((1, tm, hg_width), lambda b, t: (b, t, P_HGT)),
                  pl.BlockSpec((1, tm, ret_width), lambda b, t: (b, t, P_RGT)),
                  _seg_spec(d, ctx_tiles),
                  full((1, HEAD_DIM)), pl.BlockSpec((1, d, d), lambda b, t: (layer, 0, 0)),
                  full((1, d)), full((n_e, d)), full((n_e, d))],
        out_specs=[row, row, pl.BlockSpec((1, tm // LANES, n_e, LANES), lambda b, t: (b, t, 0, 0))],
        out_shape=[jax.ShapeDtypeStruct((b, l, d), F32),
                   jax.ShapeDtypeStruct((b, l, d), BF16),
                   jax.ShapeDtypeStruct((b, l // LANES, n_e, LANES), F32)],
        compiler_params=_cparams(("arbitrary", "arbitrary")),
        name="outproj",
    )(x, hf, hb, rf, rb, p, p, modtab, onorm.reshape(1, HEAD_DIM), wo_bf16, g2.reshape(1, d), wrh, wrl)


FFN_SPLIT = 2
CAST_ROWS = 64


def _ffn_kernel(layer, x_ref, gate_ref, wg_hbm, wu_hbm, wd_hbm, o_ref, sg_ref, su_ref, sd_ref,
                wg_ref, wu_ref, wd_ref, sem):
    e, t, n_e = pl.program_id(0), pl.program_id(1), pl.num_programs(0)
    pairs = ((wg_hbm, sg_ref, wg_ref), (wu_hbm, su_ref, wu_ref), (wd_hbm, sd_ref, wd_ref))

    def fetch(expert):
        return [pltpu.make_async_copy(hbm.at[layer, expert], stage, sem.at[i])
                for i, (hbm, stage, _) in enumerate(pairs)]

    @pl.when((e == 0) & (t == 0))
    def _():
        for cp in fetch(0):
            cp.start()

    @pl.when(t == 0)
    def _():
        for cp in fetch(e):
            cp.wait()
        for _, stage, w16 in pairs:
            def cast(i, carry, stage=stage, w16=w16):
                rows = pl.ds(pl.multiple_of(i * CAST_ROWS, CAST_ROWS), CAST_ROWS)
                w16[rows, :] = stage[rows, :].astype(BF16)
                return carry
            lax.fori_loop(0, stage.shape[0] // CAST_ROWS, cast, 0)

        @pl.when(e + 1 < n_e)
        def _():
            for cp in fetch(e + 1):
                cp.start()

    x = x_ref[0]
    ff = wg_ref.shape[1]
    piece = ff // FFN_SPLIT
    y = jnp.zeros((x.shape[0], wd_ref.shape[1]), F32)
    for i in range(FFN_SPLIT):
        cs = slice(i * piece, (i + 1) * piece)
        a = jnp.dot(x, wg_ref[:, cs], preferred_element_type=F32)
        u = jnp.dot(x, wu_ref[:, cs], preferred_element_type=F32)
        hid = (_silu(a) * u).astype(BF16)
        y = y + jnp.dot(hid, wd_ref[cs, :], preferred_element_type=F32)
    o_ref[0] = (y * gate_ref[0]).astype(BF16)


def _row_tile(rows, cap=512):
    best = 16
    for t in range(16, cap + 1, 16):
        if rows % t == 0:
            best = t
    return best


def _expert_ffn(xs, gates, wg, wu, wd, layer):
    n_e, rows, d = xs.shape
    ff = wg.shape[3]
    tm = _row_tile(rows)
    assert ff % FFN_SPLIT == 0 and d % CAST_ROWS == 0 and ff % CAST_ROWS == 0
    hbm = pl.BlockSpec(memory_space=pl.ANY)
    return pl.pallas_call(
        functools.partial(_ffn_kernel, layer),
        grid=(n_e, rows // tm),
        in_specs=[pl.BlockSpec((1, tm, d), lambda e, t: (e, t, 0)),
                  pl.BlockSpec((1, tm, 1), lambda e, t: (e, t, 0)),
                  hbm, hbm, hbm],
        out_specs=pl.BlockSpec((1, tm, d), lambda e, t: (e, t, 0)),
        out_shape=jax.ShapeDtypeStruct((n_e, rows, d), BF16),
        scratch_shapes=[pltpu.VMEM((d, ff), F32), pltpu.VMEM((d, ff), F32), pltpu.VMEM((ff, d), F32),
                        pltpu.VMEM((d, ff), BF16), pltpu.VMEM((d, ff), BF16), pltpu.VMEM((ff, d), BF16),
                        pltpu.SemaphoreType.DMA((3,))],
        compiler_params=_cparams(("arbitrary", "arbitrary")),
        name="expert_ffn",
    )(xs, gates, wg, wu, wd)


def _final_kernel(first_tile, start_ref, npass_ref, x_ref, arow_ref, ys_ref, expand_ref, lanemod_ref,
                  gt2_ref, g_ref, o_ref, buf_ref, sem):
    moe = _combine_tile(first_tile, start_ref, npass_ref, arow_ref, ys_ref, expand_ref, lanemod_ref, buf_ref, sem)
    x = x_ref[0] + gt2_ref[0][5:6] * moe
    o_ref[0] = _rms(x) * g_ref[...]


def _final(x, comb, modtab, g, ctx_len):
    b, l, d = x.shape
    tm = ROW_TILE
    ctx_tiles = ctx_len // tm
    n_t = (l - ctx_len) // tm
    row = pl.BlockSpec((1, tm, d), lambda b, t, *_: (b, t + ctx_tiles, 0))
    specs, cargs, scratch = _combine_specs(comb, tm, ctx_tiles)
    return pl.pallas_call(
        functools.partial(_final_kernel, ctx_tiles),
        grid_spec=pltpu.PrefetchScalarGridSpec(
            num_scalar_prefetch=2, grid=(b, n_t),
            in_specs=[row] + specs + [pl.BlockSpec((1, 6, d), lambda b, t, *_: (2 * b + 1, 0, 0)),
                                      pl.BlockSpec((1, d), lambda b, t, *_: (0, 0))],
            out_specs=pl.BlockSpec((1, tm, d), lambda b, t, *_: (b, t, 0)),
            scratch_shapes=scratch),
        out_shape=jax.ShapeDtypeStruct((b, l - ctx_len, d), F32),
        compiler_params=_cparams(("arbitrary", "arbitrary")),
        name="final_norm",
    )(comb[0], comb[1], x, *cargs, modtab, g.reshape(1, d))


def _rope_tables(rows, ctx_len):
    n_freq = HEAD_DIM // 4
    inv = ROPE_BASE ** (-jnp.arange(n_freq, dtype=F32) / n_freq)
    r = jnp.repeat(jnp.arange(rows, dtype=F32), GRID_W)
    cc = jnp.tile(jnp.arange(GRID_W, dtype=F32), rows)
    lat = jnp.concatenate([r[:, None] * inv, cc[:, None] * inv], axis=-1)
    ang = jnp.concatenate([jnp.zeros((ctx_len, 2 * n_freq), F32), lat], axis=0)
    cos = jnp.cos(ang)
    sin = jnp.sin(ang)
    cos_rep = jnp.repeat(cos, 2, axis=-1)
    sin_signed = jnp.stack([-sin, sin], axis=-1).reshape(ang.shape[0], HEAD_DIM)
    return cos_rep, sin_signed


def _select_kernel(nb, cap, aff_ref, tri_ref, pos_ref, off_ref, tot_ref, cum_ref):
    n_e = aff_ref.shape[2]
    bits = lax.bitcast_convert_type(aff_ref[0], jnp.int32)
    ones = jnp.ones((LANES, LANES), BF16)
    tri = tri_ref[...]

    def total(flags):
        return jnp.dot(jnp.sum(flags, axis=0).astype(BF16), ones, preferred_element_type=F32)

    def thr_step(i, thr):
        cand = thr | (jnp.int32(1) << (30 - i))
        cnt = total((bits >= cand[None]).astype(F32))
        return jnp.where(cnt >= cap, cand, thr)

    thr = lax.fori_loop(0, 31, thr_step, jnp.zeros((n_e, LANES), jnp.int32))

    def prefix(flags):
        f2 = flags.reshape(nb * n_e, LANES).astype(BF16)
        incl = jnp.dot(f2, tri, preferred_element_type=F32).reshape(nb, n_e, LANES)
        tot_ref[...] = jnp.dot(f2, ones, preferred_element_type=F32).reshape(nb, n_e, LANES)

        def step(k, carry):
            cum_ref[k] = carry
            return carry + tot_ref[k]

        lax.fori_loop(0, nb, step, jnp.zeros((n_e, LANES), F32))
        return incl, cum_ref[...]

    gt = (bits > thr[None]).astype(F32)
    eq = (bits == thr[None]).astype(F32)
    need = cap - total(gt)
    eq_incl, eq_off = prefix(eq)
    sel = gt + eq * ((eq_off + eq_incl - eq) < need[None]).astype(F32)
    incl, off = prefix(sel)
    pos_ref[0] = jnp.where(sel > 0.5, off + incl - 1.0, -1.0)
    off_ref[0] = off


def _compact_kernel(nb, off_ref, pos_ref, aff_ref, out_ref):
    b = pl.program_id(0)
    n_e = pos_ref.shape[2]
    out_ref[...] = jnp.zeros_like(out_ref)
    slot_row = lax.broadcasted_iota(jnp.int32, (LANES, LANES), 0).astype(F32)
    lane8 = lax.broadcasted_iota(jnp.int32, (8, LANES), 1)
    row8 = lax.broadcasted_iota(jnp.int32, (8, LANES), 0)
    lane_f = lane8.astype(F32)

    def block(k, carry):
        starts = [off_ref[b, k, e] for e in range(n_e)]
        picked = []
        for e in range(n_e):
            blk = starts[e] >> 7
            p = pos_ref[0, k, e:e + 1, :]
            g = aff_ref[0, k, e:e + 1, :]
            r = p - jnp.asarray(blk * LANES, F32)
            r = jnp.where(r >= LANES, r - LANES, r)
            onehot = jnp.where((slot_row == r) & (p >= 0.0), 1.0, 0.0).astype(BF16)
            g_hi = g.astype(BF16).astype(F32)
            g_mid = (g - g_hi).astype(BF16).astype(F32)
            g_lo = (g - g_hi - g_mid).astype(BF16).astype(F32)
            vals = jnp.where(row8 == 0, lane_f,
                             jnp.where(row8 == 1, 1.0,
                                       jnp.where(row8 == 2, g_hi,
                                                 jnp.where(row8 == 3, g_mid,
                                                           jnp.where(row8 == 4, g_lo, 0.0)))))
            picked.append(_nt_dot(vals.astype(BF16), onehot))
        for e in range(n_e):
            c = picked[e]
            blk = starts[e] >> 7
            tok = c[1:2] * jnp.asarray(k * LANES, F32) + c[0:1]
            gate = c[2:3] + c[3:4] + c[4:5]
            tile = jnp.where(row8 == 0, tok, jnp.where(row8 == 1, gate, 0.0))
            first = lane8 >= (starts[e] & (LANES - 1))
            out_ref[0, e, blk] += jnp.where(first, tile, 0.0)
            out_ref[0, e, blk + 1] += jnp.where(first, 0.0, tile)
        return carry

    lax.fori_loop(0, nb, block, 0)


def _route(aff_blocks, cap):
    bsz, nb, n_e, _ = aff_blocks.shape
    assert nb <= 256
    tri = jnp.asarray(np.triu(np.ones((LANES, LANES), np.float32)), BF16)
    blk = pl.BlockSpec((1, nb, n_e, LANES), lambda b: (b, 0, 0, 0))
    shp = jax.ShapeDtypeStruct((bsz, nb, n_e, LANES), F32)
    pos, off_rep = pl.pallas_call(
        functools.partial(_select_kernel, nb, cap),
        grid=(bsz,),
        in_specs=[blk, pl.BlockSpec((LANES, LANES), lambda b: (0, 0))],
        out_specs=[blk, blk],
        out_shape=[shp, shp],
        scratch_shapes=[pltpu.VMEM((nb, n_e, LANES), F32), pltpu.VMEM((nb, n_e, LANES), F32)],
        compiler_params=_cparams(("arbitrary",)),
        name="route_select",
    )(aff_blocks, tri)
    off = off_rep[:, :, :, 0].astype(jnp.int32)
    n_rows = cap // LANES + 2
    blk1 = pl.BlockSpec((1, nb, n_e, LANES), lambda b, off: (b, 0, 0, 0))
    lists = pl.pallas_call(
        functools.partial(_compact_kernel, nb),
        grid_spec=pltpu.PrefetchScalarGridSpec(
            num_scalar_prefetch=1,
            grid=(bsz,),
            in_specs=[blk1, blk1],
            out_specs=pl.BlockSpec((1, n_e, n_rows, 8, LANES), lambda b, off: (b, 0, 0, 0, 0))),
        out_shape=jax.ShapeDtypeStruct((bsz, n_e, n_rows, 8, LANES), F32),
        compiler_params=_cparams(("arbitrary",)),
        name="route_compact",
    )(off, pos, aff_blocks)
    tokens = lists[:, :, :, 0, :].reshape(bsz, n_e, n_rows * LANES)[:, :, :cap].astype(jnp.int32)
    gates = lists[:, :, :, 1, :].reshape(bsz, n_e, n_rows * LANES)[:, :, :cap]
    return gates, tokens, pos, off


def kernel(x, c, ctx, c_ctx, w_ada, b_ada, g_mix, g_ffn, w_in, w_out, hg_lb_logits, hg_onorm,
           ret_decay, w_router, w_e_gate, w_e_up, w_e_down, g_final):
    bsz, n, d = x.shape
    lc = ctx.shape[1]
    depth = w_ada.shape[0]
    l = lc + n
    hg_width = hg_lb_logits.shape[1]
    ret_width = d - hg_width
    n_e = w_router.shape[2]
    assert lc % ROW_TILE == 0 and n % ROW_TILE == 0
    assert lc % (SCAN_SUB * SCAN_CHUNK) == 0 and n % (SCAN_SUB * SCAN_CHUNK) == 0
    assert hg_width % HEAD_DIM == 0 and ret_width % HEAD_DIM == 0 and n % GRID_W == 0
    assert w_in.shape[2] == 5 * hg_width + 4 * ret_width and hg_width == ret_width

    cos_rep, sin_signed = _rope_tables(n // GRID_W, lc)
    w_tab, mask_tab = _decay_tables()

    gamma_cum = jnp.cumsum(jax.nn.softmax(hg_lb_logits.astype(F32), axis=0), axis=0)
    lbs = gamma_cum - gamma_cum[0:1]
    lb_tabs = jnp.stack([jnp.log(lbs), jnp.log1p(-lbs)], axis=1)
    log_gammas = -jnp.exp(ret_decay.astype(F32))

    cond8 = jnp.zeros((8, d), F32).at[:bsz].set(c).at[bsz].set(c_ctx)
    mods = _adaln(cond8, w_ada, b_ada).reshape(depth, 8, 6, d)
    modtabs = jnp.stack([jnp.broadcast_to(mods[:, bsz:bsz + 1], (depth, bsz, 6, d)), mods[:, :bsz]],
                        axis=2).reshape(depth, 2 * bsz, 6, d)

    cap_lat = CAPACITY_FACTOR * n // n_e
    cap_ctx = CAPACITY_FACTOR * lc // n_e
    boff = (jnp.arange(bsz, dtype=jnp.int32) * l)[:, None, None]

    gw = hg_width
    w_in16 = jnp.concatenate([w_in[:, :, gw:3 * gw], w_in[:, :, :gw], w_in[:, :, 3 * gw:]], axis=2).astype(BF16)
    w_out16 = w_out.astype(BF16)
    xs = jnp.concatenate([ctx, x], axis=1)
    comb = None
    for layer in range(depth):
        last = layer == depth - 1
        xs, pf, p = _inproj(xs, comb, modtabs[layer - 1] if layer else None, modtabs[layer], g_mix[layer],
                            w_in16, layer, lc, 2 * hg_width)
        hf, hb = _hgrn_scan(pf, p, lb_tabs[layer], w_tab, mask_tab, lc, hg_width)
        rf, rb = _ret_scan(p, log_gammas[layer], cos_rep, sin_signed, lc, ret_width)
        xs, h2, aff_t = _outproj(xs, hf, hb, rf, rb, p, modtabs[layer], hg_onorm[layer],
                                 w_out16, layer, g_ffn[layer], w_router[layer], lc, hg_width)
        blocks = [lc // LANES, n // LANES]
        caps = [cap_ctx, cap_lat]
        bases = [bsz * cap_lat + jnp.arange(bsz, dtype=jnp.int32) * cap_ctx,
                 jnp.arange(bsz, dtype=jnp.int32) * cap_lat]
        affs = [aff_t[:, :blocks[0]], aff_t[:, blocks[0]:]]
        tok_off = [0, lc]
        rows, gts, arows, starts, cnts = {}, {}, [], [], []
        for seg in (0, 1):
            if seg == 0 and last:
                arows.append(jnp.full((bsz, blocks[0], n_e, LANES), -1.0, F32))
                starts.append(jnp.zeros((bsz, blocks[0], n_e), jnp.int32))
                cnts.append(jnp.zeros((bsz, blocks[0], n_e), jnp.int32))
                continue
            g_s, i_s, pos_s, off_s = _route(affs[seg], caps[seg])
            rows[seg] = i_s + tok_off[seg] + boff
            gts[seg] = g_s
            kept = pos_s >= 0.0
            arows.append(jnp.where(kept, pos_s + bases[seg][:, None, None, None].astype(F32), -1.0))
            starts.append(off_s + bases[seg][:, None, None])
            cnts.append(jnp.sum(kept, axis=-1).astype(jnp.int32))
        order = [s_ for s_ in (1, 0) if s_ in rows]
        flat = jnp.concatenate([rows[s_].transpose(1, 0, 2).reshape(n_e, -1) for s_ in order], axis=1)
        gate = jnp.concatenate([gts[s_].transpose(1, 0, 2).reshape(n_e, -1) for s_ in order], axis=1)
        n_rows = flat.shape[1]
        pad = (-n_rows) % 16
        if pad:
            flat = jnp.pad(flat, ((0, 0), (0, pad)))
            gate = jnp.pad(gate, ((0, 0), (0, pad)))
        h2f = h2.reshape(bsz * l, d)
        gathered = h2f[flat.reshape(-1)].reshape(n_e, n_rows + pad, d)
        ys = _expert_ffn(gathered, gate[:, :, None], w_e_gate, w_e_up, w_e_down, layer)
        arow = jnp.concatenate(arows, axis=1)
        start = jnp.concatenate(starts, axis=1)
        nbt = ROW_TILE // LANES
        start_t = start[:, ::nbt]
        cnt_t = jnp.concatenate(cnts, axis=1).reshape(bsz, -1, nbt, n_e).sum(axis=2)
        need = start_t % YS_ALIGN + cnt_t
        npass = jnp.max(jnp.where(cnt_t > 0, -(-need // COMBINE_ROWS), 0), axis=-1).astype(jnp.int32)
        comb = (start.reshape(bsz, -1), npass, arow, ys)
    return _final(xs, comb, modtabs[depth - 1], g_final, lc)
```

```python
import functools

import numpy as np
import jax
import jax.numpy as jnp
from jax import lax
from jax.experimental import pallas as pl
from jax.experimental.pallas import tpu as pltpu

F32 = jnp.float32
BF16 = jnp.bfloat16

EPS = 1e-6
GRID_W = 64
ROPE_BASE = 10000.0
N_EXPERTS = 16
CAPACITY_FACTOR = 2
HEAD_DIM = 128
LANES = 128
SCAN_CHUNK = 128
ROW_TILE = 256
VMEM_LIMIT = 56 * 1024 * 1024

P_HQ, P_HI, P_HGT, P_RQ, P_RK, P_RV, P_RGT = range(7)

N_LEVELS = int(np.log2(SCAN_CHUNK))
SMALL_HALVES = (4, 2)
W_ROWS = (1 + len(SMALL_HALVES)) * SCAN_CHUNK
LOG2E = float(np.log2(np.e))
SCAN_INTERLEAVE = 4
SCAN_SUB = 2


def _cparams(sem, **kw):
    return pltpu.CompilerParams(dimension_semantics=sem, vmem_limit_bytes=VMEM_LIMIT, **kw)


def _nt_dot(a, b):
    return lax.dot_general(a, b, (((1,), (1,)), ((), ())), preferred_element_type=F32)


def _tn_dot(a, b):
    return lax.dot_general(a, b, (((0,), (0,)), ((), ())), preferred_element_type=F32)


def _sigmoid(x):
    return 1.0 / (1.0 + jnp.exp(-x))


def _silu(x):
    return x * _sigmoid(x)


def _adaln_kernel(cond_ref, w_ref, b_ref, o_ref):
    s = _silu(cond_ref[...])
    o_ref[0] = jnp.dot(s.astype(BF16), w_ref[0].astype(BF16), preferred_element_type=F32) + b_ref[0]


def _adaln(cond8, w_ada, b_ada):
    depth, d, n6 = w_ada.shape
    tn = n6 // 4
    return pl.pallas_call(
        _adaln_kernel,
        grid=(depth, n6 // tn),
        in_specs=[pl.BlockSpec((8, d), lambda l, j: (0, 0)),
                  pl.BlockSpec((1, d, tn), lambda l, j: (l, 0, j)),
                  pl.BlockSpec((1, 1, tn), lambda l, j: (l, 0, j))],
        out_specs=pl.BlockSpec((1, 8, tn), lambda l, j: (l, 0, j)),
        out_shape=jax.ShapeDtypeStruct((depth, 8, n6), F32),
        compiler_params=_cparams(("arbitrary", "arbitrary")),
        name="adaln",
    )(cond8, w_ada, b_ada.reshape(depth, 1, n6))


def _rms(x):
    return x * lax.rsqrt(jnp.mean(x * x, axis=-1, keepdims=True) + EPS)


COMBINE_ROWS = 64
YS_ALIGN = 16


def _combine_tile(first_tile, start_ref, npass_ref, arow_ref, ys_ref, expand_ref, lanemod_ref, buf_ref, sem):
    n_e = arow_ref.shape[2]
    n_blk = arow_ref.shape[1]
    tm = n_blk * LANES
    wn = COMBINE_ROWS
    n_rows = ys_ref.shape[1]
    d = ys_ref.shape[2]
    b, t, n_t = pl.program_id(0), pl.program_id(1), pl.num_programs(1)
    step = b * n_t + t
    n_steps = pl.num_programs(0) * n_t
    slot = step % 2
    wrap = t + 1 == n_t
    b_next = jnp.where(wrap, b + 1, b)
    t_next = jnp.where(wrap, 0, t + 1)

    def window(bb, tt, e, p):
        begin = (start_ref[bb, (tt + first_tile) * n_blk * n_e + e] // YS_ALIGN) * YS_ALIGN
        src = jnp.minimum(begin + p * wn, n_rows - wn)
        return begin, pl.multiple_of(src, YS_ALIGN)

    def copy(to_slot, src, e):
        return pltpu.make_async_copy(ys_ref.at[e, pl.ds(src, wn), :],
                                     buf_ref.at[to_slot, pl.ds(e * wn, wn), :], sem.at[to_slot, e])

    @pl.when(step == 0)
    def _():
        for e in range(n_e):
            copy(slot, window(b, t, e, 0)[1], e).start()

    @pl.when(step + 1 < n_steps)
    def _():
        for e in range(n_e):
            copy(1 - slot, window(b_next, t_next, e, 0)[1], e).start()

    arow = jnp.concatenate([arow_ref[0, i] for i in range(n_blk)], axis=1)
    erow = lax.broadcasted_iota(jnp.int32, (n_e, tm), 0)

    def per_expert(values):
        out = jnp.zeros((n_e, tm), F32)
        for e in range(n_e):
            out = jnp.where(erow == e, values[e].astype(F32), out)
        return out

    rel0 = arow - per_expert([window(b, t, e, 0)[0] for e in range(n_e)])

    def place(p, srcs, acc):
        lo = jnp.asarray(p * wn, F32)
        mine = (arow >= 0.0) & (rel0 >= lo) & (rel0 < lo + wn)
        rel = jnp.where(mine, arow - per_expert(srcs), 255.0).astype(BF16)
        spread = _tn_dot(rel, expand_ref[...])
        onehot = jnp.where(spread == lanemod_ref[...], 1.0, 0.0).astype(BF16)
        for e in range(n_e):
            copy(slot, srcs[e], e).wait()
        return acc + jnp.dot(onehot, buf_ref[slot], preferred_element_type=F32)

    acc = place(0, [window(b, t, e, 0)[1] for e in range(n_e)], jnp.zeros((tm, d), F32))

    def extra_pass(p, acc):
        srcs = [window(b, t, e, p)[1] for e in range(n_e)]
        for e in range(n_e):
            copy(slot, srcs[e], e).start()
        return place(p, srcs, acc)

    return lax.fori_loop(1, npass_ref[b, t + first_tile], extra_pass, acc)


def _inproj_kernel(has_moe, *refs):
    if has_moe:
        (start_ref, npass_ref, x_ref, arow_ref, ys_ref, expand_ref, lanemod_ref, gt2_ref, mod_ref, g_ref, w_ref,
         xo_ref, pf_ref, p_ref, buf_ref, sem) = refs
        moe = _combine_tile(0, start_ref, npass_ref, arow_ref, ys_ref, expand_ref, lanemod_ref, buf_ref, sem)
        x = x_ref[0] + gt2_ref[0][5:6] * moe
        xo_ref[0] = x
    else:
        x_ref, mod_ref, g_ref, w_ref, pf_ref, p_ref = refs
        x = x_ref[0]
    mod = mod_ref[0]
    h = (_rms(x) * g_ref[...]) * (1.0 + mod[1:2]) + mod[0:1]
    res = jnp.dot(h.astype(BF16), w_ref[0], preferred_element_type=F32)
    n_fg, gw = pf_ref.shape[1], pf_ref.shape[3]
    for g in range(n_fg):
        pf_ref[0, g] = res[:, g * gw:(g + 1) * gw]
    for g in range(p_ref.shape[1]):
        p_ref[0, g] = res[:, (n_fg + g) * gw:(n_fg + g + 1) * gw].astype(BF16)


def _seg_spec(d, ctx_tiles):
    return pl.BlockSpec((1, 6, d), lambda b, t, *_: (2 * b + jnp.where(t < ctx_tiles, 0, 1), 0, 0))


def _combine_consts(n_e):
    wn = COMBINE_ROWS
    expand = np.zeros((n_e, n_e * wn), np.float32)
    for e in range(n_e):
        expand[e, e * wn:(e + 1) * wn] = 1.0
    lanemod = (np.arange(n_e * wn) % wn).astype(np.float32)[None, :]
    return jnp.asarray(expand, BF16), jnp.asarray(lanemod, F32)


def _combine_specs(comb, tm, first_tile):
    start, npass, arow, ys = comb
    n_e = arow.shape[2]
    expand, lanemod = _combine_consts(n_e)
    k = n_e * COMBINE_ROWS
    specs = [pl.BlockSpec((1, tm // LANES, n_e, LANES), lambda b, t, *_: (b, t + first_tile, 0, 0)),
             pl.BlockSpec(memory_space=pl.ANY),
             pl.BlockSpec((n_e, k), lambda b, t, *_: (0, 0)),
             pl.BlockSpec((1, k), lambda b, t, *_: (0, 0))]
    scratch = [pltpu.VMEM((2, k, ys.shape[2]), BF16), pltpu.SemaphoreType.DMA((2, n_e))]
    return specs, [arow, ys, expand, lanemod], scratch


def _inproj(x, comb, modtab_prev, modtab, g, w_bf16, layer, ctx_len, n_f32, gw):
    b, l, d = x.shape
    n = w_bf16.shape[2]
    tm = ROW_TILE
    ctx_tiles = ctx_len // tm
    row = pl.BlockSpec((1, tm, d), lambda b, t, *_: (b, t, 0))
    has_moe = comb is not None
    in_specs = [row]
    args = [x]
    scratch = []
    prefetch = []
    if has_moe:
        specs, cargs, scratch = _combine_specs(comb, tm, 0)
        in_specs += specs + [_seg_spec(d, ctx_tiles)]
        args += cargs + [modtab_prev]
        prefetch = [comb[0], comb[1]]
    in_specs += [_seg_spec(d, ctx_tiles),
                 pl.BlockSpec((1, d), lambda b, t, *_: (0, 0)),
                 pl.BlockSpec((1, d, n), lambda b, t, *_: (layer, 0, 0))]
    args += [modtab, g.reshape(1, d), w_bf16]
    n_fg, n_g = n_f32 // gw, (n - n_f32) // gw
    out_specs = [pl.BlockSpec((1, n_fg, tm, gw), lambda b, t, *_: (b, 0, t, 0)),
                 pl.BlockSpec((1, n_g, tm, gw), lambda b, t, *_: (b, 0, t, 0))]
    out_shape = [jax.ShapeDtypeStruct((b, n_fg, l, gw), F32), jax.ShapeDtypeStruct((b, n_g, l, gw), BF16)]
    if has_moe:
        out_specs = [row] + out_specs
        out_shape = [jax.ShapeDtypeStruct((b, l, d), F32)] + out_shape
    res = pl.pallas_call(
        functools.partial(_inproj_kernel, has_moe),
        grid_spec=pltpu.PrefetchScalarGridSpec(
            num_scalar_prefetch=len(prefetch), grid=(b, l // tm),
            in_specs=in_specs, out_specs=out_specs, scratch_shapes=scratch),
        out_shape=out_shape,
        compiler_params=_cparams(("arbitrary", "arbitrary")),
        name="inproj",
    )(*prefetch, *args)
    if has_moe:
        return res[0], res[1], res[2]
    return x, res[0], res[1]


def _bwd_chunk(j, ctx_chunks, n_chunks):
    return jnp.where(j < ctx_chunks, ctx_chunks - 1 - j, n_chunks - 1 + ctx_chunks - j)


def _sub_rows(sub):
    c = SCAN_CHUNK
    return (slice(sub * c, (sub + 1) * c), slice((SCAN_SUB - 1 - sub) * c, (SCAN_SUB - sub) * c))


def _decay_tables():
    c = SCAN_CHUNK
    w = np.zeros((W_ROWS, c), np.float32)
    masks = np.zeros((N_LEVELS + 1, c, c), np.float32)
    for lev in range(N_LEVELS):
        m = c >> (lev + 1)
        for t in range(c):
            g0 = (t // (2 * m)) * 2 * m
            bnd = g0 + m - 1
            if t > bnd:
                masks[lev, t, g0:bnd + 1] = 1.0
            if m in SMALL_HALVES:
                r0 = (1 + SMALL_HALVES.index(m)) * c
                if t > bnd:
                    w[r0 + t, bnd + 1:t + 1] = 1.0
                else:
                    w[r0 + t, t + 1:bnd + 1] = 1.0
    masks[N_LEVELS] = np.eye(c, dtype=np.float32)
    for t in range(c):
        w[t, :t + 1] = 1.0
    w_b = w.reshape(-1, c, c)[:, ::-1, ::-1].reshape(-1, c)
    masks_b = masks[:, ::-1, ::-1]
    return (jnp.asarray(np.stack([w, w_b]), BF16), jnp.asarray(np.stack([masks, masks_b]), F32))


def _hgrn_kernel(n_heads, qf_ref, ff_ref, if_ref, qb_ref, fb_ref, ib_ref, lb_ref, w_ref, mask_ref,
                 of_ref, ob_ref, st_ref, a_ref):
    c = SCAN_CHUNK

    @pl.when(pl.program_id(1) == 0)
    def _():
        st_ref[...] = jnp.zeros_like(st_ref)

    dirs = ((qf_ref, ff_ref, if_ref, of_ref), (qb_ref, fb_ref, ib_ref, ob_ref))

    def gates(d, h, rows):
        q_ref, f_ref, i_ref, _ = dirs[d]
        sl = slice(h * HEAD_DIM, (h + 1) * HEAD_DIM)
        qb16 = q_ref[0, 0, rows, sl]
        xf = f_ref[0, 0, rows, sl]
        b_ = lb_ref[1:2, sl] + (jnp.minimum(xf, 0.0) - jnp.log(1.0 + jnp.exp(-jnp.abs(xf))))
        a_ = lb_ref[0:1, sl]
        lf2 = (jnp.maximum(a_, b_) + jnp.log(1.0 + jnp.exp(-jnp.abs(a_ - b_)))) * LOG2E
        f = jnp.exp2(lf2)
        k = 1.0 - f
        hi = lf2.astype(BF16)
        lo = (lf2 - hi.astype(F32)).astype(BF16)
        ex2 = jnp.dot(w_ref[d], jnp.concatenate([hi, lo], axis=1), preferred_element_type=F32)
        ex = ex2[:, :HEAD_DIM] + ex2[:, HEAD_DIM:]
        cum = ex[:c]
        a_ref[d, h] = cum
        return dict(d=d, h=h, sl=sl, rows=rows, qb16=qb16, q=qb16.astype(F32), v=i_ref[0, 0, rows, sl], f=f, k=k,
                    ex=ex, cum=cum)

    def diagonal(s):
        s["kb16"] = s["k"].astype(BF16)
        s["scores"] = _nt_dot(s["qb16"], s["kb16"]) * mask_ref[s["d"], N_LEVELS]

    def level(s, lev):
        d, h, q, k, ex, cum = s["d"], s["h"], s["q"], s["k"], s["ex"], s["cum"]
        m = c >> (lev + 1)
        if m == 1:
            lhs, rhs = (q * s["f"]).astype(BF16), s["kb16"]
        else:
            if m in SMALL_HALVES:
                r0 = (1 + SMALL_HALVES.index(m)) * c
                e = jnp.exp2(ex[r0:r0 + c])
            else:
                parts = []
                for g0 in range(0, c, 2 * m):
                    mid = g0 + m - 1 + d
                    parts.append(cum[g0:g0 + 2 * m] - a_ref[d, h, mid:mid + 1, :])
                e = jnp.exp2(-jnp.abs(jnp.concatenate(parts, axis=0) if len(parts) > 1 else parts[0]))
            lhs, rhs = (q * e).astype(BF16), (k * e).astype(BF16)
        s["scores"] = s["scores"] + _nt_dot(lhs, rhs) * mask_ref[d, lev]

    def carry(s):
        d, h, q, k, v, cum = s["d"], s["h"], s["q"], s["k"], s["v"], s["cum"]
        edge = (c - 1) * (1 - d)
        tot = a_ref[d, h, edge:edge + 1, :]
        st = st_ref[d, h]
        out = jnp.dot(s["scores"].astype(BF16), v, preferred_element_type=F32)
        out = out + _nt_dot((q * jnp.exp2(cum)).astype(BF16), st.astype(BF16))
        dirs[d][3][0, s["rows"], s["sl"]] = out.astype(BF16)
        kdec = (k * jnp.exp2(tot - cum)).astype(BF16)
        st_ref[d, h] = st * jnp.exp2(tot) + _tn_dot(v, kdec)

    units = [(d, h) for d in range(2) for h in range(n_heads)]
    for sub in range(SCAN_SUB):
        rows = _sub_rows(sub)
        for g in range(0, len(units), SCAN_INTERLEAVE):
            states = [gates(d, h, rows[d]) for d, h in units[g:g + SCAN_INTERLEAVE]]
            for s_ in states:
                diagonal(s_)
                for lev in range(N_LEVELS):
                    level(s_, lev)
            for s_ in states:
                carry(s_)


def _hgrn_scan(pf, p, lb_tab, w_tab, mask_tab, ctx_len, width):
    b, _, l, _ = p.shape
    c = SCAN_CHUNK
    rows = SCAN_SUB * c
    n_chunks = l // rows
    ctx_chunks = ctx_len // rows
    n_heads = width // HEAD_DIM

    def fwd(g):
        return pl.BlockSpec((1, 1, rows, width), lambda b, j: (b, g, j, 0))

    def bwd(g):
        return pl.BlockSpec((1, 1, rows, width), lambda b, j: (b, g, _bwd_chunk(j, ctx_chunks, n_chunks), 0))

    out_shape = jax.ShapeDtypeStruct((b, l, width), BF16)
    return pl.pallas_call(
        functools.partial(_hgrn_kernel, n_heads),
        grid=(b, n_chunks),
        in_specs=[fwd(P_HQ), fwd(0), fwd(P_HI), bwd(P_HQ), bwd(1), bwd(P_HI),
                  pl.BlockSpec((2, width), lambda b, j: (0, 0)),
                  pl.BlockSpec(w_tab.shape, lambda b, j: (0, 0, 0)),
                  pl.BlockSpec(mask_tab.shape, lambda b, j: (0, 0, 0, 0))],
        out_specs=[pl.BlockSpec((1, rows, width), lambda b, j: (b, j, 0)),
                   pl.BlockSpec((1, rows, width), lambda b, j: (b, _bwd_chunk(j, ctx_chunks, n_chunks), 0))],
        out_shape=[out_shape, out_shape],
        scratch_shapes=[pltpu.VMEM((2, n_heads, HEAD_DIM, HEAD_DIM), F32),
                        pltpu.VMEM((2, n_heads, c, HEAD_DIM), F32)],
        compiler_params=_cparams(("arbitrary", "arbitrary")),
        name="hgrn_scan",
    )(p, pf, p, p, pf, p, lb_tab, w_tab, mask_tab)


def _rope(z, cos_rep, sin_signed, even_lane):
    partner = jnp.where(even_lane, pltpu.roll(z, LANES - 1, 1), pltpu.roll(z, 1, 1))
    return z * cos_rep + partner * sin_signed


def _ret_kernel(n_heads, lg_ref, qf_ref, kf_ref, vf_ref, cf_ref, sf_ref,
                qb_ref, kb_ref, vb_ref, cb_ref, sb_ref, of_ref, ob_ref, st_ref, dm_ref, in_ref, tl_ref):
    c = SCAN_CHUNK
    k_scale = HEAD_DIM ** -0.5

    @pl.when(pl.program_id(1) == 0)
    def _():
        st_ref[...] = jnp.zeros_like(st_ref)
        ti = lax.broadcasted_iota(jnp.int32, (c, c), 0)
        si = lax.broadcasted_iota(jnp.int32, (c, c), 1)
        rowf = lax.broadcasted_iota(jnp.int32, (c, HEAD_DIM), 0).astype(F32)
        for d in range(2):
            rel = (ti - si) if d == 0 else (si - ti)
            relf = jnp.maximum(rel, 0).astype(F32)
            for h in range(n_heads):
                lg = lg_ref[d, h]
                dm_ref[d, h] = jnp.where(rel >= 0, jnp.exp(relf * lg), 0.0)
                if d == 0:
                    in_ref[d, h] = jnp.exp((rowf + 1.0) * lg)
                    tl_ref[d, h] = jnp.exp((c - 1.0 - rowf) * lg)
                else:
                    in_ref[d, h] = jnp.exp((c - rowf) * lg)
                    tl_ref[d, h] = jnp.exp(rowf * lg)

    even_lane = (lax.broadcasted_iota(jnp.int32, (c, HEAD_DIM), 1) & 1) == 0
    dirs = ((qf_ref, kf_ref, vf_ref, cf_ref, sf_ref, of_ref), (qb_ref, kb_ref, vb_ref, cb_ref, sb_ref, ob_ref))
    def pairs(d, h, rows):
        q_ref, k_ref, v_ref, c_ref, s_ref, _ = dirs[d]
        sl = slice(h * HEAD_DIM, (h + 1) * HEAD_DIM)
        q = _rope(q_ref[0, 0, rows, sl].astype(F32), c_ref[rows, :], s_ref[rows, :], even_lane)
        k = _rope(k_ref[0, 0, rows, sl].astype(F32), c_ref[rows, :], s_ref[rows, :], even_lane) * k_scale
        qb16 = q.astype(BF16)
        scores = _nt_dot(qb16, k.astype(BF16)) * dm_ref[d, h]
        return dict(d=d, h=h, sl=sl, rows=rows, qb16=qb16, k=k, v=v_ref[0, 0, rows, sl], scores=scores)

    def carry(s):
        d, h, v = s["d"], s["h"], s["v"]
        st = st_ref[d, h]
        out = jnp.dot(s["scores"].astype(BF16), v, preferred_element_type=F32)
        out = out + _nt_dot(s["qb16"], st.astype(BF16)) * in_ref[d, h]
        dirs[d][5][0, s["rows"], s["sl"]] = out.astype(BF16)
        edge = (c - 1) * (1 - d)
        st_ref[d, h] = (st * in_ref[d, h, edge:edge + 1, :]
                        + _tn_dot(v, (s["k"] * tl_ref[d, h]).astype(BF16)))

    units = [(d, h) for d in range(2) for h in range(n_heads)]
    for sub in range(SCAN_SUB):
        rows = _sub_rows(sub)
        for g in range(0, len(units), SCAN_INTERLEAVE):
            states = [pairs(d, h, rows[d]) for d, h in units[g:g + SCAN_INTERLEAVE]]
            for s_ in states:
                carry(s_)


def _ret_scan(p, lg, cos_rep, sin_signed, ctx_len, width):
    b, _, l, _ = p.shape
    c = SCAN_CHUNK
    rows = SCAN_SUB * c
    n_chunks = l // rows
    ctx_chunks = ctx_len // rows
    n_heads = width // HEAD_DIM

    def fwd(g):
        return pl.BlockSpec((1, 1, rows, width), lambda b, j, lg: (b, g, j, 0))

    def bwd(g):
        return pl.BlockSpec((1, 1, rows, width), lambda b, j, lg: (b, g, _bwd_chunk(j, ctx_chunks, n_chunks), 0))

    tab_f = pl.BlockSpec((rows, HEAD_DIM), lambda b, j, lg: (j, 0))
    tab_b = pl.BlockSpec((rows, HEAD_DIM), lambda b, j, lg: (_bwd_chunk(j, ctx_chunks, n_chunks), 0))
    out_shape = jax.ShapeDtypeStruct((b, l, width), BF16)
    return pl.pallas_call(
        functools.partial(_ret_kernel, n_heads),
        grid_spec=pltpu.PrefetchScalarGridSpec(
            num_scalar_prefetch=1,
            grid=(b, n_chunks),
            in_specs=[fwd(P_RQ), fwd(P_RK), fwd(P_RV), tab_f, tab_f,
                      bwd(P_RQ), bwd(P_RK), bwd(P_RV), tab_b, tab_b],
            out_specs=[pl.BlockSpec((1, rows, width), lambda b, j, lg: (b, j, 0)),
                       pl.BlockSpec((1, rows, width),
                                    lambda b, j, lg: (b, _bwd_chunk(j, ctx_chunks, n_chunks), 0))],
            scratch_shapes=[pltpu.VMEM((2, n_heads, HEAD_DIM, HEAD_DIM), F32),
                            pltpu.VMEM((2, n_heads, c, c), F32),
                            pltpu.VMEM((2, n_heads, c, HEAD_DIM), F32),
                            pltpu.VMEM((2, n_heads, c, HEAD_DIM), F32)]),
        out_shape=[out_shape, out_shape],
        compiler_params=_cparams(("arbitrary", "arbitrary")),
        name="ret_scan",
    )(lg, p, p, p, cos_rep, sin_signed, p, p, p, cos_rep, sin_signed)


def _outproj_kernel(n_hg, n_ret, x_ref, hf_ref, hb_ref, rf_ref, rb_ref, hg_ref, rg_ref, mod_ref,
                    on_ref, wo_ref, g2_ref, wrh_ref, wrl_ref, xo_ref, h2_ref, aff_ref):
    mod = mod_ref[0]

    def mixed(rows):
        o_hg = hf_ref[0, rows, :].astype(F32) + hb_ref[0, rows, :].astype(F32)
        o_rt = rf_ref[0, rows, :].astype(F32) + rb_ref[0, rows, :].astype(F32)
        parts = []
        for h in range(n_hg):
            sl = slice(h * HEAD_DIM, (h + 1) * HEAD_DIM)
            parts.append((_rms(o_hg[:, sl]) * on_ref[...]) * _silu(hg_ref[0, 0, rows, sl].astype(F32)))
        for h in range(n_ret):
            sl = slice(h * HEAD_DIM, (h + 1) * HEAD_DIM)
            parts.append(_rms(o_rt[:, sl]) * _silu(rg_ref[0, 0, rows, sl].astype(F32)))
        return jnp.concatenate(parts, axis=1).astype(BF16)

    def residual(rows, mix):
        x = x_ref[0, rows, :] + mod[2:3] * jnp.dot(mix, wo_ref[0], preferred_element_type=F32)
        xo_ref[0, rows, :] = x
        h2 = (_rms(x) * g2_ref[...]) * (1.0 + mod[4:5]) + mod[3:4]
        h2_ref[0, rows, :] = h2.astype(BF16)
        return h2

    def route(i, h2):
        hh = h2.astype(BF16)
        hl = (h2 - hh.astype(F32)).astype(BF16)
        logits = _nt_dot(wrh_ref[...], hh) + _nt_dot(wrh_ref[...], hl) + _nt_dot(wrl_ref[...], hh)
        mx = jnp.max(logits, axis=0, keepdims=True)
        ex = jnp.exp(logits - mx)
        aff_ref[0, i] = ex / jnp.sum(ex, axis=0, keepdims=True)

    blocks = [slice(i * LANES, (i + 1) * LANES) for i in range(aff_ref.shape[1])]
    mixes = [mixed(rows) for rows in blocks]
    h2s = [residual(rows, mix) for rows, mix in zip(blocks, mixes)]
    for i, h2 in enumerate(h2s):
        route(i, h2)


def _outproj(x, hf, hb, rf, rb, p, modtab, onorm, wo_bf16, layer, g2, w_router, ctx_len, hg_width):
    b, l, d = x.shape
    tm = ROW_TILE
    ctx_tiles = ctx_len // tm
    n_hg = hg_width // HEAD_DIM
    ret_width = d - hg_width
    n_ret = ret_width // HEAD_DIM
    n_e = w_router.shape[1]
    wrt = w_router.T
    wrh = wrt.astype(BF16)
    wrl = (wrt - wrh.astype(F32)).astype(BF16)
    row = pl.BlockSpec((1, tm, d), lambda b, t: (b, t, 0))
    hrow = pl.BlockSpec((1, tm, hg_width), lambda b, t: (b, t, 0))
    rrow = pl.BlockSpec((1, tm, ret_width), lambda b, t: (b, t, 0))
    full = lambda shape: pl.BlockSpec(shape, lambda b, t: tuple(0 for _ in shape))
    return pl.pallas_call(
        functools.partial(_outproj_kernel, n_hg, n_ret),
        grid=(b, l // tm),
        in_specs=[row, hrow, hrow, rrow, rrow,
                  pl.BlockSpec((1, 1, tm, hg_width), lambda b, t: (b, P_HGT, t, 0)),
                  pl.BlockSpec((1, 1, tm, ret_width), lambda b, t: (b, P_RGT, t, 0)),
                  _seg_spec(d, ctx_tiles),
                  full((1, HEAD_DIM)), pl.BlockSpec((1, d, d), lambda b, t: (layer, 0, 0)),
                  full((1, d)), full((n_e, d)), full((n_e, d))],
        out_specs=[row, row, pl.BlockSpec((1, tm // LANES, n_e, LANES), lambda b, t: (b, t, 0, 0))],
        out_shape=[jax.ShapeDtypeStruct((b, l, d), F32),
                   jax.ShapeDtypeStruct((b, l, d), BF16),
                   jax.ShapeDtypeStruct((b, l // LANES, n_e, LANES), F32)],
        compiler_params=_cparams(("arbitrary", "arbitrary")),
        name="outproj",
    )(x, hf, hb, rf, rb, p, p, modtab, onorm.reshape(1, HEAD_DIM), wo_bf16, g2.reshape(1, d), wrh, wrl)


FFN_SPLIT = 2
CAST_ROWS = 64


def _ffn_kernel(layer, x_ref, gate_ref, wg_hbm, wu_hbm, wd_hbm, o_ref, sg_ref, su_ref, sd_ref,
                wg_ref, wu_ref, wd_ref, sem):
    e, t, n_e = pl.program_id(0), pl.program_id(1), pl.num_programs(0)
    pairs = ((wg_hbm, sg_ref, wg_ref), (wu_hbm, su_ref, wu_ref), (wd_hbm, sd_ref, wd_ref))

    def fetch(expert):
        return [pltpu.make_async_copy(hbm.at[layer, expert], stage, sem.at[i])
                for i, (hbm, stage, _) in enumerate(pairs)]

    @pl.when((e == 0) & (t == 0))
    def _():
        for cp in fetch(0):
            cp.start()

    @pl.when(t == 0)
    def _():
        for cp in fetch(e):
            cp.wait()
        for _, stage, w16 in pairs:
            def cast(i, carry, stage=stage, w16=w16):
                rows = pl.ds(pl.multiple_of(i * CAST_ROWS, CAST_ROWS), CAST_ROWS)
                w16[rows, :] = stage[rows, :].astype(BF16)
                return carry
            lax.fori_loop(0, stage.shape[0] // CAST_ROWS, cast, 0)

        @pl.when(e + 1 < n_e)
        def _():
            for cp in fetch(e + 1):
                cp.start()

    x = x_ref[0]
    ff = wg_ref.shape[1]
    piece = ff // FFN_SPLIT
    y = jnp.zeros((x.shape[0], wd_ref.shape[1]), F32)
    for i in range(FFN_SPLIT):
        cs = slice(i * piece, (i + 1) * piece)
        a = jnp.dot(x, wg_ref[:, cs], preferred_element_type=F32)
        u = jnp.dot(x, wu_ref[:, cs], preferred_element_type=F32)
        hid = (_silu(a) * u).astype(BF16)
        y = y + jnp.dot(hid, wd_ref[cs, :], preferred_element_type=F32)
    o_ref[0] = (y * gate_ref[0]).astype(BF16)


def _row_tile(rows, cap=512):
    best = 16
    for t in range(16, cap + 1, 16):
        if rows % t == 0:
            best = t
    return best


def _expert_ffn(xs, gates, wg, wu, wd, layer):
    n_e, rows, d = xs.shape
    ff = wg.shape[3]
    tm = _row_tile(rows)
    assert ff % FFN_SPLIT == 0 and d % CAST_ROWS == 0 and ff % CAST_ROWS == 0
    hbm = pl.BlockSpec(memory_space=pl.ANY)
    return pl.pallas_call(
        functools.partial(_ffn_kernel, layer),
        grid=(n_e, rows // tm),
        in_specs=[pl.BlockSpec((1, tm, d), lambda e, t: (e, t, 0)),
                  pl.BlockSpec((1, tm, 1), lambda e, t: (e, t, 0)),
                  hbm, hbm, hbm],
        out_specs=pl.BlockSpec((1, tm, d), lambda e, t: (e, t, 0)),
        out_shape=jax.ShapeDtypeStruct((n_e, rows, d), BF16),
        scratch_shapes=[pltpu.VMEM((d, ff), F32), pltpu.VMEM((d, ff), F32), pltpu.VMEM((ff, d), F32),
                        pltpu.VMEM((d, ff), BF16), pltpu.VMEM((d, ff), BF16), pltpu.VMEM((ff, d), BF16),
                        pltpu.SemaphoreType.DMA((3,))],
        compiler_params=_cparams(("arbitrary", "arbitrary")),
        name="expert_ffn",
    )(xs, gates, wg, wu, wd)


def _final_kernel(first_tile, start_ref, npass_ref, x_ref, arow_ref, ys_ref, expand_ref, lanemod_ref,
                  gt2_ref, g_ref, o_ref, buf_ref, sem):
    moe = _combine_tile(first_tile, start_ref, npass_ref, arow_ref, ys_ref, expand_ref, lanemod_ref, buf_ref, sem)
    x = x_ref[0] + gt2_ref[0][5:6] * moe
    o_ref[0] = _rms(x) * g_ref[...]


def _final(x, comb, modtab, g, ctx_len):
    b, l, d = x.shape
    tm = ROW_TILE
    ctx_tiles = ctx_len // tm
    n_t = (l - ctx_len) // tm
    row = pl.BlockSpec((1, tm, d), lambda b, t, *_: (b, t + ctx_tiles, 0))
    specs, cargs, scratch = _combine_specs(comb, tm, ctx_tiles)
    return pl.pallas_call(
        functools.partial(_final_kernel, ctx_tiles),
        grid_spec=pltpu.PrefetchScalarGridSpec(
            num_scalar_prefetch=2, grid=(b, n_t),
            in_specs=[row] + specs + [pl.BlockSpec((1, 6, d), lambda b, t, *_: (2 * b + 1, 0, 0)),
                                      pl.BlockSpec((1, d), lambda b, t, *_: (0, 0))],
            out_specs=pl.BlockSpec((1, tm, d), lambda b, t, *_: (b, t, 0)),
            scratch_shapes=scratch),
        out_shape=jax.ShapeDtypeStruct((b, l - ctx_len, d), F32),
        compiler_params=_cparams(("arbitrary", "arbitrary")),
        name="final_norm",
    )(comb[0], comb[1], x, *cargs, modtab, g.reshape(1, d))


def _rope_tables(rows, ctx_len):
    n_freq = HEAD_DIM // 4
    inv = ROPE_BASE ** (-jnp.arange(n_freq, dtype=F32) / n_freq)
    r = jnp.repeat(jnp.arange(rows, dtype=F32), GRID_W)
    cc = jnp.tile(jnp.arange(GRID_W, dtype=F32), rows)
    lat = jnp.concatenate([r[:, None] * inv, cc[:, None] * inv], axis=-1)
    ang = jnp.concatenate([jnp.zeros((ctx_len, 2 * n_freq), F32), lat], axis=0)
    cos = jnp.cos(ang)
    sin = jnp.sin(ang)
    cos_rep = jnp.repeat(cos, 2, axis=-1)
    sin_signed = jnp.stack([-sin, sin], axis=-1).reshape(ang.shape[0], HEAD_DIM)
    return cos_rep, sin_signed


def _select_kernel(nb, cap, aff_ref, tri_ref, pos_ref, off_ref, tot_ref, cum_ref):
    n_e = aff_ref.shape[2]
    bits = lax.bitcast_convert_type(aff_ref[0], jnp.int32)
    ones = jnp.ones((LANES, LANES), BF16)
    tri = tri_ref[...]

    def total(flags):
        return jnp.dot(jnp.sum(flags, axis=0).astype(BF16), ones, preferred_element_type=F32)

    def thr_step(i, thr):
        cand = thr | (jnp.int32(1) << (30 - i))
        cnt = total((bits >= cand[None]).astype(F32))
        return jnp.where(cnt >= cap, cand, thr)

    thr = lax.fori_loop(0, 31, thr_step, jnp.zeros((n_e, LANES), jnp.int32))

    def prefix(flags):
        f2 = flags.reshape(nb * n_e, LANES).astype(BF16)
        incl = jnp.dot(f2, tri, preferred_element_type=F32).reshape(nb, n_e, LANES)
        tot_ref[...] = jnp.dot(f2, ones, preferred_element_type=F32).reshape(nb, n_e, LANES)

        def step(k, carry):
            cum_ref[k] = carry
            return carry + tot_ref[k]

        lax.fori_loop(0, nb, step, jnp.zeros((n_e, LANES), F32))
        return incl, cum_ref[...]

    gt = (bits > thr[None]).astype(F32)
    eq = (bits == thr[None]).astype(F32)
    need = cap - total(gt)
    eq_incl, eq_off = prefix(eq)
    sel = gt + eq * ((eq_off + eq_incl - eq) < need[None]).astype(F32)
    incl, off = prefix(sel)
    pos_ref[0] = jnp.where(sel > 0.5, off + incl - 1.0, -1.0)
    off_ref[0] = off


def _compact_kernel(nb, off_ref, pos_ref, aff_ref, out_ref):
    b = pl.program_id(0)
    n_e = pos_ref.shape[2]
    out_ref[...] = jnp.zeros_like(out_ref)
    slot_row = lax.broadcasted_iota(jnp.int32, (LANES, LANES), 0).astype(F32)
    lane8 = lax.broadcasted_iota(jnp.int32, (8, LANES), 1)
    row8 = lax.broadcasted_iota(jnp.int32, (8, LANES), 0)
    lane_f = lane8.astype(F32)

    def block(k, carry):
        starts = [off_ref[b, k, e] for e in range(n_e)]
        picked = []
        for e in range(n_e):
            blk = starts[e] >> 7
            p = pos_ref[0, k, e:e + 1, :]
            g = aff_ref[0, k, e:e + 1, :]
            r = p - jnp.asarray(blk * LANES, F32)
            r = jnp.where(r >= LANES, r - LANES, r)
            onehot = jnp.where((slot_row == r) & (p >= 0.0), 1.0, 0.0).astype(BF16)
            g_hi = g.astype(BF16).astype(F32)
            g_mid = (g - g_hi).astype(BF16).astype(F32)
            g_lo = (g - g_hi - g_mid).astype(BF16).astype(F32)
            vals = jnp.where(row8 == 0, lane_f,
                             jnp.where(row8 == 1, 1.0,
                                       jnp.where(row8 == 2, g_hi,
                                                 jnp.where(row8 == 3, g_mid,
                                                           jnp.where(row8 == 4, g_lo, 0.0)))))
            picked.append(_nt_dot(vals.astype(BF16), onehot))
        for e in range(n_e):
            c = picked[e]
            blk = starts[e] >> 7
            tok = c[1:2] * jnp.asarray(k * LANES, F32) + c[0:1]
            gate = c[2:3] + c[3:4] + c[4:5]
            tile = jnp.where(row8 == 0, tok, jnp.where(row8 == 1, gate, 0.0))
            first = lane8 >= (starts[e] & (LANES - 1))
            out_ref[0, e, blk] += jnp.where(first, tile, 0.0)
            out_ref[0, e, blk + 1] += jnp.where(first, 0.0, tile)
        return carry

    lax.fori_loop(0, nb, block, 0)


def _route(aff_blocks, cap):
    bsz, nb, n_e, _ = aff_blocks.shape
    assert nb <= 256
    tri = jnp.asarray(np.triu(np.ones((LANES, LANES), np.float32)), BF16)
    blk = pl.BlockSpec((1, nb, n_e, LANES), lambda b: (b, 0, 0, 0))
    shp = jax.ShapeDtypeStruct((bsz, nb, n_e, LANES), F32)
    pos, off_rep = pl.pallas_call(
        functools.partial(_select_kernel, nb, cap),
        grid=(bsz,),
        in_specs=[blk, pl.BlockSpec((LANES, LANES), lambda b: (0, 0))],
        out_specs=[blk, blk],
        out_shape=[shp, shp],
        scratch_shapes=[pltpu.VMEM((nb, n_e, LANES), F32), pltpu.VMEM((nb, n_e, LANES), F32)],
        compiler_params=_cparams(("arbitrary",)),
        name="route_select",
    )(aff_blocks, tri)
    off = off_rep[:, :, :, 0].astype(jnp.int32)
    n_rows = cap // LANES + 2
    blk1 = pl.BlockSpec((1, nb, n_e, LANES), lambda b, off: (b, 0, 0, 0))
    lists = pl.pallas_call(
        functools.partial(_compact_kernel, nb),
        grid_spec=pltpu.PrefetchScalarGridSpec(
            num_scalar_prefetch=1,
            grid=(bsz,),
            in_specs=[blk1, blk1],
            out_specs=pl.BlockSpec((1, n_e, n_rows, 8, LANES), lambda b, off: (b, 0, 0, 0, 0))),
        out_shape=jax.ShapeDtypeStruct((bsz, n_e, n_rows, 8, LANES), F32),
        compiler_params=_cparams(("arbitrary",)),
        name="route_compact",
    )(off, pos, aff_blocks)
    tokens = lists[:, :, :, 0, :].reshape(bsz, n_e, n_rows * LANES)[:, :, :cap].astype(jnp.int32)
    gates = lists[:, :, :, 1, :].reshape(bsz, n_e, n_rows * LANES)[:, :, :cap]
    return gates, tokens, pos, off


def kernel(x, c, ctx, c_ctx, w_ada, b_ada, g_mix, g_ffn, w_in, w_out, hg_lb_logits, hg_onorm,
           ret_decay, w_router, w_e_gate, w_e_up, w_e_down, g_final):
    bsz, n, d = x.shape
    lc = ctx.shape[1]
    depth = w_ada.shape[0]
    l = lc + n
    hg_width = hg_lb_logits.shape[1]
    ret_width = d - hg_width
    n_e = w_router.shape[2]
    assert lc % ROW_TILE == 0 and n % ROW_TILE == 0
    assert lc % (SCAN_SUB * SCAN_CHUNK) == 0 and n % (SCAN_SUB * SCAN_CHUNK) == 0
    assert hg_width % HEAD_DIM == 0 and ret_width % HEAD_DIM == 0 and n % GRID_W == 0
    assert w_in.shape[2] == 5 * hg_width + 4 * ret_width and hg_width == ret_width

    cos_rep, sin_signed = _rope_tables(n // GRID_W, lc)
    w_tab, mask_tab = _decay_tables()

    gamma_cum = jnp.cumsum(jax.nn.softmax(hg_lb_logits.astype(F32), axis=0), axis=0)
    lbs = gamma_cum - gamma_cum[0:1]
    lb_tabs = jnp.stack([jnp.log(lbs), jnp.log1p(-lbs)], axis=1)
    log_gammas = -jnp.exp(ret_decay.astype(F32))

    cond8 = jnp.zeros((8, d), F32).at[:bsz].set(c).at[bsz].set(c_ctx)
    mods = _adaln(cond8, w_ada, b_ada).reshape(depth, 8, 6, d)
    modtabs = jnp.stack([jnp.broadcast_to(mods[:, bsz:bsz + 1], (depth, bsz, 6, d)), mods[:, :bsz]],
                        axis=2).reshape(depth, 2 * bsz, 6, d)

    cap_lat = CAPACITY_FACTOR * n // n_e
    cap_ctx = CAPACITY_FACTOR * lc // n_e
    boff = (jnp.arange(bsz, dtype=jnp.int32) * l)[:, None, None]

    gw = hg_width
    w_in16 = jnp.concatenate([w_in[:, :, gw:3 * gw], w_in[:, :, :gw], w_in[:, :, 3 * gw:]], axis=2).astype(BF16)
    w_out16 = w_out.astype(BF16)
    xs = jnp.concatenate([ctx, x], axis=1)
    comb = None
    for layer in range(depth):
        last = layer == depth - 1
        xs, pf, p = _inproj(xs, comb, modtabs[layer - 1] if layer else None, modtabs[layer], g_mix[layer],
                            w_in16, layer, lc, 2 * hg_width, hg_width)
        hf, hb = _hgrn_scan(pf, p, lb_tabs[layer], w_tab, mask_tab, lc, hg_width)
        rf, rb = _ret_scan(p, log_gammas[layer], cos_rep, sin_signed, lc, ret_width)
        xs, h2, aff_t = _outproj(xs, hf, hb, rf, rb, p, modtabs[layer], hg_onorm[layer],
                                 w_out16, layer, g_ffn[layer], w_router[layer], lc, hg_width)
        blocks = [lc // LANES, n // LANES]
        caps = [cap_ctx, cap_lat]
        bases = [bsz * cap_lat + jnp.arange(bsz, dtype=jnp.int32) * cap_ctx,
                 jnp.arange(bsz, dtype=jnp.int32) * cap_lat]
        affs = [aff_t[:, :blocks[0]], aff_t[:, blocks[0]:]]
        tok_off = [0, lc]
        rows, gts, arows, starts, cnts = {}, {}, [], [], []
        for seg in (0, 1):
            if seg == 0 and last:
                arows.append(jnp.full((bsz, blocks[0], n_e, LANES), -1.0, F32))
                starts.append(jnp.zeros((bsz, blocks[0], n_e), jnp.int32))
                cnts.append(jnp.zeros((bsz, blocks[0], n_e), jnp.int32))
                continue
            g_s, i_s, pos_s, off_s = _route(affs[seg], caps[seg])
            rows[seg] = i_s + tok_off[seg] + boff
            gts[seg] = g_s
            kept = pos_s >= 0.0
            arows.append(jnp.where(kept, pos_s + bases[seg][:, None, None, None].astype(F32), -1.0))
            starts.append(off_s + bases[seg][:, None, None])
            cnts.append(jnp.sum(kept, axis=-1).astype(jnp.int32))
        order = [s_ for s_ in (1, 0) if s_ in rows]
        flat = jnp.concatenate([rows[s_].transpose(1, 0, 2).reshape(n_e, -1) for s_ in order], axis=1)
        gate = jnp.concatenate([gts[s_].transpose(1, 0, 2).reshape(n_e, -1) for s_ in order], axis=1)
        n_rows = flat.shape[1]
        pad = (-n_rows) % 16
        if pad:
            flat = jnp.pad(flat, ((0, 0), (0, pad)))
            gate = jnp.pad(gate, ((0, 0), (0, pad)))
        h2f = h2.reshape(bsz * l, d)
        gathered = h2f[flat.reshape(-1)].reshape(n_e, n_rows + pad, d)
        ys = _expert_ffn(gathered, gate[:, :, None], w_e_gate, w_e_up, w_e_down, layer)
        arow = jnp.concatenate(arows, axis=1)
        start = jnp.concatenate(starts, axis=1)
        nbt = ROW_TILE // LANES
        start_t = start[:, ::nbt]
        cnt_t = jnp.concatenate(cnts, axis=1).reshape(bsz, -1, nbt, n_e).sum(axis=2)
        need = start_t % YS_ALIGN + cnt_t
        npass = jnp.max(jnp.where(cnt_t > 0, -(-need // COMBINE_ROWS), 0), axis=-1).astype(jnp.int32)
        comb = (start.reshape(bsz, -1), npass, arow, ys)
    return _final(xs, comb, modtabs[depth - 1], g_final, lc)
```

```python
import functools

import numpy as np
import jax
import jax.numpy as jnp
from jax import lax
from jax.experimental import pallas as pl
from jax.experimental.pallas import tpu as pltpu

F32 = jnp.float32
BF16 = jnp.bfloat16

EPS = 1e-6
GRID_W = 64
ROPE_BASE = 10000.0
N_EXPERTS = 16
CAPACITY_FACTOR = 2
HEAD_DIM = 128
LANES = 128
SCAN_CHUNK = 128
ROW_TILE = 256
VMEM_LIMIT = 56 * 1024 * 1024

P_HQ, P_HI, P_HGT, P_RQ, P_RK, P_RV, P_RGT = range(7)

N_LEVELS = int(np.log2(SCAN_CHUNK))
SMALL_HALVES = (4, 2)
N_BIG_LEVELS = N_LEVELS - len(SMALL_HALVES) - 1
W_ROWS = (1 + len(SMALL_HALVES)) * SCAN_CHUNK
LOG2E = float(np.log2(np.e))
SCAN_INTERLEAVE = 4
SCAN_SUB = 2


def _cparams(sem, **kw):
    return pltpu.CompilerParams(dimension_semantics=sem, vmem_limit_bytes=VMEM_LIMIT, **kw)


def _nt_dot(a, b):
    return lax.dot_general(a, b, (((1,), (1,)), ((), ())), preferred_element_type=F32)


def _tn_dot(a, b):
    return lax.dot_general(a, b, (((0,), (0,)), ((), ())), preferred_element_type=F32)


def _sigmoid(x):
    return 1.0 / (1.0 + jnp.exp(-x))


def _silu(x):
    return x * _sigmoid(x)


def _adaln_kernel(cond_ref, w_ref, b_ref, o_ref):
    s = _silu(cond_ref[...])
    o_ref[0] = jnp.dot(s.astype(BF16), w_ref[0].astype(BF16), preferred_element_type=F32) + b_ref[0]


def _adaln(cond8, w_ada, b_ada):
    depth, d, n6 = w_ada.shape
    tn = n6 // 4
    return pl.pallas_call(
        _adaln_kernel,
        grid=(depth, n6 // tn),
        in_specs=[pl.BlockSpec((8, d), lambda l, j: (0, 0)),
                  pl.BlockSpec((1, d, tn), lambda l, j: (l, 0, j)),
                  pl.BlockSpec((1, 1, tn), lambda l, j: (l, 0, j))],
        out_specs=pl.BlockSpec((1, 8, tn), lambda l, j: (l, 0, j)),
        out_shape=jax.ShapeDtypeStruct((depth, 8, n6), F32),
        compiler_params=_cparams(("arbitrary", "arbitrary")),
        name="adaln",
    )(cond8, w_ada, b_ada.reshape(depth, 1, n6))


def _rms(x):
    return x * lax.rsqrt(jnp.mean(x * x, axis=-1, keepdims=True) + EPS)


COMBINE_ROWS = 64
YS_ALIGN = 16


def _combine_tile(first_tile, start_ref, npass_ref, arow_ref, ys_ref, expand_ref, lanemod_ref, buf_ref, sem):
    n_e = arow_ref.shape[2]
    n_blk = arow_ref.shape[1]
    tm = n_blk * LANES
    wn = COMBINE_ROWS
    n_rows = ys_ref.shape[1]
    d = ys_ref.shape[2]
    b, t, n_t = pl.program_id(0), pl.program_id(1), pl.num_programs(1)
    step = b * n_t + t
    n_steps = pl.num_programs(0) * n_t
    slot = step % 2
    wrap = t + 1 == n_t
    b_next = jnp.where(wrap, b + 1, b)
    t_next = jnp.where(wrap, 0, t + 1)

    def window(bb, tt, e, p):
        begin = (start_ref[bb, (tt + first_tile) * n_blk * n_e + e] // YS_ALIGN) * YS_ALIGN
        src = jnp.minimum(begin + p * wn, n_rows - wn)
        return begin, pl.multiple_of(src, YS_ALIGN)

    def copy(to_slot, src, e):
        return pltpu.make_async_copy(ys_ref.at[e, pl.ds(src, wn), :],
                                     buf_ref.at[to_slot, pl.ds(e * wn, wn), :], sem.at[to_slot, e])

    @pl.when(step == 0)
    def _():
        for e in range(n_e):
            copy(slot, window(b, t, e, 0)[1], e).start()

    @pl.when(step + 1 < n_steps)
    def _():
        for e in range(n_e):
            copy(1 - slot, window(b_next, t_next, e, 0)[1], e).start()

    arow = jnp.concatenate([arow_ref[0, i] for i in range(n_blk)], axis=1)
    erow = lax.broadcasted_iota(jnp.int32, (n_e, tm), 0)

    def per_expert(values):
        out = jnp.zeros((n_e, tm), F32)
        for e in range(n_e):
            out = jnp.where(erow == e, values[e].astype(F32), out)
        return out

    rel0 = arow - per_expert([window(b, t, e, 0)[0] for e in range(n_e)])

    def place(p, srcs, acc):
        lo = jnp.asarray(p * wn, F32)
        mine = (arow >= 0.0) & (rel0 >= lo) & (rel0 < lo + wn)
        rel = jnp.where(mine, arow - per_expert(srcs), 255.0).astype(BF16)
        spread = _tn_dot(rel, expand_ref[...])
        onehot = jnp.where(spread == lanemod_ref[...], 1.0, 0.0).astype(BF16)
        for e in range(n_e):
            copy(slot, srcs[e], e).wait()
        return acc + jnp.dot(onehot, buf_ref[slot], preferred_element_type=F32)

    acc = place(0, [window(b, t, e, 0)[1] for e in range(n_e)], jnp.zeros((tm, d), F32))

    def extra_pass(p, acc):
        srcs = [window(b, t, e, p)[1] for e in range(n_e)]
        for e in range(n_e):
            copy(slot, srcs[e], e).start()
        return place(p, srcs, acc)

    return lax.fori_loop(1, npass_ref[b, t + first_tile], extra_pass, acc)


def _inproj_kernel(has_moe, *refs):
    if has_moe:
        (start_ref, npass_ref, x_ref, arow_ref, ys_ref, expand_ref, lanemod_ref, gt2_ref, mod_ref, g_ref, w_ref,
         xo_ref, pf_ref, p_ref, buf_ref, sem) = refs
        moe = _combine_tile(0, start_ref, npass_ref, arow_ref, ys_ref, expand_ref, lanemod_ref, buf_ref, sem)
        x = x_ref[0] + gt2_ref[0][5:6] * moe
        xo_ref[0] = x
    else:
        x_ref, mod_ref, g_ref, w_ref, pf_ref, p_ref = refs
        x = x_ref[0]
    mod = mod_ref[0]
    h = (_rms(x) * g_ref[...]) * (1.0 + mod[1:2]) + mod[0:1]
    res = jnp.dot(h.astype(BF16), w_ref[0], preferred_element_type=F32)
    n_fg, gw = pf_ref.shape[1], pf_ref.shape[3]
    for g in range(n_fg):
        pf_ref[0, g] = res[:, g * gw:(g + 1) * gw]
    for g in range(p_ref.shape[1]):
        p_ref[0, g] = res[:, (n_fg + g) * gw:(n_fg + g + 1) * gw].astype(BF16)


def _seg_spec(d, ctx_tiles):
    return pl.BlockSpec((1, 6, d), lambda b, t, *_: (2 * b + jnp.where(t < ctx_tiles, 0, 1), 0, 0))


def _combine_consts(n_e):
    wn = COMBINE_ROWS
    expand = np.zeros((n_e, n_e * wn), np.float32)
    for e in range(n_e):
        expand[e, e * wn:(e + 1) * wn] = 1.0
    lanemod = (np.arange(n_e * wn) % wn).astype(np.float32)[None, :]
    return jnp.asarray(expand, BF16), jnp.asarray(lanemod, F32)


def _combine_specs(comb, tm, first_tile):
    start, npass, arow, ys = comb
    n_e = arow.shape[2]
    expand, lanemod = _combine_consts(n_e)
    k = n_e * COMBINE_ROWS
    specs = [pl.BlockSpec((1, tm // LANES, n_e, LANES), lambda b, t, *_: (b, t + first_tile, 0, 0)),
             pl.BlockSpec(memory_space=pl.ANY),
             pl.BlockSpec((n_e, k), lambda b, t, *_: (0, 0)),
             pl.BlockSpec((1, k), lambda b, t, *_: (0, 0))]
    scratch = [pltpu.VMEM((2, k, ys.shape[2]), BF16), pltpu.SemaphoreType.DMA((2, n_e))]
    return specs, [arow, ys, expand, lanemod], scratch


def _inproj(x, comb, modtab_prev, modtab, g, w_bf16, layer, ctx_len, n_f32, gw):
    b, l, d = x.shape
    n = w_bf16.shape[2]
    tm = ROW_TILE
    ctx_tiles = ctx_len // tm
    row = pl.BlockSpec((1, tm, d), lambda b, t, *_: (b, t, 0))
    has_moe = comb is not None
    in_specs = [row]
    args = [x]
    scratch = []
    prefetch = []
    if has_moe:
        specs, cargs, scratch = _combine_specs(comb, tm, 0)
        in_specs += specs + [_seg_spec(d, ctx_tiles)]
        args += cargs + [modtab_prev]
        prefetch = [comb[0], comb[1]]
    in_specs += [_seg_spec(d, ctx_tiles),
                 pl.BlockSpec((1, d), lambda b, t, *_: (0, 0)),
                 pl.BlockSpec((1, d, n), lambda b, t, *_: (layer, 0, 0))]
    args += [modtab, g.reshape(1, d), w_bf16]
    n_fg, n_g = n_f32 // gw, (n - n_f32) // gw
    out_specs = [pl.BlockSpec((1, n_fg, tm, gw), lambda b, t, *_: (b, 0, t, 0)),
                 pl.BlockSpec((1, n_g, tm, gw), lambda b, t, *_: (b, 0, t, 0))]
    out_shape = [jax.ShapeDtypeStruct((b, n_fg, l, gw), F32), jax.ShapeDtypeStruct((b, n_g, l, gw), BF16)]
    if has_moe:
        out_specs = [row] + out_specs
        out_shape = [jax.ShapeDtypeStruct((b, l, d), F32)] + out_shape
    res = pl.pallas_call(
        functools.partial(_inproj_kernel, has_moe),
        grid_spec=pltpu.PrefetchScalarGridSpec(
            num_scalar_prefetch=len(prefetch), grid=(b, l // tm),
            in_specs=in_specs, out_specs=out_specs, scratch_shapes=scratch),
        out_shape=out_shape,
        compiler_params=_cparams(("arbitrary", "arbitrary")),
        name="inproj",
    )(*prefetch, *args)
    if has_moe:
        return res[0], res[1], res[2]
    return x, res[0], res[1]


def _bwd_chunk(j, ctx_chunks, n_chunks):
    return jnp.where(j < ctx_chunks, ctx_chunks - 1 - j, n_chunks - 1 + ctx_chunks - j)


def _sub_rows(sub):
    c = SCAN_CHUNK
    return (slice(sub * c, (sub + 1) * c), slice((SCAN_SUB - 1 - sub) * c, (SCAN_SUB - sub) * c))


def _decay_tables():
    c = SCAN_CHUNK
    w = np.zeros((W_ROWS, c), np.float32)
    masks = np.zeros((N_LEVELS + 1, c, c), np.float32)
    for lev in range(N_LEVELS):
        m = c >> (lev + 1)
        for t in range(c):
            g0 = (t // (2 * m)) * 2 * m
            bnd = g0 + m - 1
            if t > bnd:
                masks[lev, t, g0:bnd + 1] = 1.0
            if m in SMALL_HALVES:
                r0 = (1 + SMALL_HALVES.index(m)) * c
                if t > bnd:
                    w[r0 + t, bnd + 1:t + 1] = 1.0
                else:
                    w[r0 + t, t + 1:bnd + 1] = 1.0
    masks[N_LEVELS] = np.eye(c, dtype=np.float32)
    for t in range(c):
        w[t, :t + 1] = 1.0
    w_b = w.reshape(-1, c, c)[:, ::-1, ::-1].reshape(-1, c)
    masks_b = masks[:, ::-1, ::-1]
    both = np.stack([masks, masks_b])
    maskq = np.zeros((2, N_BIG_LEVELS, c // 2, c), np.float32)
    for d in range(2):
        for lev in range(N_BIG_LEVELS):
            m = c >> (lev + 1)
            rows = np.concatenate([np.arange(g0 + m * (1 - d), g0 + m * (1 - d) + m) for g0 in range(0, c, 2 * m)])
            maskq[d, lev] = both[d, lev, rows]
    return jnp.asarray(np.stack([w, w_b]), BF16), jnp.asarray(both, F32), jnp.asarray(maskq, F32)


def _hgrn_kernel(n_heads, qf_ref, ff_ref, if_ref, qb_ref, fb_ref, ib_ref, lb_ref, w_ref, mask_ref, maskq_ref,
                 of_ref, ob_ref, st_ref, a_ref):
    c = SCAN_CHUNK

    @pl.when(pl.program_id(1) == 0)
    def _():
        st_ref[...] = jnp.zeros_like(st_ref)

    dirs = ((qf_ref, ff_ref, if_ref, of_ref), (qb_ref, fb_ref, ib_ref, ob_ref))

    def gates(d, h, rows):
        q_ref, f_ref, i_ref, _ = dirs[d]
        sl = slice(h * HEAD_DIM, (h + 1) * HEAD_DIM)
        qb16 = q_ref[0, 0, rows, sl]
        xf = f_ref[0, 0, rows, sl]
        b_ = lb_ref[1:2, sl] + (jnp.minimum(xf, 0.0) - jnp.log(1.0 + jnp.exp(-jnp.abs(xf))))
        a_ = lb_ref[0:1, sl]
        lf2 = (jnp.maximum(a_, b_) + jnp.log(1.0 + jnp.exp(-jnp.abs(a_ - b_)))) * LOG2E
        f = jnp.exp2(lf2)
        k = 1.0 - f
        hi = lf2.astype(BF16)
        lo = (lf2 - hi.astype(F32)).astype(BF16)
        ex2 = jnp.dot(w_ref[d], jnp.concatenate([hi, lo], axis=1), preferred_element_type=F32)
        ex = ex2[:, :HEAD_DIM] + ex2[:, HEAD_DIM:]
        cum = ex[:c]
        a_ref[d, h] = cum
        return dict(d=d, h=h, sl=sl, rows=rows, qb16=qb16, q=qb16.astype(F32), v=i_ref[0, 0, rows, sl], f=f, k=k,
                    ex=ex, cum=cum)

    def diagonal(s):
        s["kb16"] = s["k"].astype(BF16)
        s["scores"] = _nt_dot(s["qb16"], s["kb16"]) * mask_ref[s["d"], N_LEVELS]

    def level(s, lev):
        d, h, q, k, ex, cum = s["d"], s["h"], s["q"], s["k"], s["ex"], s["cum"]
        m = c >> (lev + 1)
        if m == 1:
            lhs, rhs = (q * s["f"]).astype(BF16), s["kb16"]
        else:
            if m in SMALL_HALVES:
                r0 = (1 + SMALL_HALVES.index(m)) * c
                e = jnp.exp2(ex[r0:r0 + c])
            else:
                parts = []
                for g0 in range(0, c, 2 * m):
                    mid = g0 + m - 1 + d
                    parts.append(cum[g0:g0 + 2 * m] - a_ref[d, h, mid:mid + 1, :])
                e = jnp.exp2(-jnp.abs(jnp.concatenate(parts, axis=0) if len(parts) > 1 else parts[0]))
                off = m * (1 - d)
                lhs = jnp.concatenate([q[g0 + off:g0 + off + m] * e[g0 + off:g0 + off + m]
                                       for g0 in range(0, c, 2 * m)], axis=0).astype(BF16)
                upd = _nt_dot(lhs, (k * e).astype(BF16)) * maskq_ref[d, lev]
                sc, pieces = s["scores"], []
                for i, g0 in enumerate(range(0, c, 2 * m)):
                    halves = [sc[g0:g0 + m], sc[g0 + m:g0 + 2 * m]]
                    halves[1 - d] = halves[1 - d] + upd[i * m:(i + 1) * m]
                    pieces += halves
                s["scores"] = jnp.concatenate(pieces, axis=0)
                return
            lhs, rhs = (q * e).astype(BF16), (k * e).astype(BF16)
        s["scores"] = s["scores"] + _nt_dot(lhs, rhs) * mask_ref[d, lev]

    def carry(s):
        d, h, q, k, v, cum = s["d"], s["h"], s["q"], s["k"], s["v"], s["cum"]
        edge = (c - 1) * (1 - d)
        tot = a_ref[d, h, edge:edge + 1, :]
        st = st_ref[d, h]
        out = jnp.dot(s["scores"].astype(BF16), v, preferred_element_type=F32)
        out = out + _nt_dot((q * jnp.exp2(cum)).astype(BF16), st.astype(BF16))
        dirs[d][3][0, s["rows"], s["sl"]] = out.astype(BF16)
        kdec = (k * jnp.exp2(tot - cum)).astype(BF16)
        st_ref[d, h] = st * jnp.exp2(tot) + _tn_dot(v, kdec)

    units = [(d, h) for d in range(2) for h in range(n_heads)]
    for sub in range(SCAN_SUB):
        rows = _sub_rows(sub)
        for g in range(0, len(units), SCAN_INTERLEAVE):
            states = [gates(d, h, rows[d]) for d, h in units[g:g + SCAN_INTERLEAVE]]
            for s_ in states:
                diagonal(s_)
                for lev in range(N_LEVELS):
                    level(s_, lev)
            for s_ in states:
                carry(s_)


def _hgrn_scan(pf, p, lb_tab, w_tab, mask_tab, maskq_tab, ctx_len, width):
    b, _, l, _ = p.shape
    c = SCAN_CHUNK
    rows = SCAN_SUB * c
    n_chunks = l // rows
    ctx_chunks = ctx_len // rows
    n_heads = width // HEAD_DIM

    def fwd(g):
        return pl.BlockSpec((1, 1, rows, width), lambda b, j: (b, g, j, 0))

    def bwd(g):
        return pl.BlockSpec((1, 1, rows, width), lambda b, j: (b, g, _bwd_chunk(j, ctx_chunks, n_chunks), 0))

    out_shape = jax.ShapeDtypeStruct((b, l, width), BF16)
    return pl.pallas_call(
        functools.partial(_hgrn_kernel, n_heads),
        grid=(b, n_chunks),
        in_specs=[fwd(P_HQ), fwd(0), fwd(P_HI), bwd(P_HQ), bwd(1), bwd(P_HI),
                  pl.BlockSpec((2, width), lambda b, j: (0, 0)),
                  pl.BlockSpec(w_tab.shape, lambda b, j: (0, 0, 0)),
                  pl.BlockSpec(mask_tab.shape, lambda b, j: (0, 0, 0, 0)),
                  pl.BlockSpec(maskq_tab.shape, lambda b, j: (0, 0, 0, 0))],
        out_specs=[pl.BlockSpec((1, rows, width), lambda b, j: (b, j, 0)),
                   pl.BlockSpec((1, rows, width), lambda b, j: (b, _bwd_chunk(j, ctx_chunks, n_chunks), 0))],
        out_shape=[out_shape, out_shape],
        scratch_shapes=[pltpu.VMEM((2, n_heads, HEAD_DIM, HEAD_DIM), F32),
                        pltpu.VMEM((2, n_heads, c, HEAD_DIM), F32)],
        compiler_params=_cparams(("arbitrary", "arbitrary")),
        name="hgrn_scan",
    )(p, pf, p, p, pf, p, lb_tab, w_tab, mask_tab, maskq_tab)


def _rope(z, cos_rep, sin_signed, even_lane):
    partner = jnp.where(even_lane, pltpu.roll(z, LANES - 1, 1), pltpu.roll(z, 1, 1))
    return z * cos_rep + partner * sin_signed


def _ret_kernel(n_heads, lg_ref, qf_ref, kf_ref, vf_ref, cf_ref, sf_ref,
                qb_ref, kb_ref, vb_ref, cb_ref, sb_ref, of_ref, ob_ref, st_ref, dm_ref, in_ref, tl_ref):
    c = SCAN_CHUNK
    k_scale = HEAD_DIM ** -0.5

    @pl.when(pl.program_id(1) == 0)
    def _():
        st_ref[...] = jnp.zeros_like(st_ref)
        ti = lax.broadcasted_iota(jnp.int32, (c, c), 0)
        si = lax.broadcasted_iota(jnp.int32, (c, c), 1)
        rowf = lax.broadcasted_iota(jnp.int32, (c, HEAD_DIM), 0).astype(F32)
        for d in range(2):
            rel = (ti - si) if d == 0 else (si - ti)
            relf = jnp.maximum(rel, 0).astype(F32)
            for h in range(n_heads):
                lg = lg_ref[d, h]
                dm_ref[d, h] = jnp.where(rel >= 0, jnp.exp(relf * lg), 0.0)
                if d == 0:
                    in_ref[d, h] = jnp.exp((rowf + 1.0) * lg)
                    tl_ref[d, h] = jnp.exp((c - 1.0 - rowf) * lg)
                else:
                    in_ref[d, h] = jnp.exp((c - rowf) * lg)
                    tl_ref[d, h] = jnp.exp(rowf * lg)

    even_lane = (lax.broadcasted_iota(jnp.int32, (c, HEAD_DIM), 1) & 1) == 0
    dirs = ((qf_ref, kf_ref, vf_ref, cf_ref, sf_ref, of_ref), (qb_ref, kb_ref, vb_ref, cb_ref, sb_ref, ob_ref))
    def pairs(d, h, rows):
        q_ref, k_ref, v_ref, c_ref, s_ref, _ = dirs[d]
        sl = slice(h * HEAD_DIM, (h + 1) * HEAD_DIM)
        q = _rope(q_ref[0, 0, rows, sl].astype(F32), c_ref[rows, :], s_ref[rows, :], even_lane)
        k = _rope(k_ref[0, 0, rows, sl].astype(F32), c_ref[rows, :], s_ref[rows, :], even_lane) * k_scale
        qb16 = q.astype(BF16)
        scores = _nt_dot(qb16, k.astype(BF16)) * dm_ref[d, h]
        return dict(d=d, h=h, sl=sl, rows=rows, qb16=qb16, k=k, v=v_ref[0, 0, rows, sl], scores=scores)

    def carry(s):
        d, h, v = s["d"], s["h"], s["v"]
        st = st_ref[d, h]
        out = jnp.dot(s["scores"].astype(BF16), v, preferred_element_type=F32)
        out = out + _nt_dot(s["qb16"], st.astype(BF16)) * in_ref[d, h]
        dirs[d][5][0, s["rows"], s["sl"]] = out.astype(BF16)
        edge = (c - 1) * (1 - d)
        st_ref[d, h] = (st * in_ref[d, h, edge:edge + 1, :]
                        + _tn_dot(v, (s["k"] * tl_ref[d, h]).astype(BF16)))

    units = [(d, h) for d in range(2) for h in range(n_heads)]
    for sub in range(SCAN_SUB):
        rows = _sub_rows(sub)
        for g in range(0, len(units), SCAN_INTERLEAVE):
            states = [pairs(d, h, rows[d]) for d, h in units[g:g + SCAN_INTERLEAVE]]
            for s_ in states:
                carry(s_)


def _ret_scan(p, lg, cos_rep, sin_signed, ctx_len, width):
    b, _, l, _ = p.shape
    c = SCAN_CHUNK
    rows = SCAN_SUB * c
    n_chunks = l // rows
    ctx_chunks = ctx_len // rows
    n_heads = width // HEAD_DIM

    def fwd(g):
        return pl.BlockSpec((1, 1, rows, width), lambda b, j, lg: (b, g, j, 0))

    def bwd(g):
        return pl.BlockSpec((1, 1, rows, width), lambda b, j, lg: (b, g, _bwd_chunk(j, ctx_chunks, n_chunks), 0))

    tab_f = pl.BlockSpec((rows, HEAD_DIM), lambda b, j, lg: (j, 0))
    tab_b = pl.BlockSpec((rows, HEAD_DIM), lambda b, j, lg: (_bwd_chunk(j, ctx_chunks, n_chunks), 0))
    out_shape = jax.ShapeDtypeStruct((b, l, width), BF16)
    return pl.pallas_call(
        functools.partial(_ret_kernel, n_heads),
        grid_spec=pltpu.PrefetchScalarGridSpec(
            num_scalar_prefetch=1,
            grid=(b, n_chunks),
            in_specs=[fwd(P_RQ), fwd(P_RK), fwd(P_RV), tab_f, tab_f,
                      bwd(P_RQ), bwd(P_RK), bwd(P_RV), tab_b, tab_b],
            out_specs=[pl.BlockSpec((1, rows, width), lambda b, j, lg: (b, j, 0)),
                       pl.BlockSpec((1, rows, width),
                                    lambda b, j, lg: (b, _bwd_chunk(j, ctx_chunks, n_chunks), 0))],
            scratch_shapes=[pltpu.VMEM((2, n_heads, HEAD_DIM, HEAD_DIM), F32),
                            pltpu.VMEM((2, n_heads, c, c), F32),
                            pltpu.VMEM((2, n_heads, c, HEAD_DIM), F32),
                            pltpu.VMEM((2, n_heads, c, HEAD_DIM), F32)]),
        out_shape=[out_shape, out_shape],
        compiler_params=_cparams(("arbitrary", "arbitrary")),
        name="ret_scan",
    )(lg, p, p, p, cos_rep, sin_signed, p, p, p, cos_rep, sin_signed)


def _outproj_kernel(n_hg, n_ret, x_ref, hf_ref, hb_ref, rf_ref, rb_ref, hg_ref, rg_ref, mod_ref,
                    on_ref, wo_ref, g2_ref, wrh_ref, wrl_ref, xo_ref, h2_ref, aff_ref):
    mod = mod_ref[0]

    def mixed(rows):
        o_hg = hf_ref[0, rows, :].astype(F32) + hb_ref[0, rows, :].astype(F32)
        o_rt = rf_ref[0, rows, :].astype(F32) + rb_ref[0, rows, :].astype(F32)
        parts = []
        for h in range(n_hg):
            sl = slice(h * HEAD_DIM, (h + 1) * HEAD_DIM)
            parts.append((_rms(o_hg[:, sl]) * on_ref[...]) * _silu(hg_ref[0, 0, rows, sl].astype(F32)))
        for h in range(n_ret):
            sl = slice(h * HEAD_DIM, (h + 1) * HEAD_DIM)
            parts.append(_rms(o_rt[:, sl]) * _silu(rg_ref[0, 0, rows, sl].astype(F32)))
        return jnp.concatenate(parts, axis=1).astype(BF16)

    def residual(rows, mix):
        x = x_ref[0, rows, :] + mod[2:3] * jnp.dot(mix, wo_ref[0], preferred_element_type=F32)
        xo_ref[0, rows, :] = x
        h2 = (_rms(x) * g2_ref[...]) * (1.0 + mod[4:5]) + mod[3:4]
        h2_ref[0, rows, :] = h2.astype(BF16)
        return h2

    def route(i, h2):
        hh = h2.astype(BF16)
        hl = (h2 - hh.astype(F32)).astype(BF16)
        logits = _nt_dot(wrh_ref[...], hh) + _nt_dot(wrh_ref[...], hl) + _nt_dot(wrl_ref[...], hh)
        mx = jnp.max(logits, axis=0, keepdims=True)
        ex = jnp.exp(logits - mx)
        aff_ref[0, i] = ex / jnp.sum(ex, axis=0, keepdims=True)

    blocks = [slice(i * LANES, (i + 1) * LANES) for i in range(aff_ref.shape[1])]
    mixes = [mixed(rows) for rows in blocks]
    h2s = [residual(rows, mix) for rows, mix in zip(blocks, mixes)]
    for i, h2 in enumerate(h2s):
        route(i, h2)


def _outproj(x, hf, hb, rf, rb, p, modtab, onorm, wo_bf16, layer, g2, w_router, ctx_len, hg_width):
    b, l, d = x.shape
    tm = ROW_TILE
    ctx_tiles = ctx_len // tm
    n_hg = hg_width // HEAD_DIM
    ret_width = d - hg_width
    n_ret = ret_width // HEAD_DIM
    n_e = w_router.shape[1]
    wrt = w_router.T
    wrh = wrt.astype(BF16)
    wrl = (wrt - wrh.astype(F32)).astype(BF16)
    row = pl.BlockSpec((1, tm, d), lambda b, t: (b, t, 0))
    hrow = pl.BlockSpec((1, tm, hg_width), lambda b, t: (b, t, 0))
    rrow = pl.BlockSpec((1, tm, ret_width), lambda b, t: (b, t, 0))
    full = lambda shape: pl.BlockSpec(shape, lambda b, t: tuple(0 for _ in shape))
    return pl.pallas_call(
        functools.partial(_outproj_kernel, n_hg, n_ret),
        grid=(b, l // tm),
        in_specs=[row, hrow, hrow, rrow, rrow,
                  pl.BlockSpec((1, 1, tm, hg_width), lambda b, t: (b, P_HGT, t, 0)),
                  pl.BlockSpec((1, 1, tm, ret_width), lambda b, t: (b, P_RGT, t, 0)),
                  _seg_spec(d, ctx_tiles),
                  full((1, HEAD_DIM)), pl.BlockSpec((1, d, d), lambda b, t: (layer, 0, 0)),
                  full((1, d)), full((n_e, d)), full((n_e, d))],
        out_specs=[row, row, pl.BlockSpec((1, tm // LANES, n_e, LANES), lambda b, t: (b, t, 0, 0))],
        out_shape=[jax.ShapeDtypeStruct((b, l, d), F32),
                   jax.ShapeDtypeStruct((b, l, d), BF16),
                   jax.ShapeDtypeStruct((b, l // LANES, n_e, LANES), F32)],
        compiler_params=_cparams(("arbitrary", "arbitrary")),
        name="outproj",
    )(x, hf, hb, rf, rb, p, p, modtab, onorm.reshape(1, HEAD_DIM), wo_bf16, g2.reshape(1, d), wrh, wrl)


FFN_SPLIT = 2
CAST_ROWS = 64


def _ffn_kernel(layer, x_ref, gate_ref, wg_hbm, wu_hbm, wd_hbm, o_ref, sg_ref, su_ref, sd_ref,
                wg_ref, wu_ref, wd_ref, sem):
    e, t, n_e = pl.program_id(0), pl.program_id(1), pl.num_programs(0)
    pairs = ((wg_hbm, sg_ref, wg_ref), (wu_hbm, su_ref, wu_ref), (wd_hbm, sd_ref, wd_ref))

    def fetch(expert):
        return [pltpu.make_async_copy(hbm.at[layer, expert], stage, sem.at[i])
                for i, (hbm, stage, _) in enumerate(pairs)]

    @pl.when((e == 0) & (t == 0))
    def _():
        for cp in fetch(0):
            cp.start()

    @pl.when(t == 0)
    def _():
        for cp in fetch(e):
            cp.wait()
        for _, stage, w16 in pairs:
            def cast(i, carry, stage=stage, w16=w16):
                rows = pl.ds(pl.multiple_of(i * CAST_ROWS, CAST_ROWS), CAST_ROWS)
                w16[rows, :] = stage[rows, :].astype(BF16)
                return carry
            lax.fori_loop(0, stage.shape[0] // CAST_ROWS, cast, 0)

        @pl.when(e + 1 < n_e)
        def _():
            for cp in fetch(e + 1):
                cp.start()

    x = x_ref[0]
    ff = wg_ref.shape[1]
    piece = ff // FFN_SPLIT
    y = jnp.zeros((x.shape[0], wd_ref.shape[1]), F32)
    for i in range(FFN_SPLIT):
        cs = slice(i * piece, (i + 1) * piece)
        a = jnp.dot(x, wg_ref[:, cs], preferred_element_type=F32)
        u = jnp.dot(x, wu_ref[:, cs], preferred_element_type=F32)
        hid = (_silu(a) * u).astype(BF16)
        y = y + jnp.dot(hid, wd_ref[cs, :], preferred_element_type=F32)
    o_ref[0] = (y * gate_ref[0]).astype(BF16)


def _row_tile(rows, cap=512):
    best = 16
    for t in range(16, cap + 1, 16):
        if rows % t == 0:
            best = t
    return best


def _expert_ffn(xs, gates, wg, wu, wd, layer):
    n_e, rows, d = xs.shape
    ff = wg.shape[3]
    tm = _row_tile(rows)
    assert ff % FFN_SPLIT == 0 and d % CAST_ROWS == 0 and ff % CAST_ROWS == 0
    hbm = pl.BlockSpec(memory_space=pl.ANY)
    return pl.pallas_call(
        functools.partial(_ffn_kernel, layer),
        grid=(n_e, rows // tm),
        in_specs=[pl.BlockSpec((1, tm, d), lambda e, t: (e, t, 0)),
                  pl.BlockSpec((1, tm, 1), lambda e, t: (e, t, 0)),
                  hbm, hbm, hbm],
        out_specs=pl.BlockSpec((1, tm, d), lambda e, t: (e, t, 0)),
        out_shape=jax.ShapeDtypeStruct((n_e, rows, d), BF16),
        scratch_shapes=[pltpu.VMEM((d, ff), F32), pltpu.VMEM((d, ff), F32), pltpu.VMEM((ff, d), F32),
                        pltpu.VMEM((d, ff), BF16), pltpu.VMEM((d, ff), BF16), pltpu.VMEM((ff, d), BF16),
                        pltpu.SemaphoreType.DMA((3,))],
        compiler_params=_cparams(("arbitrary", "arbitrary")),
        name="expert_ffn",
    )(xs, gates, wg, wu, wd)


def _final_kernel(first_tile, start_ref, npass_ref, x_ref, arow_ref, ys_ref, expand_ref, lanemod_ref,
                  gt2_ref, g_ref, o_ref, buf_ref, sem):
    moe = _combine_tile(first_tile, start_ref, npass_ref, arow_ref, ys_ref, expand_ref, lanemod_ref, buf_ref, sem)
    x = x_ref[0] + gt2_ref[0][5:6] * moe
    o_ref[0] = _rms(x) * g_ref[...]


def _final(x, comb, modtab, g, ctx_len):
    b, l, d = x.shape
    tm = ROW_TILE
    ctx_tiles = ctx_len // tm
    n_t = (l - ctx_len) // tm
    row = pl.BlockSpec((1, tm, d), lambda b, t, *_: (b, t + ctx_tiles, 0))
    specs, cargs, scratch = _combine_specs(comb, tm, ctx_tiles)
    return pl.pallas_call(
        functools.partial(_final_kernel, ctx_tiles),
        grid_spec=pltpu.PrefetchScalarGridSpec(
            num_scalar_prefetch=2, grid=(b, n_t),
            in_specs=[row] + specs + [pl.BlockSpec((1, 6, d), lambda b, t, *_: (2 * b + 1, 0, 0)),
                                      pl.BlockSpec((1, d), lambda b, t, *_: (0, 0))],
            out_specs=pl.BlockSpec((1, tm, d), lambda b, t, *_: (b, t, 0)),
            scratch_shapes=scratch),
        out_shape=jax.ShapeDtypeStruct((b, l - ctx_len, d), F32),
        compiler_params=_cparams(("arbitrary", "arbitrary")),
        name="final_norm",
    )(comb[0], comb[1], x, *cargs, modtab, g.reshape(1, d))


def _rope_tables(rows, ctx_len):
    n_freq = HEAD_DIM // 4
    inv = ROPE_BASE ** (-jnp.arange(n_freq, dtype=F32) / n_freq)
    r = jnp.repeat(jnp.arange(rows, dtype=F32), GRID_W)
    cc = jnp.tile(jnp.arange(GRID_W, dtype=F32), rows)
    lat = jnp.concatenate([r[:, None] * inv, cc[:, None] * inv], axis=-1)
    ang = jnp.concatenate([jnp.zeros((ctx_len, 2 * n_freq), F32), lat], axis=0)
    cos = jnp.cos(ang)
    sin = jnp.sin(ang)
    cos_rep = jnp.repeat(cos, 2, axis=-1)
    sin_signed = jnp.stack([-sin, sin], axis=-1).reshape(ang.shape[0], HEAD_DIM)
    return cos_rep, sin_signed


def _select_kernel(nb, cap, aff_ref, tri_ref, pos_ref, off_ref, tot_ref, cum_ref):
    n_e = aff_ref.shape[2]
    bits = lax.bitcast_convert_type(aff_ref[0], jnp.int32)
    ones = jnp.ones((LANES, LANES), BF16)
    tri = tri_ref[...]

    def total(flags):
        return jnp.dot(jnp.sum(flags, axis=0).astype(BF16), ones, preferred_element_type=F32)

    def thr_step(i, thr):
        cand = thr | (jnp.int32(1) << (30 - i))
        cnt = total((bits >= cand[None]).astype(F32))
        return jnp.where(cnt >= cap, cand, thr)

    thr = lax.fori_loop(0, 31, thr_step, jnp.zeros((n_e, LANES), jnp.int32))

    def prefix(flags):
        f2 = flags.reshape(nb * n_e, LANES).astype(BF16)
        incl = jnp.dot(f2, tri, preferred_element_type=F32).reshape(nb, n_e, LANES)
        tot_ref[...] = jnp.dot(f2, ones, preferred_element_type=F32).reshape(nb, n_e, LANES)

        def step(k, carry):
            cum_ref[k] = carry
            return carry + tot_ref[k]

        lax.fori_loop(0, nb, step, jnp.zeros((n_e, LANES), F32))
        return incl, cum_ref[...]

    gt = (bits > thr[None]).astype(F32)
    eq = (bits == thr[None]).astype(F32)
    need = cap - total(gt)
    eq_incl, eq_off = prefix(eq)
    sel = gt + eq * ((eq_off + eq_incl - eq) < need[None]).astype(F32)
    incl, off = prefix(sel)
    pos_ref[0] = jnp.where(sel > 0.5, off + incl - 1.0, -1.0)
    off_ref[0] = off


def _compact_kernel(nb, off_ref, pos_ref, aff_ref, out_ref):
    b = pl.program_id(0)
    n_e = pos_ref.shape[2]
    out_ref[...] = jnp.zeros_like(out_ref)
    slot_row = lax.broadcasted_iota(jnp.int32, (LANES, LANES), 0).astype(F32)
    lane8 = lax.broadcasted_iota(jnp.int32, (8, LANES), 1)
    row8 = lax.broadcasted_iota(jnp.int32, (8, LANES), 0)
    lane_f = lane8.astype(F32)

    def block(k, carry):
        starts = [off_ref[b, k, e] for e in range(n_e)]
        picked = []
        for e in range(n_e):
            blk = starts[e] >> 7
            p = pos_ref[0, k, e:e + 1, :]
            g = aff_ref[0, k, e:e + 1, :]
            r = p - jnp.asarray(blk * LANES, F32)
            r = jnp.where(r >= LANES, r - LANES, r)
            onehot = jnp.where((slot_row == r) & (p >= 0.0), 1.0, 0.0).astype(BF16)
            g_hi = g.astype(BF16).astype(F32)
            g_mid = (g - g_hi).astype(BF16).astype(F32)
            g_lo = (g - g_hi - g_mid).astype(BF16).astype(F32)
            vals = jnp.where(row8 == 0, lane_f,
                             jnp.where(row8 == 1, 1.0,
                                       jnp.where(row8 == 2, g_hi,
                                                 jnp.where(row8 == 3, g_mid,
                                                           jnp.where(row8 == 4, g_lo, 0.0)))))
            picked.append(_nt_dot(vals.astype(BF16), onehot))
        for e in range(n_e):
            c = picked[e]
            blk = starts[e] >> 7
            tok = c[1:2] * jnp.asarray(k * LANES, F32) + c[0:1]
            gate = c[2:3] + c[3:4] + c[4:5]
            tile = jnp.where(row8 == 0, tok, jnp.where(row8 == 1, gate, 0.0))
            first = lane8 >= (starts[e] & (LANES - 1))
            out_ref[0, e, blk] += jnp.where(first, tile, 0.0)
            out_ref[0, e, blk + 1] += jnp.where(first, 0.0, tile)
        return carry

    lax.fori_loop(0, nb, block, 0, unroll=2)


def _route(aff_blocks, cap):
    bsz, nb, n_e, _ = aff_blocks.shape
    assert nb <= 256
    tri = jnp.asarray(np.triu(np.ones((LANES, LANES), np.float32)), BF16)
    blk = pl.BlockSpec((1, nb, n_e, LANES), lambda b: (b, 0, 0, 0))
    shp = jax.ShapeDtypeStruct((bsz, nb, n_e, LANES), F32)
    pos, off_rep = pl.pallas_call(
        functools.partial(_select_kernel, nb, cap),
        grid=(bsz,),
        in_specs=[blk, pl.BlockSpec((LANES, LANES), lambda b: (0, 0))],
        out_specs=[blk, blk],
        out_shape=[shp, shp],
        scratch_shapes=[pltpu.VMEM((nb, n_e, LANES), F32), pltpu.VMEM((nb, n_e, LANES), F32)],
        compiler_params=_cparams(("arbitrary",)),
        name="route_select",
    )(aff_blocks, tri)
    off = off_rep[:, :, :, 0].astype(jnp.int32)
    n_rows = cap // LANES + 2
    blk1 = pl.BlockSpec((1, nb, n_e, LANES), lambda b, off: (b, 0, 0, 0))
    lists = pl.pallas_call(
        functools.partial(_compact_kernel, nb),
        grid_spec=pltpu.PrefetchScalarGridSpec(
            num_scalar_prefetch=1,
            grid=(bsz,),
            in_specs=[blk1, blk1],
            out_specs=pl.BlockSpec((1, n_e, n_rows, 8, LANES), lambda b, off: (b, 0, 0, 0, 0))),
        out_shape=jax.ShapeDtypeStruct((bsz, n_e, n_rows, 8, LANES), F32),
        compiler_params=_cparams(("arbitrary",)),
        name="route_compact",
    )(off, pos, aff_blocks)
    tokens = lists[:, :, :, 0, :].reshape(bsz, n_e, n_rows * LANES)[:, :, :cap].astype(jnp.int32)
    gates = lists[:, :, :, 1, :].reshape(bsz, n_e, n_rows * LANES)[:, :, :cap]
    return gates, tokens, pos, off


def kernel(x, c, ctx, c_ctx, w_ada, b_ada, g_mix, g_ffn, w_in, w_out, hg_lb_logits, hg_onorm,
           ret_decay, w_router, w_e_gate, w_e_up, w_e_down, g_final):
    bsz, n, d = x.shape
    lc = ctx.shape[1]
    depth = w_ada.shape[0]
    l = lc + n
    hg_width = hg_lb_logits.shape[1]
    ret_width = d - hg_width
    n_e = w_router.shape[2]
    assert lc % ROW_TILE == 0 and n % ROW_TILE == 0
    assert lc % (SCAN_SUB * SCAN_CHUNK) == 0 and n % (SCAN_SUB * SCAN_CHUNK) == 0
    assert hg_width % HEAD_DIM == 0 and ret_width % HEAD_DIM == 0 and n % GRID_W == 0
    assert w_in.shape[2] == 5 * hg_width + 4 * ret_width and hg_width == ret_width

    cos_rep, sin_signed = _rope_tables(n // GRID_W, lc)
    w_tab, mask_tab, maskq_tab = _decay_tables()

    gamma_cum = jnp.cumsum(jax.nn.softmax(hg_lb_logits.astype(F32), axis=0), axis=0)
    lbs = gamma_cum - gamma_cum[0:1]
    lb_tabs = jnp.stack([jnp.log(lbs), jnp.log1p(-lbs)], axis=1)
    log_gammas = -jnp.exp(ret_decay.astype(F32))

    cond8 = jnp.zeros((8, d), F32).at[:bsz].set(c).at[bsz].set(c_ctx)
    mods = _adaln(cond8, w_ada, b_ada).reshape(depth, 8, 6, d)
    modtabs = jnp.stack([jnp.broadcast_to(mods[:, bsz:bsz + 1], (depth, bsz, 6, d)), mods[:, :bsz]],
                        axis=2).reshape(depth, 2 * bsz, 6, d)

    cap_lat = CAPACITY_FACTOR * n // n_e
    cap_ctx = CAPACITY_FACTOR * lc // n_e
    boff = (jnp.arange(bsz, dtype=jnp.int32) * l)[:, None, None]

    gw = hg_width
    w_in16 = jnp.concatenate([w_in[:, :, gw:3 * gw], w_in[:, :, :gw], w_in[:, :, 3 * gw:]], axis=2).astype(BF16)
    w_out16 = w_out.astype(BF16)
    xs = jnp.concatenate([ctx, x], axis=1)
    comb = None
    for layer in range(depth):
        last = layer == depth - 1
        xs, pf, p = _inproj(xs, comb, modtabs[layer - 1] if layer else None, modtabs[layer], g_mix[layer],
                            w_in16, layer, lc, 2 * hg_width, hg_width)
        hf, hb = _hgrn_scan(pf, p, lb_tabs[layer], w_tab, mask_tab, maskq_tab, lc, hg_width)
        rf, rb = _ret_scan(p, log_gammas[layer], cos_rep, sin_signed, lc, ret_width)
        xs, h2, aff_t = _outproj(xs, hf, hb, rf, rb, p, modtabs[layer], hg_onorm[layer],
                                 w_out16, layer, g_ffn[layer], w_router[layer], lc, hg_width)
        blocks = [lc // LANES, n // LANES]
        caps = [cap_ctx, cap_lat]
        bases = [bsz * cap_lat + jnp.arange(bsz, dtype=jnp.int32) * cap_ctx,
                 jnp.arange(bsz, dtype=jnp.int32) * cap_lat]
        affs = [aff_t[:, :blocks[0]], aff_t[:, blocks[0]:]]
        tok_off = [0, lc]
        rows, gts, arows, starts, cnts = {}, {}, [], [], []
        for seg in (0, 1):
            if seg == 0 and last:
                arows.append(jnp.full((bsz, blocks[0], n_e, LANES), -1.0, F32))
                starts.append(jnp.zeros((bsz, blocks[0], n_e), jnp.int32))
                cnts.append(jnp.zeros((bsz, blocks[0], n_e), jnp.int32))
                continue
            g_s, i_s, pos_s, off_s = _route(affs[seg], caps[seg])
            rows[seg] = i_s + tok_off[seg] + boff
            gts[seg] = g_s
            kept = pos_s >= 0.0
            arows.append(jnp.where(kept, pos_s + bases[seg][:, None, None, None].astype(F32), -1.0))
            starts.append(off_s + bases[seg][:, None, None])
            cnts.append(jnp.sum(kept, axis=-1).astype(jnp.int32))
        order = [s_ for s_ in (1, 0) if s_ in rows]
        flat = jnp.concatenate([rows[s_].transpose(1, 0, 2).reshape(n_e, -1) for s_ in order], axis=1)
        gate = jnp.concatenate([gts[s_].transpose(1, 0, 2).reshape(n_e, -1) for s_ in order], axis=1)
        n_rows = flat.shape[1]
        pad = (-n_rows) % 16
        if pad:
            flat = jnp.pad(flat, ((0, 0), (0, pad)))
            gate = jnp.pad(gate, ((0, 0), (0, pad)))
        h2f = h2.reshape(bsz * l, d)
        gathered = h2f[flat.reshape(-1)].reshape(n_e, n_rows + pad, d)
        ys = _expert_ffn(gathered, gate[:, :, None], w_e_gate, w_e_up, w_e_down, layer)
        arow = jnp.concatenate(arows, axis=1)
        start = jnp.concatenate(starts, axis=1)
        nbt = ROW_TILE // LANES
        start_t = start[:, ::nbt]
        cnt_t = jnp.concatenate(cnts, axis=1).reshape(bsz, -1, nbt, n_e).sum(axis=2)
        need = start_t % YS_ALIGN + cnt_t
        npass = jnp.max(jnp.where(cnt_t > 0, -(-need // COMBINE_ROWS), 0), axis=-1).astype(jnp.int32)
        comb = (start.reshape(bsz, -1), npass, arow, ys)
    return _final(xs, comb, modtabs[depth - 1], g_final, lc)
```

```python
import functools

import numpy as np
import jax
import jax.numpy as jnp
from jax import lax
from jax.experimental import pallas as pl
from jax.experimental.pallas import tpu as pltpu

F32 = jnp.float32
BF16 = jnp.bfloat16

EPS = 1e-6
GRID_W = 64
ROPE_BASE = 10000.0
N_EXPERTS = 16
CAPACITY_FACTOR = 2
HEAD_DIM = 128
LANES = 128
SCAN_CHUNK = 128
ROW_TILE = 256
VMEM_LIMIT = 56 * 1024 * 1024

P_HQ, P_HI, P_HGT, P_RQ, P_RK, P_RV, P_RGT = range(7)

N_LEVELS = int(np.log2(SCAN_CHUNK))
SMALL_HALVES = (4, 2)
N_BIG_LEVELS = N_LEVELS - len(SMALL_HALVES) - 1
W_ROWS = (1 + len(SMALL_HALVES)) * SCAN_CHUNK
LOG2E = float(np.log2(np.e))
SCAN_INTERLEAVE = 4
SCAN_SUB = 2


def _cparams(sem, **kw):
    return pltpu.CompilerParams(dimension_semantics=sem, vmem_limit_bytes=VMEM_LIMIT, **kw)


def _nt_dot(a, b):
    return lax.dot_general(a, b, (((1,), (1,)), ((), ())), preferred_element_type=F32)


def _tn_dot(a, b):
    return lax.dot_general(a, b, (((0,), (0,)), ((), ())), preferred_element_type=F32)


def _sigmoid(x):
    return 1.0 / (1.0 + jnp.exp(-x))


def _silu(x):
    return x * _sigmoid(x)


def _adaln_kernel(cond_ref, w_ref, b_ref, o_ref):
    s = _silu(cond_ref[...])
    o_ref[0] = jnp.dot(s.astype(BF16), w_ref[0].astype(BF16), preferred_element_type=F32) + b_ref[0]


def _adaln(cond8, w_ada, b_ada):
    depth, d, n6 = w_ada.shape
    tn = n6 // 4
    return pl.pallas_call(
        _adaln_kernel,
        grid=(depth, n6 // tn),
        in_specs=[pl.BlockSpec((8, d), lambda l, j: (0, 0)),
                  pl.BlockSpec((1, d, tn), lambda l, j: (l, 0, j)),
                  pl.BlockSpec((1, 1, tn), lambda l, j: (l, 0, j))],
        out_specs=pl.BlockSpec((1, 8, tn), lambda l, j: (l, 0, j)),
        out_shape=jax.ShapeDtypeStruct((depth, 8, n6), F32),
        compiler_params=_cparams(("arbitrary", "arbitrary")),
        name="adaln",
    )(cond8, w_ada, b_ada.reshape(depth, 1, n6))


def _rms(x):
    return x * lax.rsqrt(jnp.mean(x * x, axis=-1, keepdims=True) + EPS)


COMBINE_ROWS = 64
YS_ALIGN = 16


def _combine_tile(first_tile, start_ref, npass_ref, arow_ref, ys_ref, expand_ref, lanemod_ref, buf_ref, sem):
    n_e = arow_ref.shape[2]
    n_blk = arow_ref.shape[1]
    tm = n_blk * LANES
    wn = COMBINE_ROWS
    n_rows = ys_ref.shape[1]
    d = ys_ref.shape[2]
    b, t, n_t = pl.program_id(0), pl.program_id(1), pl.num_programs(1)
    step = b * n_t + t
    n_steps = pl.num_programs(0) * n_t
    slot = step % 2
    wrap = t + 1 == n_t
    b_next = jnp.where(wrap, b + 1, b)
    t_next = jnp.where(wrap, 0, t + 1)

    def window(bb, tt, e, p):
        begin = (start_ref[bb, (tt + first_tile) * n_blk * n_e + e] // YS_ALIGN) * YS_ALIGN
        src = jnp.minimum(begin + p * wn, n_rows - wn)
        return begin, pl.multiple_of(src, YS_ALIGN)

    def copy(to_slot, src, e):
        return pltpu.make_async_copy(ys_ref.at[e, pl.ds(src, wn), :],
                                     buf_ref.at[to_slot, pl.ds(e * wn, wn), :], sem.at[to_slot, e])

    @pl.when(step == 0)
    def _():
        for e in range(n_e):
            copy(slot, window(b, t, e, 0)[1], e).start()

    @pl.when(step + 1 < n_steps)
    def _():
        for e in range(n_e):
            copy(1 - slot, window(b_next, t_next, e, 0)[1], e).start()

    arow = jnp.concatenate([arow_ref[0, i] for i in range(n_blk)], axis=1)
    erow = lax.broadcasted_iota(jnp.int32, (n_e, tm), 0)

    def per_expert(values):
        out = jnp.zeros((n_e, tm), F32)
        for e in range(n_e):
            out = jnp.where(erow == e, values[e].astype(F32), out)
        return out

    rel0 = arow - per_expert([window(b, t, e, 0)[0] for e in range(n_e)])

    def place(p, srcs, acc):
        lo = jnp.asarray(p * wn, F32)
        mine = (arow >= 0.0) & (rel0 >= lo) & (rel0 < lo + wn)
        rel = jnp.where(mine, arow - per_expert(srcs), 255.0).astype(BF16)
        spread = _tn_dot(rel, expand_ref[...])
        onehot = jnp.where(spread == lanemod_ref[...], 1.0, 0.0).astype(BF16)
        for e in range(n_e):
            copy(slot, srcs[e], e).wait()
        return acc + jnp.dot(onehot, buf_ref[slot], preferred_element_type=F32)

    acc = place(0, [window(b, t, e, 0)[1] for e in range(n_e)], jnp.zeros((tm, d), F32))

    def extra_pass(p, acc):
        srcs = [window(b, t, e, p)[1] for e in range(n_e)]
        for e in range(n_e):
            copy(slot, srcs[e], e).start()
        return place(p, srcs, acc)

    return lax.fori_loop(1, npass_ref[b, t + first_tile], extra_pass, acc)


def _inproj_kernel(has_moe, *refs):
    if has_moe:
        (start_ref, npass_ref, x_ref, arow_ref, ys_ref, expand_ref, lanemod_ref, gt2_ref, mod_ref, g_ref, w_ref,
         xo_ref, pf_ref, p_ref, buf_ref, sem) = refs
        moe = _combine_tile(0, start_ref, npass_ref, arow_ref, ys_ref, expand_ref, lanemod_ref, buf_ref, sem)
        x = x_ref[0] + gt2_ref[0][5:6] * moe
        xo_ref[0] = x
    else:
        x_ref, mod_ref, g_ref, w_ref, pf_ref, p_ref = refs
        x = x_ref[0]
    mod = mod_ref[0]
    h = (_rms(x) * g_ref[...]) * (1.0 + mod[1:2]) + mod[0:1]
    res = jnp.dot(h.astype(BF16), w_ref[0], preferred_element_type=F32)
    n_fg, gw = pf_ref.shape[1], pf_ref.shape[3]
    for g in range(n_fg):
        pf_ref[0, g] = res[:, g * gw:(g + 1) * gw]
    for g in range(p_ref.shape[1]):
        p_ref[0, g] = res[:, (n_fg + g) * gw:(n_fg + g + 1) * gw].astype(BF16)


def _seg_spec(d, ctx_tiles):
    return pl.BlockSpec((1, 6, d), lambda b, t, *_: (2 * b + jnp.where(t < ctx_tiles, 0, 1), 0, 0))


def _combine_consts(n_e):
    wn = COMBINE_ROWS
    expand = np.zeros((n_e, n_e * wn), np.float32)
    for e in range(n_e):
        expand[e, e * wn:(e + 1) * wn] = 1.0
    lanemod = (np.arange(n_e * wn) % wn).astype(np.float32)[None, :]
    return jnp.asarray(expand, BF16), jnp.asarray(lanemod, F32)


def _combine_specs(comb, tm, first_tile):
    start, npass, arow, ys = comb
    n_e = arow.shape[2]
    expand, lanemod = _combine_consts(n_e)
    k = n_e * COMBINE_ROWS
    specs = [pl.BlockSpec((1, tm // LANES, n_e, LANES), lambda b, t, *_: (b, t + first_tile, 0, 0)),
             pl.BlockSpec(memory_space=pl.ANY),
             pl.BlockSpec((n_e, k), lambda b, t, *_: (0, 0)),
             pl.BlockSpec((1, k), lambda b, t, *_: (0, 0))]
    scratch = [pltpu.VMEM((2, k, ys.shape[2]), BF16), pltpu.SemaphoreType.DMA((2, n_e))]
    return specs, [arow, ys, expand, lanemod], scratch


def _inproj(x, comb, modtab_prev, modtab, g, w_bf16, layer, ctx_len, n_f32, gw):
    b, l, d = x.shape
    n = w_bf16.shape[2]
    tm = ROW_TILE
    ctx_tiles = ctx_len // tm
    row = pl.BlockSpec((1, tm, d), lambda b, t, *_: (b, t, 0))
    has_moe = comb is not None
    in_specs = [row]
    args = [x]
    scratch = []
    prefetch = []
    if has_moe:
        specs, cargs, scratch = _combine_specs(comb, tm, 0)
        in_specs += specs + [_seg_spec(d, ctx_tiles)]
        args += cargs + [modtab_prev]
        prefetch = [comb[0], comb[1]]
    in_specs += [_seg_spec(d, ctx_tiles),
                 pl.BlockSpec((1, d), lambda b, t, *_: (0, 0)),
                 pl.BlockSpec((1, d, n), lambda b, t, *_: (layer, 0, 0))]
    args += [modtab, g.reshape(1, d), w_bf16]
    n_fg, n_g = n_f32 // gw, (n - n_f32) // gw
    out_specs = [pl.BlockSpec((1, n_fg, tm, gw), lambda b, t, *_: (b, 0, t, 0)),
                 pl.BlockSpec((1, n_g, tm, gw), lambda b, t, *_: (b, 0, t, 0))]
    out_shape = [jax.ShapeDtypeStruct((b, n_fg, l, gw), F32), jax.ShapeDtypeStruct((b, n_g, l, gw), BF16)]
    if has_moe:
        out_specs = [row] + out_specs
        out_shape = [jax.ShapeDtypeStruct((b, l, d), F32)] + out_shape
    res = pl.pallas_call(
        functools.partial(_inproj_kernel, has_moe),
        grid_spec=pltpu.PrefetchScalarGridSpec(
            num_scalar_prefetch=len(prefetch), grid=(b, l // tm),
            in_specs=in_specs, out_specs=out_specs, scratch_shapes=scratch),
        out_shape=out_shape,
        compiler_params=_cparams(("arbitrary", "arbitrary")),
        name="inproj",
    )(*prefetch, *args)
    if has_moe:
        return res[0], res[1], res[2]
    return x, res[0], res[1]


def _bwd_chunk(j, ctx_chunks, n_chunks):
    return jnp.where(j < ctx_chunks, ctx_chunks - 1 - j, n_chunks - 1 + ctx_chunks - j)


def _sub_rows(sub):
    c = SCAN_CHUNK
    return (slice(sub * c, (sub + 1) * c), slice((SCAN_SUB - 1 - sub) * c, (SCAN_SUB - sub) * c))


def _decay_tables():
    c = SCAN_CHUNK
    w = np.zeros((W_ROWS, c), np.float32)
    masks = np.zeros((N_LEVELS + 1, c, c), np.float32)
    for lev in range(N_LEVELS):
        m = c >> (lev + 1)
        for t in range(c):
            g0 = (t // (2 * m)) * 2 * m
            bnd = g0 + m - 1
            if t > bnd:
                masks[lev, t, g0:bnd + 1] = 1.0
            if m in SMALL_HALVES:
                r0 = (1 + SMALL_HALVES.index(m)) * c
                if t > bnd:
                    w[r0 + t, bnd + 1:t + 1] = 1.0
                else:
                    w[r0 + t, t + 1:bnd + 1] = 1.0
    masks[N_LEVELS] = np.eye(c, dtype=np.float32)
    for t in range(c):
        w[t, :t + 1] = 1.0
    w_b = w.reshape(-1, c, c)[:, ::-1, ::-1].reshape(-1, c)
    masks_b = masks[:, ::-1, ::-1]
    both = np.stack([masks, masks_b])
    maskq = np.zeros((2, N_BIG_LEVELS, c // 2, c), np.float32)
    for d in range(2):
        for lev in range(N_BIG_LEVELS):
            m = c >> (lev + 1)
            rows = np.concatenate([np.arange(g0 + m * (1 - d), g0 + m * (1 - d) + m) for g0 in range(0, c, 2 * m)])
            maskq[d, lev] = both[d, lev, rows]
    return jnp.asarray(np.stack([w, w_b]), BF16), jnp.asarray(both, F32), jnp.asarray(maskq, F32)


def _hgrn_kernel(n_heads, qf_ref, ff_ref, if_ref, qb_ref, fb_ref, ib_ref, lb_ref, w_ref, mask_ref, maskq_ref,
                 of_ref, ob_ref, st_ref, a_ref):
    c = SCAN_CHUNK

    @pl.when(pl.program_id(1) == 0)
    def _():
        st_ref[...] = jnp.zeros_like(st_ref)

    dirs = ((qf_ref, ff_ref, if_ref, of_ref), (qb_ref, fb_ref, ib_ref, ob_ref))

    def gates(d, h, rows):
        q_ref, f_ref, i_ref, _ = dirs[d]
        sl = slice(h * HEAD_DIM, (h + 1) * HEAD_DIM)
        qb16 = q_ref[0, 0, rows, sl]
        xf = f_ref[0, 0, rows, sl]
        b_ = lb_ref[1:2, sl] + (jnp.minimum(xf, 0.0) - jnp.log(1.0 + jnp.exp(-jnp.abs(xf))))
        a_ = lb_ref[0:1, sl]
        lf2 = (jnp.maximum(a_, b_) + jnp.log(1.0 + jnp.exp(-jnp.abs(a_ - b_)))) * LOG2E
        f = jnp.exp2(lf2)
        k = 1.0 - f
        hi = lf2.astype(BF16)
        lo = (lf2 - hi.astype(F32)).astype(BF16)
        ex2 = jnp.dot(w_ref[d], jnp.concatenate([hi, lo], axis=1), preferred_element_type=F32)
        ex = ex2[:, :HEAD_DIM] + ex2[:, HEAD_DIM:]
        cum = ex[:c]
        a_ref[d, h] = cum
        return dict(d=d, h=h, sl=sl, rows=rows, qb16=qb16, q=qb16.astype(F32), v=i_ref[0, 0, rows, sl], f=f, k=k,
                    ex=ex, cum=cum)

    def diagonal(s):
        s["kb16"] = s["k"].astype(BF16)
        s["scores"] = _nt_dot(s["qb16"], s["kb16"]) * mask_ref[s["d"], N_LEVELS]

    def level(s, lev):
        d, h, q, k, ex, cum = s["d"], s["h"], s["q"], s["k"], s["ex"], s["cum"]
        m = c >> (lev + 1)
        if m == 1:
            lhs, rhs = (q * s["f"]).astype(BF16), s["kb16"]
        else:
            if m in SMALL_HALVES:
                r0 = (1 + SMALL_HALVES.index(m)) * c
                e = jnp.exp2(ex[r0:r0 + c])
            else:
                parts = []
                for g0 in range(0, c, 2 * m):
                    mid = g0 + m - 1 + d
                    parts.append(cum[g0:g0 + 2 * m] - a_ref[d, h, mid:mid + 1, :])
                e = jnp.exp2(-jnp.abs(jnp.concatenate(parts, axis=0) if len(parts) > 1 else parts[0]))
                off = m * (1 - d)
                lhs = jnp.concatenate([q[g0 + off:g0 + off + m] * e[g0 + off:g0 + off + m]
                                       for g0 in range(0, c, 2 * m)], axis=0).astype(BF16)
                upd = _nt_dot(lhs, (k * e).astype(BF16)) * maskq_ref[d, lev]
                sc, pieces = s["scores"], []
                for i, g0 in enumerate(range(0, c, 2 * m)):
                    halves = [sc[g0:g0 + m], sc[g0 + m:g0 + 2 * m]]
                    halves[1 - d] = halves[1 - d] + upd[i * m:(i + 1) * m]
                    pieces += halves
                s["scores"] = jnp.concatenate(pieces, axis=0)
                return
            lhs, rhs = (q * e).astype(BF16), (k * e).astype(BF16)
        s["scores"] = s["scores"] + _nt_dot(lhs, rhs) * mask_ref[d, lev]

    def carry(s):
        d, h, q, k, v, cum = s["d"], s["h"], s["q"], s["k"], s["v"], s["cum"]
        edge = (c - 1) * (1 - d)
        tot = a_ref[d, h, edge:edge + 1, :]
        st = st_ref[d, h]
        out = jnp.dot(s["scores"].astype(BF16), v, preferred_element_type=F32)
        out = out + _nt_dot((q * jnp.exp2(cum)).astype(BF16), st.astype(BF16))
        dirs[d][3][0, s["rows"], s["sl"]] = out.astype(BF16)
        kdec = (k * jnp.exp2(tot - cum)).astype(BF16)
        st_ref[d, h] = st * jnp.exp2(tot) + _tn_dot(v, kdec)

    units = [(d, h) for d in range(2) for h in range(n_heads)]
    for sub in range(SCAN_SUB):
        rows = _sub_rows(sub)
        for g in range(0, len(units), SCAN_INTERLEAVE):
            states = [gates(d, h, rows[d]) for d, h in units[g:g + SCAN_INTERLEAVE]]
            for s_ in states:
                diagonal(s_)
                for lev in range(N_LEVELS):
                    level(s_, lev)
            for s_ in states:
                carry(s_)


def _hgrn_scan(pf, p, lb_tab, w_tab, mask_tab, maskq_tab, ctx_len, width):
    b, _, l, _ = p.shape
    c = SCAN_CHUNK
    rows = SCAN_SUB * c
    n_chunks = l // rows
    ctx_chunks = ctx_len // rows
    n_heads = width // HEAD_DIM

    def fwd(g):
        return pl.BlockSpec((1, 1, rows, width), lambda b, j: (b, g, j, 0))

    def bwd(g):
        return pl.BlockSpec((1, 1, rows, width), lambda b, j: (b, g, _bwd_chunk(j, ctx_chunks, n_chunks), 0))

    out_shape = jax.ShapeDtypeStruct((b, l, width), BF16)
    return pl.pallas_call(
        functools.partial(_hgrn_kernel, n_heads),
        grid=(b, n_chunks),
        in_specs=[fwd(P_HQ), fwd(0), fwd(P_HI), bwd(P_HQ), bwd(1), bwd(P_HI),
                  pl.BlockSpec((2, width), lambda b, j: (0, 0)),
                  pl.BlockSpec(w_tab.shape, lambda b, j: (0, 0, 0)),
                  pl.BlockSpec(mask_tab.shape, lambda b, j: (0, 0, 0, 0)),
                  pl.BlockSpec(maskq_tab.shape, lambda b, j: (0, 0, 0, 0))],
        out_specs=[pl.BlockSpec((1, rows, width), lambda b, j: (b, j, 0)),
                   pl.BlockSpec((1, rows, width), lambda b, j: (b, _bwd_chunk(j, ctx_chunks, n_chunks), 0))],
        out_shape=[out_shape, out_shape],
        scratch_shapes=[pltpu.VMEM((2, n_heads, HEAD_DIM, HEAD_DIM), F32),
                        pltpu.VMEM((2, n_heads, c, HEAD_DIM), F32)],
        compiler_params=_cparams(("arbitrary", "arbitrary")),
        name="hgrn_scan",
    )(p, pf, p, p, pf, p, lb_tab, w_tab, mask_tab, maskq_tab)


def _rope(z, cos_rep, sin_signed, even_lane):
    partner = jnp.where(even_lane, pltpu.roll(z, LANES - 1, 1), pltpu.roll(z, 1, 1))
    return z * cos_rep + partner * sin_signed


def _ret_kernel(n_heads, lg_ref, qf_ref, kf_ref, vf_ref, cf_ref, sf_ref,
                qb_ref, kb_ref, vb_ref, cb_ref, sb_ref, of_ref, ob_ref, st_ref, dm_ref, in_ref, tl_ref):
    c = SCAN_CHUNK
    k_scale = HEAD_DIM ** -0.5

    @pl.when(pl.program_id(1) == 0)
    def _():
        st_ref[...] = jnp.zeros_like(st_ref)
        ti = lax.broadcasted_iota(jnp.int32, (c, c), 0)
        si = lax.broadcasted_iota(jnp.int32, (c, c), 1)
        rowf = lax.broadcasted_iota(jnp.int32, (c, HEAD_DIM), 0).astype(F32)
        for d in range(2):
            rel = (ti - si) if d == 0 else (si - ti)
            relf = jnp.maximum(rel, 0).astype(F32)
            for h in range(n_heads):
                lg = lg_ref[d, h]
                dm_ref[d, h] = jnp.where(rel >= 0, jnp.exp(relf * lg), 0.0)
                if d == 0:
                    in_ref[d, h] = jnp.exp((rowf + 1.0) * lg)
                    tl_ref[d, h] = jnp.exp((c - 1.0 - rowf) * lg)
                else:
                    in_ref[d, h] = jnp.exp((c - rowf) * lg)
                    tl_ref[d, h] = jnp.exp(rowf * lg)

    even_lane = (lax.broadcasted_iota(jnp.int32, (c, HEAD_DIM), 1) & 1) == 0
    dirs = ((qf_ref, kf_ref, vf_ref, cf_ref, sf_ref, of_ref), (qb_ref, kb_ref, vb_ref, cb_ref, sb_ref, ob_ref))
    def pairs(d, h, rows):
        q_ref, k_ref, v_ref, c_ref, s_ref, _ = dirs[d]
        sl = slice(h * HEAD_DIM, (h + 1) * HEAD_DIM)
        q = _rope(q_ref[0, 0, rows, sl].astype(F32), c_ref[rows, :], s_ref[rows, :], even_lane)
        k = _rope(k_ref[0, 0, rows, sl].astype(F32), c_ref[rows, :], s_ref[rows, :], even_lane) * k_scale
        qb16 = q.astype(BF16)
        scores = _nt_dot(qb16, k.astype(BF16)) * dm_ref[d, h]
        return dict(d=d, h=h, sl=sl, rows=rows, qb16=qb16, k=k, v=v_ref[0, 0, rows, sl], scores=scores)

    def carry(s):
        d, h, v = s["d"], s["h"], s["v"]
        st = st_ref[d, h]
        out = jnp.dot(s["scores"].astype(BF16), v, preferred_element_type=F32)
        out = out + _nt_dot(s["qb16"], st.astype(BF16)) * in_ref[d, h]
        dirs[d][5][0, s["rows"], s["sl"]] = out.astype(BF16)
        edge = (c - 1) * (1 - d)
        st_ref[d, h] = (st * in_ref[d, h, edge:edge + 1, :]
                        + _tn_dot(v, (s["k"] * tl_ref[d, h]).astype(BF16)))

    units = [(d, h) for d in range(2) for h in range(n_heads)]
    for sub in range(SCAN_SUB):
        rows = _sub_rows(sub)
        for g in range(0, len(units), SCAN_INTERLEAVE):
            states = [pairs(d, h, rows[d]) for d, h in units[g:g + SCAN_INTERLEAVE]]
            for s_ in states:
                carry(s_)


def _ret_scan(p, lg, cos_rep, sin_signed, ctx_len, width):
    b, _, l, _ = p.shape
    c = SCAN_CHUNK
    rows = SCAN_SUB * c
    n_chunks = l // rows
    ctx_chunks = ctx_len // rows
    n_heads = width // HEAD_DIM

    def fwd(g):
        return pl.BlockSpec((1, 1, rows, width), lambda b, j, lg: (b, g, j, 0))

    def bwd(g):
        return pl.BlockSpec((1, 1, rows, width), lambda b, j, lg: (b, g, _bwd_chunk(j, ctx_chunks, n_chunks), 0))

    tab_f = pl.BlockSpec((rows, HEAD_DIM), lambda b, j, lg: (j, 0))
    tab_b = pl.BlockSpec((rows, HEAD_DIM), lambda b, j, lg: (_bwd_chunk(j, ctx_chunks, n_chunks), 0))
    out_shape = jax.ShapeDtypeStruct((b, l, width), BF16)
    return pl.pallas_call(
        functools.partial(_ret_kernel, n_heads),
        grid_spec=pltpu.PrefetchScalarGridSpec(
            num_scalar_prefetch=1,
            grid=(b, n_chunks),
            in_specs=[fwd(P_RQ), fwd(P_RK), fwd(P_RV), tab_f, tab_f,
                      bwd(P_RQ), bwd(P_RK), bwd(P_RV), tab_b, tab_b],
            out_specs=[pl.BlockSpec((1, rows, width), lambda b, j, lg: (b, j, 0)),
                       pl.BlockSpec((1, rows, width),
                                    lambda b, j, lg: (b, _bwd_chunk(j, ctx_chunks, n_chunks), 0))],
            scratch_shapes=[pltpu.VMEM((2, n_heads, HEAD_DIM, HEAD_DIM), F32),
                            pltpu.VMEM((2, n_heads, c, c), F32),
                            pltpu.VMEM((2, n_heads, c, HEAD_DIM), F32),
                            pltpu.VMEM((2, n_heads, c, HEAD_DIM), F32)]),
        out_shape=[out_shape, out_shape],
        compiler_params=_cparams(("arbitrary", "arbitrary")),
        name="ret_scan",
    )(lg, p, p, p, cos_rep, sin_signed, p, p, p, cos_rep, sin_signed)


def _outproj_kernel(n_hg, n_ret, x_ref, hf_ref, hb_ref, rf_ref, rb_ref, hg_ref, rg_ref, mod_ref,
                    on_ref, wo_ref, g2_ref, wrh_ref, wrl_ref, xo_ref, h2_ref, aff_ref):
    mod = mod_ref[0]

    def mixed(rows):
        o_hg = hf_ref[0, rows, :].astype(F32) + hb_ref[0, rows, :].astype(F32)
        o_rt = rf_ref[0, rows, :].astype(F32) + rb_ref[0, rows, :].astype(F32)
        parts = []
        for h in range(n_hg):
            sl = slice(h * HEAD_DIM, (h + 1) * HEAD_DIM)
            parts.append((_rms(o_hg[:, sl]) * on_ref[...]) * _silu(hg_ref[0, 0, rows, sl].astype(F32)))
        for h in range(n_ret):
            sl = slice(h * HEAD_DIM, (h + 1) * HEAD_DIM)
            parts.append(_rms(o_rt[:, sl]) * _silu(rg_ref[0, 0, rows, sl].astype(F32)))
        return jnp.concatenate(parts, axis=1).astype(BF16)

    def residual(rows, mix):
        x = x_ref[0, rows, :] + mod[2:3] * jnp.dot(mix, wo_ref[0], preferred_element_type=F32)
        xo_ref[0, rows, :] = x
        h2 = (_rms(x) * g2_ref[...]) * (1.0 + mod[4:5]) + mod[3:4]
        h2_ref[0, rows, :] = h2.astype(BF16)
        return h2

    def route(i, h2):
        hh = h2.astype(BF16)
        hl = (h2 - hh.astype(F32)).astype(BF16)
        logits = _nt_dot(wrh_ref[...], hh) + _nt_dot(wrh_ref[...], hl) + _nt_dot(wrl_ref[...], hh)
        mx = jnp.max(logits, axis=0, keepdims=True)
        ex = jnp.exp(logits - mx)
        aff_ref[0, i] = ex / jnp.sum(ex, axis=0, keepdims=True)

    blocks = [slice(i * LANES, (i + 1) * LANES) for i in range(aff_ref.shape[1])]
    mixes = [mixed(rows) for rows in blocks]
    h2s = [residual(rows, mix) for rows, mix in zip(blocks, mixes)]
    for i, h2 in enumerate(h2s):
        route(i, h2)


def _outproj(x, hf, hb, rf, rb, p, modtab, onorm, wo_bf16, layer, g2, w_router, ctx_len, hg_width):
    b, l, d = x.shape
    tm = ROW_TILE
    ctx_tiles = ctx_len // tm
    n_hg = hg_width // HEAD_DIM
    ret_width = d - hg_width
    n_ret = ret_width // HEAD_DIM
    n_e = w_router.shape[1]
    wrt = w_router.T
    wrh = wrt.astype(BF16)
    wrl = (wrt - wrh.astype(F32)).astype(BF16)
    row = pl.BlockSpec((1, tm, d), lambda b, t: (b, t, 0))
    hrow = pl.BlockSpec((1, tm, hg_width), lambda b, t: (b, t, 0))
    rrow = pl.BlockSpec((1, tm, ret_width), lambda b, t: (b, t, 0))
    full = lambda shape: pl.BlockSpec(shape, lambda b, t: tuple(0 for _ in shape))
    return pl.pallas_call(
        functools.partial(_outproj_kernel, n_hg, n_ret),
        grid=(b, l // tm),
        in_specs=[row, hrow, hrow, rrow, rrow,
                  pl.BlockSpec((1, 1, tm, hg_width), lambda b, t: (b, P_HGT, t, 0)),
                  pl.BlockSpec((1, 1, tm, ret_width), lambda b, t: (b, P_RGT, t, 0)),
                  _seg_spec(d, ctx_tiles),
                  full((1, HEAD_DIM)), pl.BlockSpec((1, d, d), lambda b, t: (layer, 0, 0)),
                  full((1, d)), full((n_e, d)), full((n_e, d))],
        out_specs=[row, row, pl.BlockSpec((1, tm // LANES, n_e, LANES), lambda b, t: (b, t, 0, 0))],
        out_shape=[jax.ShapeDtypeStruct((b, l, d), F32),
                   jax.ShapeDtypeStruct((b, l, d), BF16),
                   jax.ShapeDtypeStruct((b, l // LANES, n_e, LANES), F32)],
        compiler_params=_cparams(("arbitrary", "arbitrary")),
        name="outproj",
    )(x, hf, hb, rf, rb, p, p, modtab, onorm.reshape(1, HEAD_DIM), wo_bf16, g2.reshape(1, d), wrh, wrl)


FFN_SPLIT = 2
CAST_ROWS = 64


def _ffn_kernel(layer, x_ref, gate_ref, wg_hbm, wu_hbm, wd_hbm, o_ref, sg_ref, su_ref, sd_ref,
                wg_ref, wu_ref, wd_ref, sem):
    e, t, n_e = pl.program_id(0), pl.program_id(1), pl.num_programs(0)
    pairs = ((wg_hbm, sg_ref, wg_ref), (wu_hbm, su_ref, wu_ref), (wd_hbm, sd_ref, wd_ref))

    def fetch(expert):
        return [pltpu.make_async_copy(hbm.at[layer, expert], stage, sem.at[i])
                for i, (hbm, stage, _) in enumerate(pairs)]

    @pl.when((e == 0) & (t == 0))
    def _():
        for cp in fetch(0):
            cp.start()

    @pl.when(t == 0)
    def _():
        for cp in fetch(e):
            cp.wait()
        for _, stage, w16 in pairs:
            def cast(i, carry, stage=stage, w16=w16):
                rows = pl.ds(pl.multiple_of(i * CAST_ROWS, CAST_ROWS), CAST_ROWS)
                w16[rows, :] = stage[rows, :].astype(BF16)
                return carry
            lax.fori_loop(0, stage.shape[0] // CAST_ROWS, cast, 0)

        @pl.when(e + 1 < n_e)
        def _():
            for cp in fetch(e + 1):
                cp.start()

    x = x_ref[0]
    ff = wg_ref.shape[1]
    piece = ff // FFN_SPLIT
    y = jnp.zeros((x.shape[0], wd_ref.shape[1]), F32)
    for i in range(FFN_SPLIT):
        cs = slice(i * piece, (i + 1) * piece)
        a = jnp.dot(x, wg_ref[:, cs], preferred_element_type=F32)
        u = jnp.dot(x, wu_ref[:, cs], preferred_element_type=F32)
        hid = (_silu(a) * u).astype(BF16)
        y = y + jnp.dot(hid, wd_ref[cs, :], preferred_element_type=F32)
    o_ref[0] = (y * gate_ref[0]).astype(BF16)


def _row_tile(rows, cap=512):
    best = 16
    for t in range(16, cap + 1, 16):
        if rows % t == 0:
            best = t
    return best


def _expert_ffn(xs, gates, wg, wu, wd, layer):
    n_e, rows, d = xs.shape
    ff = wg.shape[3]
    tm = _row_tile(rows)
    assert ff % FFN_SPLIT == 0 and d % CAST_ROWS == 0 and ff % CAST_ROWS == 0
    hbm = pl.BlockSpec(memory_space=pl.ANY)
    return pl.pallas_call(
        functools.partial(_ffn_kernel, layer),
        grid=(n_e, rows // tm),
        in_specs=[pl.BlockSpec((1, tm, d), lambda e, t: (e, t, 0)),
                  pl.BlockSpec((1, tm, 1), lambda e, t: (e, t, 0)),
                  hbm, hbm, hbm],
        out_specs=pl.BlockSpec((1, tm, d), lambda e, t: (e, t, 0)),
        out_shape=jax.ShapeDtypeStruct((n_e, rows, d), BF16),
        scratch_shapes=[pltpu.VMEM((d, ff), F32), pltpu.VMEM((d, ff), F32), pltpu.VMEM((ff, d), F32),
                        pltpu.VMEM((d, ff), BF16), pltpu.VMEM((d, ff), BF16), pltpu.VMEM((ff, d), BF16),
                        pltpu.SemaphoreType.DMA((3,))],
        compiler_params=_cparams(("arbitrary", "arbitrary")),
        name="expert_ffn",
    )(xs, gates, wg, wu, wd)


def _final_kernel(first_tile, start_ref, npass_ref, x_ref, arow_ref, ys_ref, expand_ref, lanemod_ref,
                  gt2_ref, g_ref, o_ref, buf_ref, sem):
    moe = _combine_tile(first_tile, start_ref, npass_ref, arow_ref, ys_ref, expand_ref, lanemod_ref, buf_ref, sem)
    x = x_ref[0] + gt2_ref[0][5:6] * moe
    o_ref[0] = _rms(x) * g_ref[...]


def _final(x, comb, modtab, g, ctx_len):
    b, l, d = x.shape
    tm = ROW_TILE
    ctx_tiles = ctx_len // tm
    n_t = (l - ctx_len) // tm
    row = pl.BlockSpec((1, tm, d), lambda b, t, *_: (b, t + ctx_tiles, 0))
    specs, cargs, scratch = _combine_specs(comb, tm, ctx_tiles)
    return pl.pallas_call(
        functools.partial(_final_kernel, ctx_tiles),
        grid_spec=pltpu.PrefetchScalarGridSpec(
            num_scalar_prefetch=2, grid=(b, n_t),
            in_specs=[row] + specs + [pl.BlockSpec((1, 6, d), lambda b, t, *_: (2 * b + 1, 0, 0)),
                                      pl.BlockSpec((1, d), lambda b, t, *_: (0, 0))],
            out_specs=pl.BlockSpec((1, tm, d), lambda b, t, *_: (b, t, 0)),
            scratch_shapes=scratch),
        out_shape=jax.ShapeDtypeStruct((b, l - ctx_len, d), F32),
        compiler_params=_cparams(("arbitrary", "arbitrary")),
        name="final_norm",
    )(comb[0], comb[1], x, *cargs, modtab, g.reshape(1, d))


def _rope_tables(rows, ctx_len):
    n_freq = HEAD_DIM // 4
    inv = ROPE_BASE ** (-jnp.arange(n_freq, dtype=F32) / n_freq)
    r = jnp.repeat(jnp.arange(rows, dtype=F32), GRID_W)
    cc = jnp.tile(jnp.arange(GRID_W, dtype=F32), rows)
    lat = jnp.concatenate([r[:, None] * inv, cc[:, None] * inv], axis=-1)
    ang = jnp.concatenate([jnp.zeros((ctx_len, 2 * n_freq), F32), lat], axis=0)
    cos = jnp.cos(ang)
    sin = jnp.sin(ang)
    cos_rep = jnp.repeat(cos, 2, axis=-1)
    sin_signed = jnp.stack([-sin, sin], axis=-1).reshape(ang.shape[0], HEAD_DIM)
    return cos_rep, sin_signed


def _select_kernel(nb, cap, aff_ref, tri_ref, pos_ref, off_ref, tot_ref, cum_ref):
    n_e = aff_ref.shape[2]
    bits = lax.bitcast_convert_type(aff_ref[0], jnp.int32)
    ones = jnp.ones((LANES, LANES), BF16)
    tri = tri_ref[...]

    def total(flags):
        return jnp.dot(jnp.sum(flags, axis=0).astype(BF16), ones, preferred_element_type=F32)

    def thr_step(i, thr):
        cand = thr | (jnp.int32(1) << (30 - i))
        cnt = total((bits >= cand[None]).astype(F32))
        return jnp.where(cnt >= cap, cand, thr)

    thr = lax.fori_loop(0, 31, thr_step, jnp.zeros((n_e, LANES), jnp.int32))

    def prefix(flags):
        f2 = flags.reshape(nb * n_e, LANES).astype(BF16)
        incl = jnp.dot(f2, tri, preferred_element_type=F32).reshape(nb, n_e, LANES)
        tot_ref[...] = jnp.dot(f2, ones, preferred_element_type=F32).reshape(nb, n_e, LANES)

        lane = lax.broadcasted_iota(jnp.int32, (n_e, LANES), 1)

        def step(k, state):
            carry, by_block = state
            cum_ref[k] = carry
            return carry + tot_ref[k], jnp.where(lane == k, carry, by_block)

        zero = tot_ref[0] * 0.0
        _, by_block = lax.fori_loop(0, nb, step, (zero, zero))
        return incl, cum_ref[...], by_block

    gt = (bits > thr[None]).astype(F32)
    eq = (bits == thr[None]).astype(F32)
    need = cap - total(gt)
    eq_incl, eq_off, _ = prefix(eq)
    sel = gt + eq * ((eq_off + eq_incl - eq) < need[None]).astype(F32)
    incl, off, off_by_block = prefix(sel)
    pos_ref[0] = jnp.where(sel > 0.5, off + incl - 1.0, -1.0)
    off_ref[0] = off_by_block


def _compact_kernel(nb, off_ref, pos_ref, aff_ref, out_ref):
    b = pl.program_id(0)
    n_e = pos_ref.shape[2]
    out_ref[...] = jnp.zeros_like(out_ref)
    slot_row = lax.broadcasted_iota(jnp.int32, (LANES, LANES), 0).astype(F32)
    lane8 = lax.broadcasted_iota(jnp.int32, (8, LANES), 1)
    row8 = lax.broadcasted_iota(jnp.int32, (8, LANES), 0)
    lane_f = lane8.astype(F32)

    def block(k, carry):
        starts = [off_ref[b, k, e] for e in range(n_e)]
        picked = []
        for e in range(n_e):
            blk = starts[e] >> 7
            p = pos_ref[0, k, e:e + 1, :]
            g = aff_ref[0, k, e:e + 1, :]
            r = p - jnp.asarray(blk * LANES, F32)
            r = jnp.where(r >= LANES, r - LANES, r)
            onehot = jnp.where((slot_row == r) & (p >= 0.0), 1.0, 0.0).astype(BF16)
            g_hi = g.astype(BF16).astype(F32)
            g_mid = (g - g_hi).astype(BF16).astype(F32)
            g_lo = (g - g_hi - g_mid).astype(BF16).astype(F32)
            vals = jnp.where(row8 == 0, lane_f,
                             jnp.where(row8 == 1, 1.0,
                                       jnp.where(row8 == 2, g_hi,
                                                 jnp.where(row8 == 3, g_mid,
                                                           jnp.where(row8 == 4, g_lo, 0.0)))))
            picked.append(_nt_dot(vals.astype(BF16), onehot))
        for e in range(n_e):
            c = picked[e]
            blk = starts[e] >> 7
            tok = c[1:2] * jnp.asarray(k * LANES, F32) + c[0:1]
            gate = c[2:3] + c[3:4] + c[4:5]
            tile = jnp.where(row8 == 0, tok, jnp.where(row8 == 1, gate, 0.0))
            first = lane8 >= (starts[e] & (LANES - 1))
            out_ref[0, e, blk] += jnp.where(first, tile, 0.0)
            out_ref[0, e, blk + 1] += jnp.where(first, 0.0, tile)
        return carry

    lax.fori_loop(0, nb, block, 0, unroll=min(4, nb))


def _route(aff_blocks, cap):
    bsz, nb, n_e, _ = aff_blocks.shape
    assert nb <= LANES
    tri = jnp.asarray(np.triu(np.ones((LANES, LANES), np.float32)), BF16)
    blk = pl.BlockSpec((1, nb, n_e, LANES), lambda b: (b, 0, 0, 0))
    pos, off_by_block = pl.pallas_call(
        functools.partial(_select_kernel, nb, cap),
        grid=(bsz,),
        in_specs=[blk, pl.BlockSpec((LANES, LANES), lambda b: (0, 0))],
        out_specs=[blk, pl.BlockSpec((1, n_e, LANES), lambda b: (b, 0, 0))],
        out_shape=[jax.ShapeDtypeStruct((bsz, nb, n_e, LANES), F32),
                   jax.ShapeDtypeStruct((bsz, n_e, LANES), F32)],
        scratch_shapes=[pltpu.VMEM((nb, n_e, LANES), F32), pltpu.VMEM((nb, n_e, LANES), F32)],
        compiler_params=_cparams(("arbitrary",)),
        name="route_select",
    )(aff_blocks, tri)
    off = off_by_block[:, :, :nb].transpose(0, 2, 1).astype(jnp.int32)
    n_rows = cap // LANES + 2
    blk1 = pl.BlockSpec((1, nb, n_e, LANES), lambda b, off: (b, 0, 0, 0))
    lists = pl.pallas_call(
        functools.partial(_compact_kernel, nb),
        grid_spec=pltpu.PrefetchScalarGridSpec(
            num_scalar_prefetch=1,
            grid=(bsz,),
            in_specs=[blk1, blk1],
            out_specs=pl.BlockSpec((1, n_e, n_rows, 8, LANES), lambda b, off: (b, 0, 0, 0, 0))),
        out_shape=jax.ShapeDtypeStruct((bsz, n_e, n_rows, 8, LANES), F32),
        compiler_params=_cparams(("arbitrary",)),
        name="route_compact",
    )(off, pos, aff_blocks)
    tokens = lists[:, :, :, 0, :].reshape(bsz, n_e, n_rows * LANES)[:, :, :cap].astype(jnp.int32)
    gates = lists[:, :, :, 1, :].reshape(bsz, n_e, n_rows * LANES)[:, :, :cap]
    return gates, tokens, pos, off


def kernel(x, c, ctx, c_ctx, w_ada, b_ada, g_mix, g_ffn, w_in, w_out, hg_lb_logits, hg_onorm,
           ret_decay, w_router, w_e_gate, w_e_up, w_e_down, g_final):
    bsz, n, d = x.shape
    lc = ctx.shape[1]
    depth = w_ada.shape[0]
    l = lc + n
    hg_width = hg_lb_logits.shape[1]
    ret_width = d - hg_width
    n_e = w_router.shape[2]
    assert lc % ROW_TILE == 0 and n % ROW_TILE == 0
    assert lc % (SCAN_SUB * SCAN_CHUNK) == 0 and n % (SCAN_SUB * SCAN_CHUNK) == 0
    assert hg_width % HEAD_DIM == 0 and ret_width % HEAD_DIM == 0 and n % GRID_W == 0
    assert w_in.shape[2] == 5 * hg_width + 4 * ret_width and hg_width == ret_width

    cos_rep, sin_signed = _rope_tables(n // GRID_W, lc)
    w_tab, mask_tab, maskq_tab = _decay_tables()

    gamma_cum = jnp.cumsum(jax.nn.softmax(hg_lb_logits.astype(F32), axis=0), axis=0)
    lbs = gamma_cum - gamma_cum[0:1]
    lb_tabs = jnp.stack([jnp.log(lbs), jnp.log1p(-lbs)], axis=1)
    log_gammas = -jnp.exp(ret_decay.astype(F32))

    cond8 = jnp.zeros((8, d), F32).at[:bsz].set(c).at[bsz].set(c_ctx)
    mods = _adaln(cond8, w_ada, b_ada).reshape(depth, 8, 6, d)
    modtabs = jnp.stack([jnp.broadcast_to(mods[:, bsz:bsz + 1], (depth, bsz, 6, d)), mods[:, :bsz]],
                        axis=2).reshape(depth, 2 * bsz, 6, d)

    cap_lat = CAPACITY_FACTOR * n // n_e
    cap_ctx = CAPACITY_FACTOR * lc // n_e
    boff = (jnp.arange(bsz, dtype=jnp.int32) * l)[:, None, None]

    gw = hg_width
    w_in16 = jnp.concatenate([w_in[:, :, gw:3 * gw], w_in[:, :, :gw], w_in[:, :, 3 * gw:]], axis=2).astype(BF16)
    w_out16 = w_out.astype(BF16)
    xs = jnp.concatenate([ctx, x], axis=1)
    comb = None
    for layer in range(depth):
        last = layer == depth - 1
        xs, pf, p = _inproj(xs, comb, modtabs[layer - 1] if layer else None, modtabs[layer], g_mix[layer],
                            w_in16, layer, lc, 2 * hg_width, hg_width)
        hf, hb = _hgrn_scan(pf, p, lb_tabs[layer], w_tab, mask_tab, maskq_tab, lc, hg_width)
        rf, rb = _ret_scan(p, log_gammas[layer], cos_rep, sin_signed, lc, ret_width)
        xs, h2, aff_t = _outproj(xs, hf, hb, rf, rb, p, modtabs[layer], hg_onorm[layer],
                                 w_out16, layer, g_ffn[layer], w_router[layer], lc, hg_width)
        blocks = [lc // LANES, n // LANES]
        caps = [cap_ctx, cap_lat]
        bases = [bsz * cap_lat + jnp.arange(bsz, dtype=jnp.int32) * cap_ctx,
                 jnp.arange(bsz, dtype=jnp.int32) * cap_lat]
        affs = [aff_t[:, :blocks[0]], aff_t[:, blocks[0]:]]
        tok_off = [0, lc]
        rows, gts, arows, starts, cnts = {}, {}, [], [], []
        for seg in (0, 1):
            if seg == 0 and last:
                arows.append(jnp.full((bsz, blocks[0], n_e, LANES), -1.0, F32))
                starts.append(jnp.zeros((bsz, blocks[0], n_e), jnp.int32))
                cnts.append(jnp.zeros((bsz, blocks[0], n_e), jnp.int32))
                continue
            g_s, i_s, pos_s, off_s = _route(affs[seg], caps[seg])
            rows[seg] = i_s + tok_off[seg] + boff
            gts[seg] = g_s
            kept = pos_s >= 0.0
            arows.append(jnp.where(kept, pos_s + bases[seg][:, None, None, None].astype(F32), -1.0))
            starts.append(off_s + bases[seg][:, None, None])
            cnts.append(jnp.sum(kept, axis=-1).astype(jnp.int32))
        order = [s_ for s_ in (1, 0) if s_ in rows]
        flat = jnp.concatenate([rows[s_].transpose(1, 0, 2).reshape(n_e, -1) for s_ in order], axis=1)
        gate = jnp.concatenate([gts[s_].transpose(1, 0, 2).reshape(n_e, -1) for s_ in order], axis=1)
        n_rows = flat.shape[1]
        pad = (-n_rows) % 16
        if pad:
            flat = jnp.pad(flat, ((0, 0), (0, pad)))
            gate = jnp.pad(gate, ((0, 0), (0, pad)))
        h2f = h2.reshape(bsz * l, d)
        gathered = h2f[flat.reshape(-1)].reshape(n_e, n_rows + pad, d)
        ys = _expert_ffn(gathered, gate[:, :, None], w_e_gate, w_e_up, w_e_down, layer)
        arow = jnp.concatenate(arows, axis=1)
        start = jnp.concatenate(starts, axis=1)
        nbt = ROW_TILE // LANES
        start_t = start[:, ::nbt]
        cnt_t = jnp.concatenate(cnts, axis=1).reshape(bsz, -1, nbt, n_e).sum(axis=2)
        need = start_t % YS_ALIGN + cnt_t
        npass = jnp.max(jnp.where(cnt_t > 0, -(-need // COMBINE_ROWS), 0), axis=-1).astype(jnp.int32)
        comb = (start.reshape(bsz, -1), npass, arow, ys)
    return _final(xs, comb, modtabs[depth - 1], g_final, lc)
```

```python
import functools

import numpy as np
import jax
import jax.numpy as jnp
from jax import lax
from jax.experimental import pallas as pl
from jax.experimental.pallas import tpu as pltpu

F32 = jnp.float32
BF16 = jnp.bfloat16

EPS = 1e-6
GRID_W = 64
ROPE_BASE = 10000.0
N_EXPERTS = 16
CAPACITY_FACTOR = 2
HEAD_DIM = 128
LANES = 128
SCAN_CHUNK = 128
ROW_TILE = 256
VMEM_LIMIT = 56 * 1024 * 1024

P_HQ, P_HI, P_HGT, P_RQ, P_RK, P_RV, P_RGT = range(7)

N_LEVELS = int(np.log2(SCAN_CHUNK))
SMALL_HALVES = (4, 2)
N_BIG_LEVELS = N_LEVELS - len(SMALL_HALVES) - 1
W_ROWS = (1 + len(SMALL_HALVES)) * SCAN_CHUNK
LOG2E = float(np.log2(np.e))
SCAN_INTERLEAVE = 4
SCAN_SUB = 2


def _cparams(sem, **kw):
    return pltpu.CompilerParams(dimension_semantics=sem, vmem_limit_bytes=VMEM_LIMIT, **kw)


def _nt_dot(a, b):
    return lax.dot_general(a, b, (((1,), (1,)), ((), ())), preferred_element_type=F32)


def _tn_dot(a, b):
    return lax.dot_general(a, b, (((0,), (0,)), ((), ())), preferred_element_type=F32)


def _sigmoid(x):
    return 1.0 / (1.0 + jnp.exp(-x))


def _silu(x):
    return x * _sigmoid(x)


def _adaln_kernel(cond_ref, w_ref, b_ref, o_ref):
    s = _silu(cond_ref[...])
    o_ref[0] = jnp.dot(s.astype(BF16), w_ref[0].astype(BF16), preferred_element_type=F32) + b_ref[0]


def _adaln(cond8, w_ada, b_ada):
    depth, d, n6 = w_ada.shape
    tn = n6 // 4
    return pl.pallas_call(
        _adaln_kernel,
        grid=(depth, n6 // tn),
        in_specs=[pl.BlockSpec((8, d), lambda l, j: (0, 0)),
                  pl.BlockSpec((1, d, tn), lambda l, j: (l, 0, j)),
                  pl.BlockSpec((1, 1, tn), lambda l, j: (l, 0, j))],
        out_specs=pl.BlockSpec((1, 8, tn), lambda l, j: (l, 0, j)),
        out_shape=jax.ShapeDtypeStruct((depth, 8, n6), F32),
        compiler_params=_cparams(("arbitrary", "arbitrary")),
        name="adaln",
    )(cond8, w_ada, b_ada.reshape(depth, 1, n6))


def _rms(x):
    return x * lax.rsqrt(jnp.mean(x * x, axis=-1, keepdims=True) + EPS)


COMBINE_ROWS = 64
YS_ALIGN = 16


def _combine_tile(first_tile, start_ref, npass_ref, arow_ref, ys_ref, expand_ref, lanemod_ref, buf_ref, sem):
    n_e = arow_ref.shape[2]
    n_blk = arow_ref.shape[1]
    tm = n_blk * LANES
    wn = COMBINE_ROWS
    n_rows = ys_ref.shape[1]
    d = ys_ref.shape[2]
    b, t, n_t = pl.program_id(0), pl.program_id(1), pl.num_programs(1)
    step = b * n_t + t
    n_steps = pl.num_programs(0) * n_t
    slot = step % 2
    wrap = t + 1 == n_t
    b_next = jnp.where(wrap, b + 1, b)
    t_next = jnp.where(wrap, 0, t + 1)

    def window(bb, tt, e, p):
        begin = (start_ref[bb, (tt + first_tile) * n_blk * n_e + e] // YS_ALIGN) * YS_ALIGN
        src = jnp.minimum(begin + p * wn, n_rows - wn)
        return begin, pl.multiple_of(src, YS_ALIGN)

    def copy(to_slot, src, e):
        return pltpu.make_async_copy(ys_ref.at[e, pl.ds(src, wn), :],
                                     buf_ref.at[to_slot, pl.ds(e * wn, wn), :], sem.at[to_slot, e])

    @pl.when(step == 0)
    def _():
        for e in range(n_e):
            copy(slot, window(b, t, e, 0)[1], e).start(priority=e % 2)

    @pl.when(step + 1 < n_steps)
    def _():
        for e in range(n_e):
            copy(1 - slot, window(b_next, t_next, e, 0)[1], e).start(priority=e % 2)

    arow = jnp.concatenate([arow_ref[0, i] for i in range(n_blk)], axis=1)
    erow = lax.broadcasted_iota(jnp.int32, (n_e, tm), 0)

    def per_expert(values):
        out = jnp.zeros((n_e, tm), F32)
        for e in range(n_e):
            out = jnp.where(erow == e, values[e].astype(F32), out)
        return out

    rel0 = arow - per_expert([window(b, t, e, 0)[0] for e in range(n_e)])

    def place(p, srcs, acc):
        lo = jnp.asarray(p * wn, F32)
        mine = (arow >= 0.0) & (rel0 >= lo) & (rel0 < lo + wn)
        rel = jnp.where(mine, arow - per_expert(srcs), 255.0).astype(BF16)
        spread = _tn_dot(rel, expand_ref[...])
        onehot = jnp.where(spread == lanemod_ref[...], 1.0, 0.0).astype(BF16)
        for e in range(n_e):
            copy(slot, srcs[e], e).wait()
        return acc + jnp.dot(onehot, buf_ref[slot], preferred_element_type=F32)

    acc = place(0, [window(b, t, e, 0)[1] for e in range(n_e)], jnp.zeros((tm, d), F32))

    def extra_pass(p, acc):
        srcs = [window(b, t, e, p)[1] for e in range(n_e)]
        for e in range(n_e):
            copy(slot, srcs[e], e).start(priority=e % 2)
        return place(p, srcs, acc)

    return lax.fori_loop(1, npass_ref[b, t + first_tile], extra_pass, acc)


def _inproj_kernel(has_moe, *refs):
    if has_moe:
        (start_ref, npass_ref, x_ref, arow_ref, ys_ref, expand_ref, lanemod_ref, gt2_ref, mod_ref, g_ref, w_ref,
         xo_ref, pf_ref, p_ref, buf_ref, sem) = refs
        moe = _combine_tile(0, start_ref, npass_ref, arow_ref, ys_ref, expand_ref, lanemod_ref, buf_ref, sem)
        x = x_ref[0] + gt2_ref[0][5:6] * moe
        xo_ref[0] = x
    else:
        x_ref, mod_ref, g_ref, w_ref, pf_ref, p_ref = refs
        x = x_ref[0]
    mod = mod_ref[0]
    h = (_rms(x) * g_ref[...]) * (1.0 + mod[1:2]) + mod[0:1]
    res = jnp.dot(h.astype(BF16), w_ref[0], preferred_element_type=F32)
    n_fg, gw = pf_ref.shape[1], pf_ref.shape[3]
    for g in range(n_fg):
        pf_ref[0, g] = res[:, g * gw:(g + 1) * gw]
    for g in range(p_ref.shape[1]):
        p_ref[0, g] = res[:, (n_fg + g) * gw:(n_fg + g + 1) * gw].astype(BF16)


def _seg_spec(d, ctx_tiles):
    return pl.BlockSpec((1, 6, d), lambda b, t, *_: (2 * b + jnp.where(t < ctx_tiles, 0, 1), 0, 0))


def _combine_consts(n_e):
    wn = COMBINE_ROWS
    expand = np.zeros((n_e, n_e * wn), np.float32)
    for e in range(n_e):
        expand[e, e * wn:(e + 1) * wn] = 1.0
    lanemod = (np.arange(n_e * wn) % wn).astype(np.float32)[None, :]
    return jnp.asarray(expand, BF16), jnp.asarray(lanemod, F32)


def _combine_specs(comb, tm, first_tile):
    start, npass, arow, ys = comb
    n_e = arow.shape[2]
    expand, lanemod = _combine_consts(n_e)
    k = n_e * COMBINE_ROWS
    specs = [pl.BlockSpec((1, tm // LANES, n_e, LANES), lambda b, t, *_: (b, t + first_tile, 0, 0)),
             pl.BlockSpec(memory_space=pl.ANY),
             pl.BlockSpec((n_e, k), lambda b, t, *_: (0, 0)),
             pl.BlockSpec((1, k), lambda b, t, *_: (0, 0))]
    scratch = [pltpu.VMEM((2, k, ys.shape[2]), BF16), pltpu.SemaphoreType.DMA((2, n_e))]
    return specs, [arow, ys, expand, lanemod], scratch


def _inproj(x, comb, modtab_prev, modtab, g, w_bf16, layer, ctx_len, n_f32, gw):
    b, l, d = x.shape
    n = w_bf16.shape[2]
    tm = ROW_TILE
    ctx_tiles = ctx_len // tm
    row = pl.BlockSpec((1, tm, d), lambda b, t, *_: (b, t, 0))
    has_moe = comb is not None
    in_specs = [row]
    args = [x]
    scratch = []
    prefetch = []
    if has_moe:
        specs, cargs, scratch = _combine_specs(comb, tm, 0)
        in_specs += specs + [_seg_spec(d, ctx_tiles)]
        args += cargs + [modtab_prev]
        prefetch = [comb[0], comb[1]]
    in_specs += [_seg_spec(d, ctx_tiles),
                 pl.BlockSpec((1, d), lambda b, t, *_: (0, 0)),
                 pl.BlockSpec((1, d, n), lambda b, t, *_: (layer, 0, 0))]
    args += [modtab, g.reshape(1, d), w_bf16]
    n_fg, n_g = n_f32 // gw, (n - n_f32) // gw
    out_specs = [pl.BlockSpec((1, n_fg, tm, gw), lambda b, t, *_: (b, 0, t, 0)),
                 pl.BlockSpec((1, n_g, tm, gw), lambda b, t, *_: (b, 0, t, 0))]
    out_shape = [jax.ShapeDtypeStruct((b, n_fg, l, gw), F32), jax.ShapeDtypeStruct((b, n_g, l, gw), BF16)]
    if has_moe:
        out_specs = [row] + out_specs
        out_shape = [jax.ShapeDtypeStruct((b, l, d), F32)] + out_shape
    res = pl.pallas_call(
        functools.partial(_inproj_kernel, has_moe),
        grid_spec=pltpu.PrefetchScalarGridSpec(
            num_scalar_prefetch=len(prefetch), grid=(b, l // tm),
            in_specs=in_specs, out_specs=out_specs, scratch_shapes=scratch),
        out_shape=out_shape,
        compiler_params=_cparams(("arbitrary", "arbitrary")),
        name="inproj",
    )(*prefetch, *args)
    if has_moe:
        return res[0], res[1], res[2]
    return x, res[0], res[1]


def _bwd_chunk(j, ctx_chunks, n_chunks):
    return jnp.where(j < ctx_chunks, ctx_chunks - 1 - j, n_chunks - 1 + ctx_chunks - j)


def _sub_rows(sub):
    c = SCAN_CHUNK
    return (slice(sub * c, (sub + 1) * c), slice((SCAN_SUB - 1 - sub) * c, (SCAN_SUB - sub) * c))


def _decay_tables():
    c = SCAN_CHUNK
    w = np.zeros((W_ROWS, c), np.float32)
    masks = np.zeros((N_LEVELS + 1, c, c), np.float32)
    for lev in range(N_LEVELS):
        m = c >> (lev + 1)
        for t in range(c):
            g0 = (t // (2 * m)) * 2 * m
            bnd = g0 + m - 1
            if t > bnd:
                masks[lev, t, g0:bnd + 1] = 1.0
            if m in SMALL_HALVES:
                r0 = (1 + SMALL_HALVES.index(m)) * c
                if t > bnd:
                    w[r0 + t, bnd + 1:t + 1] = 1.0
                else:
                    w[r0 + t, t + 1:bnd + 1] = 1.0
    masks[N_LEVELS] = np.eye(c, dtype=np.float32)
    for t in range(c):
        w[t, :t + 1] = 1.0
    w_b = w.reshape(-1, c, c)[:, ::-1, ::-1].reshape(-1, c)
    masks_b = masks[:, ::-1, ::-1]
    both = np.stack([masks, masks_b])
    maskq = np.zeros((2, N_BIG_LEVELS, c // 2, c), np.float32)
    for d in range(2):
        for lev in range(N_BIG_LEVELS):
            m = c >> (lev + 1)
            rows = np.concatenate([np.arange(g0 + m * (1 - d), g0 + m * (1 - d) + m) for g0 in range(0, c, 2 * m)])
            maskq[d, lev] = both[d, lev, rows]
    return jnp.asarray(np.stack([w, w_b]), BF16), jnp.asarray(both, F32), jnp.asarray(maskq, F32)


def _hgrn_kernel(n_heads, qf_ref, ff_ref, if_ref, qb_ref, fb_ref, ib_ref, lb_ref, w_ref, mask_ref, maskq_ref,
                 of_ref, ob_ref, st_ref, a_ref):
    c = SCAN_CHUNK

    @pl.when(pl.program_id(1) == 0)
    def _():
        st_ref[...] = jnp.zeros_like(st_ref)

    dirs = ((qf_ref, ff_ref, if_ref, of_ref), (qb_ref, fb_ref, ib_ref, ob_ref))

    def gates(d, h, rows):
        q_ref, f_ref, i_ref, _ = dirs[d]
        sl = slice(h * HEAD_DIM, (h + 1) * HEAD_DIM)
        qb16 = q_ref[0, 0, rows, sl]
        xf = f_ref[0, 0, rows, sl]
        b_ = lb_ref[1:2, sl] + (jnp.minimum(xf, 0.0) - jnp.log(1.0 + jnp.exp(-jnp.abs(xf))))
        a_ = lb_ref[0:1, sl]
        lf2 = (jnp.maximum(a_, b_) + jnp.log(1.0 + jnp.exp(-jnp.abs(a_ - b_)))) * LOG2E
        f = jnp.exp2(lf2)
        k = 1.0 - f
        hi = lf2.astype(BF16)
        lo = (lf2 - hi.astype(F32)).astype(BF16)
        ex2 = jnp.dot(w_ref[d], jnp.concatenate([hi, lo], axis=1), preferred_element_type=F32)
        ex = ex2[:, :HEAD_DIM] + ex2[:, HEAD_DIM:]
        cum = ex[:c]
        a_ref[d, h] = cum
        return dict(d=d, h=h, sl=sl, rows=rows, qb16=qb16, q=qb16.astype(F32), v=i_ref[0, 0, rows, sl], f=f, k=k,
                    ex=ex, cum=cum)

    def diagonal(s):
        s["kb16"] = s["k"].astype(BF16)
        s["scores"] = _nt_dot(s["qb16"], s["kb16"]) * mask_ref[s["d"], N_LEVELS]

    def level(s, lev):
        d, h, q, k, ex, cum = s["d"], s["h"], s["q"], s["k"], s["ex"], s["cum"]
        m = c >> (lev + 1)
        if m == 1:
            lhs, rhs = (q * s["f"]).astype(BF16), s["kb16"]
        else:
            if m in SMALL_HALVES:
                r0 = (1 + SMALL_HALVES.index(m)) * c
                e = jnp.exp2(ex[r0:r0 + c])
            else:
                parts = []
                for g0 in range(0, c, 2 * m):
                    mid = g0 + m - 1 + d
                    parts.append(cum[g0:g0 + 2 * m] - a_ref[d, h, mid:mid + 1, :])
                e = jnp.exp2(-jnp.abs(jnp.concatenate(parts, axis=0) if len(parts) > 1 else parts[0]))
                off = m * (1 - d)
                lhs = jnp.concatenate([q[g0 + off:g0 + off + m] * e[g0 + off:g0 + off + m]
                                       for g0 in range(0, c, 2 * m)], axis=0).astype(BF16)
                upd = _nt_dot(lhs, (k * e).astype(BF16)) * maskq_ref[d, lev]
                sc, pieces = s["scores"], []
                for i, g0 in enumerate(range(0, c, 2 * m)):
                    halves = [sc[g0:g0 + m], sc[g0 + m:g0 + 2 * m]]
                    halves[1 - d] = halves[1 - d] + upd[i * m:(i + 1) * m]
                    pieces += halves
                s["scores"] = jnp.concatenate(pieces, axis=0)
                return
            lhs, rhs = (q * e).astype(BF16), (k * e).astype(BF16)
        s["scores"] = s["scores"] + _nt_dot(lhs, rhs) * mask_ref[d, lev]

    def carry(s):
        d, h, q, k, v, cum = s["d"], s["h"], s["q"], s["k"], s["v"], s["cum"]
        edge = (c - 1) * (1 - d)
        tot = a_ref[d, h, edge:edge + 1, :]
        st = st_ref[d, h]
        out = jnp.dot(s["scores"].astype(BF16), v, preferred_element_type=F32)
        out = out + _nt_dot((q * jnp.exp2(cum)).astype(BF16), st.astype(BF16))
        dirs[d][3][0, s["rows"], s["sl"]] = out.astype(BF16)
        kdec = (k * jnp.exp2(tot - cum)).astype(BF16)
        st_ref[d, h] = st * jnp.exp2(tot) + _tn_dot(v, kdec)

    units = [(d, h) for d in range(2) for h in range(n_heads)]
    for sub in range(SCAN_SUB):
        rows = _sub_rows(sub)
        for g in range(0, len(units), SCAN_INTERLEAVE):
            states = [gates(d, h, rows[d]) for d, h in units[g:g + SCAN_INTERLEAVE]]
            for s_ in states:
                diagonal(s_)
                for lev in range(N_LEVELS):
                    level(s_, lev)
            for s_ in states:
                carry(s_)


def _hgrn_scan(pf, p, lb_tab, w_tab, mask_tab, maskq_tab, ctx_len, width):
    b, _, l, _ = p.shape
    c = SCAN_CHUNK
    rows = SCAN_SUB * c
    n_chunks = l // rows
    ctx_chunks = ctx_len // rows
    n_heads = width // HEAD_DIM

    def fwd(g):
        return pl.BlockSpec((1, 1, rows, width), lambda b, j: (b, g, j, 0))

    def bwd(g):
        return pl.BlockSpec((1, 1, rows, width), lambda b, j: (b, g, _bwd_chunk(j, ctx_chunks, n_chunks), 0))

    out_shape = jax.ShapeDtypeStruct((b, l, width), BF16)
    return pl.pallas_call(
        functools.partial(_hgrn_kernel, n_heads),
        grid=(b, n_chunks),
        in_specs=[fwd(P_HQ), fwd(0), fwd(P_HI), bwd(P_HQ), bwd(1), bwd(P_HI),
                  pl.BlockSpec((2, width), lambda b, j: (0, 0)),
                  pl.BlockSpec(w_tab.shape, lambda b, j: (0, 0, 0)),
                  pl.BlockSpec(mask_tab.shape, lambda b, j: (0, 0, 0, 0)),
                  pl.BlockSpec(maskq_tab.shape, lambda b, j: (0, 0, 0, 0))],
        out_specs=[pl.BlockSpec((1, rows, width), lambda b, j: (b, j, 0)),
                   pl.BlockSpec((1, rows, width), lambda b, j: (b, _bwd_chunk(j, ctx_chunks, n_chunks), 0))],
        out_shape=[out_shape, out_shape],
        scratch_shapes=[pltpu.VMEM((2, n_heads, HEAD_DIM, HEAD_DIM), F32),
                        pltpu.VMEM((2, n_heads, c, HEAD_DIM), F32)],
        compiler_params=_cparams(("arbitrary", "arbitrary")),
        name="hgrn_scan",
    )(p, pf, p, p, pf, p, lb_tab, w_tab, mask_tab, maskq_tab)


def _rope(z, cos_rep, sin_signed, even_lane):
    partner = jnp.where(even_lane, pltpu.roll(z, LANES - 1, 1), pltpu.roll(z, 1, 1))
    return z * cos_rep + partner * sin_signed


def _ret_kernel(n_heads, lg_ref, qf_ref, kf_ref, vf_ref, cf_ref, sf_ref,
                qb_ref, kb_ref, vb_ref, cb_ref, sb_ref, of_ref, ob_ref, st_ref, dm_ref, in_ref, tl_ref):
    c = SCAN_CHUNK
    k_scale = HEAD_DIM ** -0.5

    @pl.when(pl.program_id(1) == 0)
    def _():
        st_ref[...] = jnp.zeros_like(st_ref)
        ti = lax.broadcasted_iota(jnp.int32, (c, c), 0)
        si = lax.broadcasted_iota(jnp.int32, (c, c), 1)
        rowf = lax.broadcasted_iota(jnp.int32, (c, HEAD_DIM), 0).astype(F32)
        for d in range(2):
            rel = (ti - si) if d == 0 else (si - ti)
            relf = jnp.maximum(rel, 0).astype(F32)
            for h in range(n_heads):
                lg = lg_ref[d, h]
                dm_ref[d, h] = jnp.where(rel >= 0, jnp.exp(relf * lg), 0.0)
                if d == 0:
                    in_ref[d, h] = jnp.exp((rowf + 1.0) * lg)
                    tl_ref[d, h] = jnp.exp((c - 1.0 - rowf) * lg)
                else:
                    in_ref[d, h] = jnp.exp((c - rowf) * lg)
                    tl_ref[d, h] = jnp.exp(rowf * lg)

    even_lane = (lax.broadcasted_iota(jnp.int32, (c, HEAD_DIM), 1) & 1) == 0
    dirs = ((qf_ref, kf_ref, vf_ref, cf_ref, sf_ref, of_ref), (qb_ref, kb_ref, vb_ref, cb_ref, sb_ref, ob_ref))
    def pairs(d, h, rows):
        q_ref, k_ref, v_ref, c_ref, s_ref, _ = dirs[d]
        sl = slice(h * HEAD_DIM, (h + 1) * HEAD_DIM)
        q = _rope(q_ref[0, 0, rows, sl].astype(F32), c_ref[rows, :], s_ref[rows, :], even_lane)
        k = _rope(k_ref[0, 0, rows, sl].astype(F32), c_ref[rows, :], s_ref[rows, :], even_lane) * k_scale
        qb16 = q.astype(BF16)
        scores = _nt_dot(qb16, k.astype(BF16)) * dm_ref[d, h]
        return dict(d=d, h=h, sl=sl, rows=rows, qb16=qb16, k=k, v=v_ref[0, 0, rows, sl], scores=scores)

    def carry(s):
        d, h, v = s["d"], s["h"], s["v"]
        st = st_ref[d, h]
        out = jnp.dot(s["scores"].astype(BF16), v, preferred_element_type=F32)
        out = out + _nt_dot(s["qb16"], st.astype(BF16)) * in_ref[d, h]
        dirs[d][5][0, s["rows"], s["sl"]] = out.astype(BF16)
        edge = (c - 1) * (1 - d)
        st_ref[d, h] = (st * in_ref[d, h, edge:edge + 1, :]
                        + _tn_dot(v, (s["k"] * tl_ref[d, h]).astype(BF16)))

    units = [(d, h) for d in range(2) for h in range(n_heads)]
    for sub in range(SCAN_SUB):
        rows = _sub_rows(sub)
        for g in range(0, len(units), SCAN_INTERLEAVE):
            states = [pairs(d, h, rows[d]) for d, h in units[g:g + SCAN_INTERLEAVE]]
            for s_ in states:
                carry(s_)


def _ret_scan(p, lg, cos_rep, sin_signed, ctx_len, width):
    b, _, l, _ = p.shape
    c = SCAN_CHUNK
    rows = SCAN_SUB * c
    n_chunks = l // rows
    ctx_chunks = ctx_len // rows
    n_heads = width // HEAD_DIM

    def fwd(g):
        return pl.BlockSpec((1, 1, rows, width), lambda b, j, lg: (b, g, j, 0))

    def bwd(g):
        return pl.BlockSpec((1, 1, rows, width), lambda b, j, lg: (b, g, _bwd_chunk(j, ctx_chunks, n_chunks), 0))

    tab_f = pl.BlockSpec((rows, HEAD_DIM), lambda b, j, lg: (j, 0))
    tab_b = pl.BlockSpec((rows, HEAD_DIM), lambda b, j, lg: (_bwd_chunk(j, ctx_chunks, n_chunks), 0))
    out_shape = jax.ShapeDtypeStruct((b, l, width), BF16)
    return pl.pallas_call(
        functools.partial(_ret_kernel, n_heads),
        grid_spec=pltpu.PrefetchScalarGridSpec(
            num_scalar_prefetch=1,
            grid=(b, n_chunks),
            in_specs=[fwd(P_RQ), fwd(P_RK), fwd(P_RV), tab_f, tab_f,
                      bwd(P_RQ), bwd(P_RK), bwd(P_RV), tab_b, tab_b],
            out_specs=[pl.BlockSpec((1, rows, width), lambda b, j, lg: (b, j, 0)),
                       pl.BlockSpec((1, rows, width),
                                    lambda b, j, lg: (b, _bwd_chunk(j, ctx_chunks, n_chunks), 0))],
            scratch_shapes=[pltpu.VMEM((2, n_heads, HEAD_DIM, HEAD_DIM), F32),
                            pltpu.VMEM((2, n_heads, c, c), F32),
                            pltpu.VMEM((2, n_heads, c, HEAD_DIM), F32),
                            pltpu.VMEM((2, n_heads, c, HEAD_DIM), F32)]),
        out_shape=[out_shape, out_shape],
        compiler_params=_cparams(("arbitrary", "arbitrary")),
        name="ret_scan",
    )(lg, p, p, p, cos_rep, sin_signed, p, p, p, cos_rep, sin_signed)


def _outproj_kernel(n_hg, n_ret, x_ref, hf_ref, hb_ref, rf_ref, rb_ref, hg_ref, rg_ref, mod_ref,
                    on_ref, wo_ref, g2_ref, wrh_ref, wrl_ref, xo_ref, h2_ref, aff_ref):
    mod = mod_ref[0]

    def mixed(rows):
        o_hg = hf_ref[0, rows, :].astype(F32) + hb_ref[0, rows, :].astype(F32)
        o_rt = rf_ref[0, rows, :].astype(F32) + rb_ref[0, rows, :].astype(F32)
        parts = []
        for h in range(n_hg):
            sl = slice(h * HEAD_DIM, (h + 1) * HEAD_DIM)
            parts.append((_rms(o_hg[:, sl]) * on_ref[...]) * _silu(hg_ref[0, 0, rows, sl].astype(F32)))
        for h in range(n_ret):
            sl = slice(h * HEAD_DIM, (h + 1) * HEAD_DIM)
            parts.append(_rms(o_rt[:, sl]) * _silu(rg_ref[0, 0, rows, sl].astype(F32)))
        return jnp.concatenate(parts, axis=1).astype(BF16)

    def residual(rows, mix):
        x = x_ref[0, rows, :] + mod[2:3] * jnp.dot(mix, wo_ref[0], preferred_element_type=F32)
        xo_ref[0, rows, :] = x
        h2 = (_rms(x) * g2_ref[...]) * (1.0 + mod[4:5]) + mod[3:4]
        h2_ref[0, rows, :] = h2.astype(BF16)
        return h2

    def route(i, h2):
        hh = h2.astype(BF16)
        hl = (h2 - hh.astype(F32)).astype(BF16)
        logits = _nt_dot(wrh_ref[...], hh) + _nt_dot(wrh_ref[...], hl) + _nt_dot(wrl_ref[...], hh)
        mx = jnp.max(logits, axis=0, keepdims=True)
        ex = jnp.exp(logits - mx)
        aff_ref[0, i] = ex / jnp.sum(ex, axis=0, keepdims=True)

    blocks = [slice(i * LANES, (i + 1) * LANES) for i in range(aff_ref.shape[1])]
    mixes = [mixed(rows) for rows in blocks]
    h2s = [residual(rows, mix) for rows, mix in zip(blocks, mixes)]
    for i, h2 in enumerate(h2s):
        route(i, h2)


def _outproj(x, hf, hb, rf, rb, p, modtab, onorm, wo_bf16, layer, g2, w_router, ctx_len, hg_width):
    b, l, d = x.shape
    tm = ROW_TILE
    ctx_tiles = ctx_len // tm
    n_hg = hg_width // HEAD_DIM
    ret_width = d - hg_width
    n_ret = ret_width // HEAD_DIM
    n_e = w_router.shape[1]
    wrt = w_router.T
    wrh = wrt.astype(BF16)
    wrl = (wrt - wrh.astype(F32)).astype(BF16)
    row = pl.BlockSpec((1, tm, d), lambda b, t: (b, t, 0))
    hrow = pl.BlockSpec((1, tm, hg_width), lambda b, t: (b, t, 0))
    rrow = pl.BlockSpec((1, tm, ret_width), lambda b, t: (b, t, 0))
    full = lambda shape: pl.BlockSpec(shape, lambda b, t: tuple(0 for _ in shape))
    return pl.pallas_call(
        functools.partial(_outproj_kernel, n_hg, n_ret),
        grid=(b, l // tm),
        in_specs=[row, hrow, hrow, rrow, rrow,
                  pl.BlockSpec((1, 1, tm, hg_width), lambda b, t: (b, P_HGT, t, 0)),
                  pl.BlockSpec((1, 1, tm, ret_width), lambda b, t: (b, P_RGT, t, 0)),
                  _seg_spec(d, ctx_tiles),
                  full((1, HEAD_DIM)), pl.BlockSpec((1, d, d), lambda b, t: (layer, 0, 0)),
                  full((1, d)), full((n_e, d)), full((n_e, d))],
        out_specs=[row, row, pl.BlockSpec((1, tm // LANES, n_e, LANES), lambda b, t: (b, t, 0, 0))],
        out_shape=[jax.ShapeDtypeStruct((b, l, d), F32),
                   jax.ShapeDtypeStruct((b, l, d), BF16),
                   jax.ShapeDtypeStruct((b, l // LANES, n_e, LANES), F32)],
        compiler_params=_cparams(("arbitrary", "arbitrary")),
        name="outproj",
    )(x, hf, hb, rf, rb, p, p, modtab, onorm.reshape(1, HEAD_DIM), wo_bf16, g2.reshape(1, d), wrh, wrl)


FFN_SPLIT = 2
CAST_ROWS = 64


def _ffn_kernel(layer, x_ref, gate_ref, wg_hbm, wu_hbm, wd_hbm, o_ref, sg_ref, su_ref, sd_ref,
                wg_ref, wu_ref, wd_ref, sem):
    e, t, n_e = pl.program_id(0), pl.program_id(1), pl.num_programs(0)
    pairs = ((wg_hbm, sg_ref, wg_ref), (wu_hbm, su_ref, wu_ref), (wd_hbm, sd_ref, wd_ref))

    def fetch(expert):
        return [pltpu.make_async_copy(hbm.at[layer, expert], stage, sem.at[i])
                for i, (hbm, stage, _) in enumerate(pairs)]

    @pl.when((e == 0) & (t == 0))
    def _():
        for cp in fetch(0):
            cp.start()

    @pl.when(t == 0)
    def _():
        for cp in fetch(e):
            cp.wait()
        for _, stage, w16 in pairs:
            def cast(i, carry, stage=stage, w16=w16):
                rows = pl.ds(pl.multiple_of(i * CAST_ROWS, CAST_ROWS), CAST_ROWS)
                w16[rows, :] = stage[rows, :].astype(BF16)
                return carry
            lax.fori_loop(0, stage.shape[0] // CAST_ROWS, cast, 0)

        @pl.when(e + 1 < n_e)
        def _():
            for cp in fetch(e + 1):
                cp.start()

    x = x_ref[0]
    ff = wg_ref.shape[1]
    piece = ff // FFN_SPLIT
    y = jnp.zeros((x.shape[0], wd_ref.shape[1]), F32)
    for i in range(FFN_SPLIT):
        cs = slice(i * piece, (i + 1) * piece)
        a = jnp.dot(x, wg_ref[:, cs], preferred_element_type=F32)
        u = jnp.dot(x, wu_ref[:, cs], preferred_element_type=F32)
        hid = (_silu(a) * u).astype(BF16)
        y = y + jnp.dot(hid, wd_ref[cs, :], preferred_element_type=F32)
    o_ref[0] = (y * gate_ref[0]).astype(BF16)


def _row_tile(rows, cap=512):
    best = 16
    for t in range(16, cap + 1, 16):
        if rows % t == 0:
            best = t
    return best


def _expert_ffn(xs, gates, wg, wu, wd, layer):
    n_e, rows, d = xs.shape
    ff = wg.shape[3]
    tm = _row_tile(rows)
    assert ff % FFN_SPLIT == 0 and d % CAST_ROWS == 0 and ff % CAST_ROWS == 0
    hbm = pl.BlockSpec(memory_space=pl.ANY)
    return pl.pallas_call(
        functools.partial(_ffn_kernel, layer),
        grid=(n_e, rows // tm),
        in_specs=[pl.BlockSpec((1, tm, d), lambda e, t: (e, t, 0)),
                  pl.BlockSpec((1, tm, 1), lambda e, t: (e, t, 0)),
                  hbm, hbm, hbm],
        out_specs=pl.BlockSpec((1, tm, d), lambda e, t: (e, t, 0)),
        out_shape=jax.ShapeDtypeStruct((n_e, rows, d), BF16),
        scratch_shapes=[pltpu.VMEM((d, ff), F32), pltpu.VMEM((d, ff), F32), pltpu.VMEM((ff, d), F32),
                        pltpu.VMEM((d, ff), BF16), pltpu.VMEM((d, ff), BF16), pltpu.VMEM((ff, d), BF16),
                        pltpu.SemaphoreType.DMA((3,))],
        compiler_params=_cparams(("arbitrary", "arbitrary")),
        name="expert_ffn",
    )(xs, gates, wg, wu, wd)


def _final_kernel(first_tile, start_ref, npass_ref, x_ref, arow_ref, ys_ref, expand_ref, lanemod_ref,
                  gt2_ref, g_ref, o_ref, buf_ref, sem):
    moe = _combine_tile(first_tile, start_ref, npass_ref, arow_ref, ys_ref, expand_ref, lanemod_ref, buf_ref, sem)
    x = x_ref[0] + gt2_ref[0][5:6] * moe
    o_ref[0] = _rms(x) * g_ref[...]


def _final(x, comb, modtab, g, ctx_len):
    b, l, d = x.shape
    tm = ROW_TILE
    ctx_tiles = ctx_len // tm
    n_t = (l - ctx_len) // tm
    row = pl.BlockSpec((1, tm, d), lambda b, t, *_: (b, t + ctx_tiles, 0))
    specs, cargs, scratch = _combine_specs(comb, tm, ctx_tiles)
    return pl.pallas_call(
        functools.partial(_final_kernel, ctx_tiles),
        grid_spec=pltpu.PrefetchScalarGridSpec(
            num_scalar_prefetch=2, grid=(b, n_t),
            in_specs=[row] + specs + [pl.BlockSpec((1, 6, d), lambda b, t, *_: (2 * b + 1, 0, 0)),
                                      pl.BlockSpec((1, d), lambda b, t, *_: (0, 0))],
            out_specs=pl.BlockSpec((1, tm, d), lambda b, t, *_: (b, t, 0)),
            scratch_shapes=scratch),
        out_shape=jax.ShapeDtypeStruct((b, l - ctx_len, d), F32),
        compiler_params=_cparams(("arbitrary", "arbitrary")),
        name="final_norm",
    )(comb[0], comb[1], x, *cargs, modtab, g.reshape(1, d))


def _rope_tables(rows, ctx_len):
    n_freq = HEAD_DIM // 4
    inv = ROPE_BASE ** (-jnp.arange(n_freq, dtype=F32) / n_freq)
    r = jnp.repeat(jnp.arange(rows, dtype=F32), GRID_W)
    cc = jnp.tile(jnp.arange(GRID_W, dtype=F32), rows)
    lat = jnp.concatenate([r[:, None] * inv, cc[:, None] * inv], axis=-1)
    ang = jnp.concatenate([jnp.zeros((ctx_len, 2 * n_freq), F32), lat], axis=0)
    cos = jnp.cos(ang)
    sin = jnp.sin(ang)
    cos_rep = jnp.repeat(cos, 2, axis=-1)
    sin_signed = jnp.stack([-sin, sin], axis=-1).reshape(ang.shape[0], HEAD_DIM)
    return cos_rep, sin_signed


def _select_kernel(nb, cap, aff_ref, tri_ref, pos_ref, off_ref, tot_ref, cum_ref):
    n_e = aff_ref.shape[2]
    bits = lax.bitcast_convert_type(aff_ref[0], jnp.int32)
    ones = jnp.ones((LANES, LANES), BF16)
    tri = tri_ref[...]

    def total(flags):
        return jnp.dot(jnp.sum(flags, axis=0).astype(BF16), ones, preferred_element_type=F32)

    def thr_step(i, thr):
        cand = thr | (jnp.int32(1) << (30 - i))
        cnt = total((bits >= cand[None]).astype(F32))
        return jnp.where(cnt >= cap, cand, thr)

    thr = lax.fori_loop(0, 31, thr_step, jnp.zeros((n_e, LANES), jnp.int32))

    def prefix(flags):
        f2 = flags.reshape(nb * n_e, LANES).astype(BF16)
        incl = jnp.dot(f2, tri, preferred_element_type=F32).reshape(nb, n_e, LANES)
        tot_ref[...] = jnp.dot(f2, ones, preferred_element_type=F32).reshape(nb, n_e, LANES)

        lane = lax.broadcasted_iota(jnp.int32, (n_e, LANES), 1)

        def step(k, state):
            carry, by_block = state
            cum_ref[k] = carry
            return carry + tot_ref[k], jnp.where(lane == k, carry, by_block)

        zero = tot_ref[0] * 0.0
        _, by_block = lax.fori_loop(0, nb, step, (zero, zero))
        return incl, cum_ref[...], by_block

    gt = (bits > thr[None]).astype(F32)
    eq = (bits == thr[None]).astype(F32)
    need = cap - total(gt)
    eq_incl, eq_off, _ = prefix(eq)
    sel = gt + eq * ((eq_off + eq_incl - eq) < need[None]).astype(F32)
    incl, off, off_by_block = prefix(sel)
    pos_ref[0] = jnp.where(sel > 0.5, off + incl - 1.0, -1.0)
    off_ref[0] = off_by_block


def _compact_kernel(nb, off_ref, pos_ref, aff_ref, out_ref):
    b = pl.program_id(0)
    n_e = pos_ref.shape[2]
    out_ref[...] = jnp.zeros_like(out_ref)
    slot_row = lax.broadcasted_iota(jnp.int32, (LANES, LANES), 0).astype(F32)
    lane8 = lax.broadcasted_iota(jnp.int32, (8, LANES), 1)
    row8 = lax.broadcasted_iota(jnp.int32, (8, LANES), 0)
    lane_f = lane8.astype(F32)

    def block(k, carry):
        starts = [off_ref[b, k, e] for e in range(n_e)]
        picked = []
        for e in range(n_e):
            blk = starts[e] >> 7
            p = pos_ref[0, k, e:e + 1, :]
            g = aff_ref[0, k, e:e + 1, :]
            r = p - jnp.asarray(blk * LANES, F32)
            r = jnp.where(r >= LANES, r - LANES, r)
            onehot = jnp.where((slot_row == r) & (p >= 0.0), 1.0, 0.0).astype(BF16)
            g_hi = g.astype(BF16).astype(F32)
            g_mid = (g - g_hi).astype(BF16).astype(F32)
            g_lo = (g - g_hi - g_mid).astype(BF16).astype(F32)
            vals = jnp.where(row8 == 0, lane_f,
                             jnp.where(row8 == 1, 1.0,
                                       jnp.where(row8 == 2, g_hi,
                                                 jnp.where(row8 == 3, g_mid,
                                                           jnp.where(row8 == 4, g_lo, 0.0)))))
            picked.append(_nt_dot(vals.astype(BF16), onehot))
        for e in range(n_e):
            c = picked[e]
            blk = starts[e] >> 7
            tok = c[1:2] * jnp.asarray(k * LANES, F32) + c[0:1]
            gate = c[2:3] + c[3:4] + c[4:5]
            tile = jnp.where(row8 == 0, tok, jnp.where(row8 == 1, gate, 0.0))
            first = lane8 >= (starts[e] & (LANES - 1))
            out_ref[0, e, blk] += jnp.where(first, tile, 0.0)
            out_ref[0, e, blk + 1] += jnp.where(first, 0.0, tile)
        return carry

    lax.fori_loop(0, nb, block, 0, unroll=min(4, nb))


def _route(aff_blocks, cap):
    bsz, nb, n_e, _ = aff_blocks.shape
    assert nb <= LANES
    tri = jnp.asarray(np.triu(np.ones((LANES, LANES), np.float32)), BF16)
    blk = pl.BlockSpec((1, nb, n_e, LANES), lambda b: (b, 0, 0, 0))
    pos, off_by_block = pl.pallas_call(
        functools.partial(_select_kernel, nb, cap),
        grid=(bsz,),
        in_specs=[blk, pl.BlockSpec((LANES, LANES), lambda b: (0, 0))],
        out_specs=[blk, pl.BlockSpec((1, n_e, LANES), lambda b: (b, 0, 0))],
        out_shape=[jax.ShapeDtypeStruct((bsz, nb, n_e, LANES), F32),
                   jax.ShapeDtypeStruct((bsz, n_e, LANES), F32)],
        scratch_shapes=[pltpu.VMEM((nb, n_e, LANES), F32), pltpu.VMEM((nb, n_e, LANES), F32)],
        compiler_params=_cparams(("arbitrary",)),
        name="route_select",
    )(aff_blocks, tri)
    off = off_by_block[:, :, :nb].transpose(0, 2, 1).astype(jnp.int32)
    n_rows = cap // LANES + 2
    blk1 = pl.BlockSpec((1, nb, n_e, LANES), lambda b, off: (b, 0, 0, 0))
    lists = pl.pallas_call(
        functools.partial(_compact_kernel, nb),
        grid_spec=pltpu.PrefetchScalarGridSpec(
            num_scalar_prefetch=1,
            grid=(bsz,),
            in_specs=[blk1, blk1],
            out_specs=pl.BlockSpec((1, n_e, n_rows, 8, LANES), lambda b, off: (b, 0, 0, 0, 0))),
        out_shape=jax.ShapeDtypeStruct((bsz, n_e, n_rows, 8, LANES), F32),
        compiler_params=_cparams(("arbitrary",)),
        name="route_compact",
    )(off, pos, aff_blocks)
    tokens = lists[:, :, :, 0, :].reshape(bsz, n_e, n_rows * LANES)[:, :, :cap].astype(jnp.int32)
    gates = lists[:, :, :, 1, :].reshape(bsz, n_e, n_rows * LANES)[:, :, :cap]
    return gates, tokens, pos, off


def kernel(x, c, ctx, c_ctx, w_ada, b_ada, g_mix, g_ffn, w_in, w_out, hg_lb_logits, hg_onorm,
           ret_decay, w_router, w_e_gate, w_e_up, w_e_down, g_final):
    bsz, n, d = x.shape
    lc = ctx.shape[1]
    depth = w_ada.shape[0]
    l = lc + n
    hg_width = hg_lb_logits.shape[1]
    ret_width = d - hg_width
    n_e = w_router.shape[2]
    assert lc % ROW_TILE == 0 and n % ROW_TILE == 0
    assert lc % (SCAN_SUB * SCAN_CHUNK) == 0 and n % (SCAN_SUB * SCAN_CHUNK) == 0
    assert hg_width % HEAD_DIM == 0 and ret_width % HEAD_DIM == 0 and n % GRID_W == 0
    assert w_in.shape[2] == 5 * hg_width + 4 * ret_width and hg_width == ret_width

    cos_rep, sin_signed = _rope_tables(n // GRID_W, lc)
    w_tab, mask_tab, maskq_tab = _decay_tables()

    gamma_cum = jnp.cumsum(jax.nn.softmax(hg_lb_logits.astype(F32), axis=0), axis=0)
    lbs = gamma_cum - gamma_cum[0:1]
    lb_tabs = jnp.stack([jnp.log(lbs), jnp.log1p(-lbs)], axis=1)
    log_gammas = -jnp.exp(ret_decay.astype(F32))

    cond8 = jnp.zeros((8, d), F32).at[:bsz].set(c).at[bsz].set(c_ctx)
    mods = _adaln(cond8, w_ada, b_ada).reshape(depth, 8, 6, d)
    modtabs = jnp.stack([jnp.broadcast_to(mods[:, bsz:bsz + 1], (depth, bsz, 6, d)), mods[:, :bsz]],
                        axis=2).reshape(depth, 2 * bsz, 6, d)

    cap_lat = CAPACITY_FACTOR * n // n_e
    cap_ctx = CAPACITY_FACTOR * lc // n_e
    boff = (jnp.arange(bsz, dtype=jnp.int32) * l)[:, None, None]

    gw = hg_width
    w_in16 = jnp.concatenate([w_in[:, :, gw:3 * gw], w_in[:, :, :gw], w_in[:, :, 3 * gw:]], axis=2).astype(BF16)
    w_out16 = w_out.astype(BF16)
    xs = jnp.concatenate([ctx, x], axis=1)
    comb = None
    for layer in range(depth):
        last = layer == depth - 1
        xs, pf, p = _inproj(xs, comb, modtabs[layer - 1] if layer else None, modtabs[layer], g_mix[layer],
                            w_in16, layer, lc, 2 * hg_width, hg_width)
        hf, hb = _hgrn_scan(pf, p, lb_tabs[layer], w_tab, mask_tab, maskq_tab, lc, hg_width)
        rf, rb = _ret_scan(p, log_gammas[layer], cos_rep, sin_signed, lc, ret_width)
        xs, h2, aff_t = _outproj(xs, hf, hb, rf, rb, p, modtabs[layer], hg_onorm[layer],
                                 w_out16, layer, g_ffn[layer], w_router[layer], lc, hg_width)
        blocks = [lc // LANES, n // LANES]
        caps = [cap_ctx, cap_lat]
        bases = [bsz * cap_lat + jnp.arange(bsz, dtype=jnp.int32) * cap_ctx,
                 jnp.arange(bsz, dtype=jnp.int32) * cap_lat]
        affs = [aff_t[:, :blocks[0]], aff_t[:, blocks[0]:]]
        tok_off = [0, lc]
        rows, gts, arows, starts, cnts = {}, {}, [], [], []
        for seg in (0, 1):
            if seg == 0 and last:
                arows.append(jnp.full((bsz, blocks[0], n_e, LANES), -1.0, F32))
                starts.append(jnp.zeros((bsz, blocks[0], n_e), jnp.int32))
                cnts.append(jnp.zeros((bsz, blocks[0], n_e), jnp.int32))
                continue
            g_s, i_s, pos_s, off_s = _route(affs[seg], caps[seg])
            rows[seg] = i_s + tok_off[seg] + boff
            gts[seg] = g_s
            kept = pos_s >= 0.0
            arows.append(jnp.where(kept, pos_s + bases[seg][:, None, None, None].astype(F32), -1.0))
            starts.append(off_s + bases[seg][:, None, None])
            cnts.append(jnp.sum(kept, axis=-1).astype(jnp.int32))
        order = [s_ for s_ in (1, 0) if s_ in rows]
        flat = jnp.concatenate([rows[s_].transpose(1, 0, 2).reshape(n_e, -1) for s_ in order], axis=1)
        gate = jnp.concatenate([gts[s_].transpose(1, 0, 2).reshape(n_e, -1) for s_ in order], axis=1)
        n_rows = flat.shape[1]
        pad = (-n_rows) % 16
        if pad:
            flat = jnp.pad(flat, ((0, 0), (0, pad)))
            gate = jnp.pad(gate, ((0, 0), (0, pad)))
        h2f = h2.reshape(bsz * l, d)
        gathered = h2f[flat.reshape(-1)].reshape(n_e, n_rows + pad, d)
        ys = _expert_ffn(gathered, gate[:, :, None], w_e_gate, w_e_up, w_e_down, layer)
        arow = jnp.concatenate(arows, axis=1)
        start = jnp.concatenate(starts, axis=1)
        nbt = ROW_TILE // LANES
        start_t = start[:, ::nbt]
        cnt_t = jnp.concatenate(cnts, axis=1).reshape(bsz, -1, nbt, n_e).sum(axis=2)
        need = start_t % YS_ALIGN + cnt_t
        npass = jnp.max(jnp.where(cnt_t > 0, -(-need // COMBINE_ROWS), 0), axis=-1).astype(jnp.int32)
        comb = (start.reshape(bsz, -1), npass, arow, ys)
    return _final(xs, comb, modtabs[depth - 1], g_final, lc)
```
